```python
import jax, jax.numpy as jnp
from jax import lax
import numpy as np

D_MODEL = 4096
BATCH = 4
SEQ = 2048
DEPTH = 2
DEC_BATCH = 8
DEC_SEQ = 8
PAST_LEN = 16384
PAGE_SIZE = 128

HEAD_DIM = 128
N_ATT_HEADS = D_MODEL // (2 * HEAD_DIM)
D_ATT = N_ATT_HEADS * HEAD_DIM
D_SGU = D_MODEL - D_ATT
SGU_GROUP_DIM = 128
N_SGU_GROUPS = D_SGU // SGU_GROUP_DIM
CHUNK = 128
DILATED_BRANCHES = ((128, 1), (512, 4), (2048, 16))
WINDOW_MAX = 2048
D_IN = 3 * D_ATT + 2 * D_SGU
D_FF = -(-8 * D_MODEL // (3 * 256)) * 256
Q_BLOCK = 64
RMS_EPS = 1e-6

kernel_name = "hymba_dilated_attn_sgu_decoder_step"


def rmsnorm(x, gain):
    xf = x.astype(jnp.float32)
    r = xf * lax.rsqrt(jnp.mean(xf * xf, axis=-1, keepdims=True) + RMS_EPS)
    return (r * gain.astype(jnp.float32)).astype(x.dtype)


def alibi_slopes(n_heads):
    return jnp.exp2(-8.0 * jnp.arange(1, n_heads + 1, dtype=jnp.float32) / n_heads)


def dilated_mixture_attention(q, k_ext, v_ext, q_idx):
    slopes = alibi_slopes(q.shape[2])
    scale = HEAD_DIM ** -0.5
    outs, lses = [], []
    for window, dil in DILATED_BRANCHES:
        dist = jnp.arange(0, window + 1, dil, dtype=jnp.int32)
        idx = q_idx[:, None] - dist[None, :]
        valid = idx >= 0
        idx = jnp.maximum(idx, 0)
        kg = jnp.take(k_ext, idx, axis=1)
        vg = jnp.take(v_ext, idx, axis=1)
        s = jnp.einsum('bqhd,bqkhd->bqhk', q, kg,
                       preferred_element_type=jnp.float32) * scale
        s = s - slopes[:, None] * dist.astype(jnp.float32)[None, :]
        s = jnp.where(valid[None, :, None, :], s, -jnp.inf)
        lse = jax.nn.logsumexp(s, axis=-1)
        p = jnp.exp(s - lse[..., None])
        o = jnp.einsum('bqhk,bqkhd->bqhd', p, vg.astype(jnp.float32))
        outs.append(o)
        lses.append(lse)
    w = jax.nn.softmax(jnp.stack(lses, axis=0), axis=0)
    out = jnp.sum(w[..., None] * jnp.stack(outs, axis=0), axis=0)
    return out.astype(q.dtype)


def prompt_attention(q, k, v):
    B, S, H, Dh = q.shape
    n_blk = S // Q_BLOCK

    def block(i):
        start = i * Q_BLOCK
        qb = lax.dynamic_slice_in_dim(q, start, Q_BLOCK, axis=1)
        q_idx = start + jnp.arange(Q_BLOCK, dtype=jnp.int32)
        return dilated_mixture_attention(qb, k, v, q_idx)

    o = lax.map(block, jnp.arange(n_blk, dtype=jnp.int32))
    return jnp.moveaxis(o, 0, 1).reshape(B, S, H, Dh)


def mixer_projections(h, w_in, q_gain, k_gain, sgu_gain):
    B, T, _ = h.shape
    z = h @ w_in
    q, k, v, u, g = jnp.split(z, [D_ATT, 2 * D_ATT, 3 * D_ATT, 3 * D_ATT + D_SGU], axis=-1)
    q = rmsnorm(q.reshape(B, T, N_ATT_HEADS, HEAD_DIM), q_gain)
    k = rmsnorm(k.reshape(B, T, N_ATT_HEADS, HEAD_DIM), k_gain)
    v = v.reshape(B, T, N_ATT_HEADS, HEAD_DIM)
    u = u.reshape(B, T, N_SGU_GROUPS, SGU_GROUP_DIM)
    g = rmsnorm(g.reshape(B, T, N_SGU_GROUPS, SGU_GROUP_DIM), sgu_gain)
    return q, k, v, u, g


def spatial_gate(g, w_s, b_s):
    L = g.shape[2]
    w = jnp.tril(w_s[:, :L, :L])
    s = jnp.einsum('gij,bnjgc->bnigc', w, g)
    return s + b_s[:, :L].T[None, None, :, :, None]


def mixer_output(attn, u, gate, w_out):
    B, T = attn.shape[:2]
    y = jnp.concatenate([attn.reshape(B, T, D_ATT), (u * gate).reshape(B, T, D_SGU)], axis=-1)
    return y @ w_out


def swiglu(h, w_gate, w_up, w_down):
    return (jax.nn.silu(h @ w_gate) * (h @ w_up)) @ w_down


def setup_inputs(seed: int = 0) -> dict:
    key = jax.random.key(seed)
    ks = jax.random.split(key, 16)
    f32 = jnp.float32
    win_buf = min(WINDOW_MAX, PAST_LEN)

    def nrm(k, shape, scale):
        return jax.random.normal(k, shape, f32) * scale

    return {
        "x_prompt": nrm(ks[0], (BATCH, SEQ, D_MODEL), 1.0),
        "x_sample": nrm(ks[1], (DEC_BATCH, DEC_SEQ, D_MODEL), 1.0),
        "cache_k_win": nrm(ks[2], (DEPTH, DEC_BATCH, win_buf, N_ATT_HEADS, HEAD_DIM), 1.0),
        "cache_v_win": nrm(ks[3], (DEPTH, DEC_BATCH, win_buf, N_ATT_HEADS, HEAD_DIM), 1.0),
        "norm1": 1.0 + nrm(ks[4], (DEPTH, D_MODEL), 0.1),
        "w_in": nrm(ks[5], (DEPTH, D_MODEL, D_IN), D_MODEL ** -0.5),
        "q_gain": 1.0 + nrm(ks[6], (DEPTH, HEAD_DIM), 0.1),
        "k_gain": 1.0 + nrm(ks[7], (DEPTH, HEAD_DIM), 0.1),
        "sgu_gain": 1.0 + nrm(ks[8], (DEPTH, N_SGU_GROUPS, SGU_GROUP_DIM), 0.1),
        "w_spatial": nrm(ks[9], (DEPTH, N_SGU_GROUPS, CHUNK, CHUNK), CHUNK ** -0.5),
        "b_spatial": 1.0 + nrm(ks[10], (DEPTH, N_SGU_GROUPS, CHUNK), 0.1),
        "w_out": nrm(ks[11], (DEPTH, D_MODEL, D_MODEL), D_MODEL ** -0.5),
        "norm2": 1.0 + nrm(ks[12], (DEPTH, D_MODEL), 0.1),
        "w_gate": nrm(ks[13], (DEPTH, D_MODEL, D_FF), D_MODEL ** -0.5),
        "w_up": nrm(ks[14], (DEPTH, D_MODEL, D_FF), D_MODEL ** -0.5),
        "w_down": nrm(ks[15], (DEPTH, D_FF, D_MODEL), D_FF ** -0.5),
    }


def reference(x_prompt, x_sample, cache_k_win, cache_v_win, norm1, w_in, q_gain, k_gain,
              sgu_gain, w_spatial, b_spatial, w_out, norm2, w_gate, w_up, w_down):
    xp, xs = x_prompt, x_sample
    Bp, Sp, _ = xp.shape
    Bs, Ts, _ = xs.shape
    win_buf = cache_k_win.shape[2]
    q_idx_s = win_buf + jnp.arange(Ts, dtype=jnp.int32)
    kp_rows, vp_rows, ks_rows, vs_rows, gs_rows = [], [], [], [], []
    for l in range(DEPTH):
        hp = rmsnorm(xp, norm1[l])
        hs = rmsnorm(xs, norm1[l])
        qp, kp, vp, up, gp = mixer_projections(hp, w_in[l], q_gain[l], k_gain[l], sgu_gain[l])
        qs, ks, vs, us, gs = mixer_projections(hs, w_in[l], q_gain[l], k_gain[l], sgu_gain[l])

        att_p = prompt_attention(qp, kp, vp)
        k_ext = jnp.concatenate([cache_k_win[l], ks], axis=1)
        v_ext = jnp.concatenate([cache_v_win[l], vs], axis=1)
        att_s = dilated_mixture_attention(qs, k_ext, v_ext, q_idx_s)

        gate_p = spatial_gate(gp.reshape(Bp, Sp // CHUNK, CHUNK, N_SGU_GROUPS, SGU_GROUP_DIM),
                              w_spatial[l], b_spatial[l]).reshape(Bp, Sp, N_SGU_GROUPS, SGU_GROUP_DIM)
        gate_s = spatial_gate(gs[:, None], w_spatial[l], b_spatial[l])[:, 0]

        xp = xp + mixer_output(att_p, up, gate_p, w_out[l])
        xs = xs + mixer_output(att_s, us, gate_s, w_out[l])
        xp = xp + swiglu(rmsnorm(xp, norm2[l]), w_gate[l], w_up[l], w_down[l])
        xs = xs + swiglu(rmsnorm(xs, norm2[l]), w_gate[l], w_up[l], w_down[l])

        kp_rows.append(kp[:, -WINDOW_MAX:])
        vp_rows.append(vp[:, -WINDOW_MAX:])
        ks_rows.append(ks)
        vs_rows.append(vs)
        gs_rows.append(gs)
    k_win_prompt = jnp.stack(kp_rows, axis=0)
    v_win_prompt = jnp.stack(vp_rows, axis=0)
    k_new_sample = jnp.stack(ks_rows, axis=0)
    v_new_sample = jnp.stack(vs_rows, axis=0)
    sgu_v_sample = jnp.stack(gs_rows, axis=0)
    return (xp, xs, k_win_prompt, v_win_prompt, k_new_sample, v_new_sample, sgu_v_sample)
```

```python
import functools

import jax
import jax.numpy as jnp
from jax import lax
from jax.experimental import pallas as pl
from jax.experimental.pallas import tpu as pltpu

F32 = jnp.float32
BF16 = jnp.bfloat16

D_MODEL = 4096
HEAD_DIM = 128
N_HEADS = 16
D_ATT = N_HEADS * HEAD_DIM
D_SGU = D_MODEL - D_ATT
N_GROUPS = D_SGU // HEAD_DIM
CHUNK = 128
WINDOW_MAX = 2048
RMS_EPS = 1e-6
SCALE = HEAD_DIM ** -0.5
MASKED = 1e30
LANES = 128
VMEM_LIMIT = 56 * 1024 * 1024

HEADS_PER_STEP = 4
QB = 128


def _params(n_axes):
    return pltpu.CompilerParams(dimension_semantics=("arbitrary",) * n_axes,
                                vmem_limit_bytes=VMEM_LIMIT)


def _cast_weight(w_ref, wb_ref):
    k = w_ref.shape[0]
    ck = 256
    def body(c, carry):
        r = pl.multiple_of(c * ck, ck)
        wb_ref[pl.ds(r, ck), :] = w_ref[pl.ds(r, ck), :].astype(BF16)
        return carry
    lax.fori_loop(0, k // ck, body, 0)


def _rms_rows_kernel(x_ref, g_ref, o_ref):
    x = x_ref[...]
    ms = jnp.mean(x * x, axis=-1, keepdims=True)
    o_ref[...] = (x * lax.rsqrt(ms + RMS_EPS) * g_ref[...]).astype(o_ref.dtype)


def rms_rows(x, gain):
    m, d = x.shape
    tm = min(m, 256)
    return pl.pallas_call(
        _rms_rows_kernel,
        out_shape=jax.ShapeDtypeStruct((m, d), BF16),
        grid=(m // tm,),
        in_specs=[pl.BlockSpec((tm, d), lambda i: (i, 0)),
                  pl.BlockSpec((1, d), lambda i: (0, 0))],
        out_specs=pl.BlockSpec((tm, d), lambda i: (i, 0)),
        compiler_params=_params(1),
        name="rms_rows",
    )(x, gain.reshape(1, d))


def _inproj_kernel(*refs, norm, want_f32, want_bf16, regroup, has_alias):
    it = iter(refs)
    h_ref = next(it)
    w_ref = next(it)
    gain_ref = next(it) if norm else None
    if has_alias:
        next(it)
    f32_ref = next(it) if want_f32 else None
    bf_ref = next(it) if want_bf16 else None
    o4_ref = next(it) if regroup else None
    o16_ref = next(it) if regroup else None
    wb_ref = next(it)
    zs_ref = next(it) if regroup else None

    @pl.when(pl.program_id(1) == 0)
    def _():
        _cast_weight(w_ref, wb_ref)

    z = jnp.dot(h_ref[...], wb_ref[...], preferred_element_type=F32)
    tm, tn = z.shape
    if norm:
        parts = []
        for hh in range(tn // HEAD_DIM):
            cols = slice(hh * HEAD_DIM, (hh + 1) * HEAD_DIM)
            zh = z[:, cols]
            ms = jnp.mean(zh * zh, axis=-1, keepdims=True)
            parts.append(zh * lax.rsqrt(ms + RMS_EPS) * gain_ref[:, cols])
        z = jnp.concatenate(parts, axis=1)
    if want_f32:
        f32_ref[...] = z
    if want_bf16:
        bf_ref[...] = z.astype(BF16)
    if regroup:
        for c in range(tn // LANES):
            cols = slice(c * LANES, (c + 1) * LANES)
            zs_ref[c] = z[:, cols]
            for r in range(4):
                o4_ref[r, :, cols] = zs_ref[c, pl.ds(r, tm // 4, stride=4), :].astype(BF16)
            for r in range(16):
                o16_ref[r, :, cols] = zs_ref[c, pl.ds(r, tm // 16, stride=16), :].astype(BF16)


def in_proj_section(h, w_in, layer, col0, gain, *, want_f32, want_bf16, regroup=False,
                    seq=None, stack=None, depth=1):
    m, k = h.shape
    n_sec = D_ATT
    tn = 512
    tm = min(m, 512)
    ni, nj = m // tm, n_sec // tn
    jb0 = col0 // tn
    norm = gain is not None
    stacked = want_f32 and depth > 1
    has_alias = stacked and layer > 0

    in_specs = [pl.BlockSpec((tm, k), lambda j, i: (i, 0)),
                pl.BlockSpec((None, k, tn), lambda j, i: (layer, 0, jb0 + j))]
    args = [h, w_in]
    if norm:
        in_specs.append(pl.BlockSpec((1, tn), lambda j, i: (0, j)))
        args.append(gain.reshape(1, n_sec))
    if has_alias:
        in_specs.append(pl.BlockSpec(memory_space=pl.ANY))
        args.append(stack)

    out_shape, out_specs = [], []
    if want_f32:
        rows = depth * m if stacked else m
        row0 = layer * ni if stacked else 0
        out_shape.append(jax.ShapeDtypeStruct((rows, n_sec), F32))
        out_specs.append(pl.BlockSpec((tm, tn), lambda j, i: (row0 + i, j)))
    if want_bf16:
        out_shape.append(jax.ShapeDtypeStruct((m, n_sec), BF16))
        out_specs.append(pl.BlockSpec((tm, tn), lambda j, i: (i, j)))
    if regroup:
        b = m // seq
        tpb = seq // tm
        out_shape.append(jax.ShapeDtypeStruct((b, 4, seq // 4, n_sec), BF16))
        out_specs.append(pl.BlockSpec((None, 4, tm // 4, tn),
                                      lambda j, i: (i // tpb, 0, i % tpb, j)))
        out_shape.append(jax.ShapeDtypeStruct((b, 16, seq // 16, n_sec), BF16))
        out_specs.append(pl.BlockSpec((None, 16, tm // 16, tn),
                                      lambda j, i: (i // tpb, 0, i % tpb, j)))
    scratch = [pltpu.VMEM((k, tn), BF16)]
    if regroup:
        scratch.append(pltpu.VMEM((tn // LANES, tm, LANES), F32))

    kern = functools.partial(_inproj_kernel, norm=norm, want_f32=want_f32, want_bf16=want_bf16,
                             regroup=regroup, has_alias=has_alias)
    outs = pl.pallas_call(
        kern, out_shape=out_shape, grid=(nj, ni), in_specs=in_specs, out_specs=out_specs,
        scratch_shapes=scratch,
        input_output_aliases=({len(args) - 1: 0} if has_alias else {}),
        compiler_params=_params(2), name="in_proj",
    )(*args)
    return outs


def _attn_tile(q, k, v, dist, slope, rows, hh, acc, m_s, l_s, first):
    s = lax.dot_general(q, k, (((1,), (1,)), ((), ())), preferred_element_type=F32)
    s = s * SCALE - slope * dist
    m_cur = jnp.max(s, axis=-1, keepdims=True)
    if first:
        m_new = jnp.broadcast_to(m_cur, (QB, LANES))
        p = jnp.exp(s - m_cur)
        l_new = jnp.broadcast_to(jnp.sum(p, axis=-1, keepdims=True), (QB, LANES))
        a_new = jnp.dot(p.astype(BF16), v, preferred_element_type=F32)
    else:
        m_prev = m_s[hh, rows, :]
        m_new = jnp.maximum(m_prev, m_cur)
        alpha = jnp.exp(m_prev - m_new)
        p = jnp.exp(s - jnp.concatenate([m_new] * (s.shape[1] // LANES), axis=1))
        l_new = alpha * l_s[hh, rows, :] + jnp.sum(p, axis=-1, keepdims=True)
        a_new = alpha * acc[hh, rows, :] + jnp.dot(p.astype(BF16), v, preferred_element_type=F32)
    m_s[hh, rows, :] = m_new
    l_s[hh, rows, :] = l_new
    acc[hh, rows, :] = a_new


def _band_dist(off, kb, dil):
    o_i = lax.broadcasted_iota(jnp.int32, (QB, kb), 0)
    c_i = lax.broadcasted_iota(jnp.int32, (QB, kb), 1)
    d = o_i - c_i + off
    return jnp.where((d >= 0) & (d <= 128), (d * dil).astype(F32), MASKED)


def _attn_kernel(sl_ref, qn, kn, vn, q4, k4, v4, q16, k16, v16, o_ref, acc, m_s, l_s, *, seq):
    g = pl.program_id(1)
    ph = pl.program_id(2)
    n_ph = pl.num_programs(2)

    def windowed(q_ref, k_ref, v_ref, n_blocks, dil, row_fn, first):
        def body(qb, carry):
            q0 = pl.multiple_of(qb * QB, QB)
            k0 = pl.multiple_of(jnp.maximum(qb - 1, 0) * QB, QB)
            dist = _band_dist(q0 - k0, 2 * QB, dil)
            for hh in range(HEADS_PER_STEP):
                cols = slice(hh * HEAD_DIM, (hh + 1) * HEAD_DIM)
                _attn_tile(q_ref[pl.ds(q0, QB), cols], k_ref[pl.ds(k0, 2 * QB), cols],
                           v_ref[pl.ds(k0, 2 * QB), cols], dist,
                           sl_ref[g * HEADS_PER_STEP + hh], row_fn(q0), hh,
                           acc, m_s, l_s, first)
            return carry
        lax.fori_loop(0, n_blocks, body, 0)

    @pl.when(ph == 0)
    def _():
        windowed(qn, kn, vn, seq // QB, 1, lambda q0: pl.ds(q0, QB), True)

    @pl.when((ph >= 1) & (ph <= 4))
    def _():
        r = ph - 1
        windowed(q4, k4, v4, seq // 4 // QB, 4,
                 lambda q0: pl.ds(r + 4 * q0, QB, stride=4), False)

    @pl.when(ph >= 5)
    def _():
        r = ph - 5
        o_i = lax.broadcasted_iota(jnp.int32, (QB, QB), 0)
        c_i = lax.broadcasted_iota(jnp.int32, (QB, QB), 1)
        d = o_i - c_i
        dist = jnp.where(d >= 0, (d * 16).astype(F32), MASKED)
        rows = pl.ds(r, QB, stride=16)
        for hh in range(HEADS_PER_STEP):
            cols = slice(hh * HEAD_DIM, (hh + 1) * HEAD_DIM)
            _attn_tile(q16[:, cols], k16[:, cols], v16[:, cols], dist,
                       sl_ref[g * HEADS_PER_STEP + hh], rows, hh, acc, m_s, l_s, False)

    @pl.when(ph == n_ph - 1)
    def _():
        def body(c, carry):
            r0 = pl.multiple_of(c * 256, 256)
            rows = pl.ds(r0, 256)
            for hh in range(HEADS_PER_STEP):
                cols = slice(hh * HEAD_DIM, (hh + 1) * HEAD_DIM)
                o_ref[rows, cols] = (acc[hh, rows, :] / l_s[hh, rows, :]).astype(o_ref.dtype)
            return carry
        lax.fori_loop(0, seq // 256, body, 0)


def prompt_attention(slopes, qs, ks, vs, batch, seq):
    assert seq // 16 == QB and seq % (4 * QB) == 0
    tw = HEADS_PER_STEP * HEAD_DIM
    nat = pl.BlockSpec((None, seq, tw), lambda b, g, p: (b, 0, g))
    r4 = pl.BlockSpec((None, None, seq // 4, tw),
                      lambda b, g, p: (b, jnp.clip(p - 1, 0, 3), 0, g))
    r16 = pl.BlockSpec((None, None, seq // 16, tw),
                       lambda b, g, p: (b, jnp.clip(p - 5, 0, 15), 0, g))
    d3 = lambda a: a.reshape(batch, seq, D_ATT)
    return pl.pallas_call(
        functools.partial(_attn_kernel, seq=seq),
        out_shape=jax.ShapeDtypeStruct((batch, seq, D_ATT), BF16),
        grid=(batch, N_HEADS // HEADS_PER_STEP, 21),
        in_specs=[pl.BlockSpec(memory_space=pltpu.SMEM),
                  nat, nat, nat, r4, r4, r4, r16, r16, r16],
        out_specs=pl.BlockSpec((None, seq, tw), lambda b, g, p: (b, 0, g)),
        scratch_shapes=[pltpu.VMEM((HEADS_PER_STEP, seq, LANES), F32)] * 3,
        compiler_params=_params(3), name="prompt_attention",
    )(slopes, d3(qs[0]), d3(ks[0]), d3(vs[0]), qs[1], ks[1], vs[1], qs[2], ks[2], vs[2])


def _sgu_kernel(g_ref, u_ref, w_ref, bt_ref, o_ref, wt_ref):
    @pl.when(pl.program_id(0) == 0)
    def _():
        r_i = lax.broadcasted_iota(jnp.int32, (CHUNK, CHUNK), 0)
        c_i = lax.broadcasted_iota(jnp.int32, (CHUNK, CHUNK), 1)
        for gi in range(N_GROUPS):
            wt_ref[gi] = jnp.where(r_i >= c_i, w_ref[gi], 0.0).astype(BF16)

    for gi in range(N_GROUPS):
        cols = slice(gi * HEAD_DIM, (gi + 1) * HEAD_DIM)
        gate = jnp.dot(wt_ref[gi], g_ref[:, cols], preferred_element_type=F32) + bt_ref[:, gi:gi + 1]
        o_ref[:, cols] = (u_ref[:, cols] * gate).astype(BF16)


def prompt_sgu(g, u, w_spatial, b_t, layer):
    m = g.shape[0]
    row = pl.BlockSpec((CHUNK, D_SGU), lambda c: (c, 0))
    return pl.pallas_call(
        _sgu_kernel,
        out_shape=jax.ShapeDtypeStruct((m, D_SGU), BF16),
        grid=(m // CHUNK,),
        in_specs=[row, row,
                  pl.BlockSpec((None, N_GROUPS, CHUNK, CHUNK), lambda c: (layer, 0, 0, 0)),
                  pl.BlockSpec((None, CHUNK, N_GROUPS), lambda c: (layer, 0, 0))],
        out_specs=row,
        scratch_shapes=[pltpu.VMEM((N_GROUPS, CHUNK, CHUNK), BF16)],
        compiler_params=_params(1), name="prompt_sgu",
    )(g, u, w_spatial, b_t)


def _outproj_kernel(a_ref, ug_ref, w_ref, x_ref, o_ref, wb_ref):
    @pl.when(pl.program_id(1) == 0)
    def _():
        _cast_weight(w_ref, wb_ref)
    z = jnp.dot(a_ref[...], wb_ref[:D_ATT, :], preferred_element_type=F32)
    z = z + jnp.dot(ug_ref[...], wb_ref[D_ATT:, :], preferred_element_type=F32)
    o_ref[...] = x_ref[...] + z


def out_proj(att, ug, w_out, x, layer):
    m = x.shape[0]
    tn = 512
    tm = min(m, 512)
    lhs = pl.BlockSpec((tm, D_ATT), lambda j, i: (i, 0))
    res = pl.BlockSpec((tm, tn), lambda j, i: (i, j))
    return pl.pallas_call(
        _outproj_kernel,
        out_shape=jax.ShapeDtypeStruct((m, D_MODEL), F32),
        grid=(D_MODEL // tn, m // tm),
        in_specs=[lhs, lhs,
                  pl.BlockSpec((None, D_MODEL, tn), lambda j, i: (layer, 0, j)),
                  res],
        out_specs=res,
        scratch_shapes=[pltpu.VMEM((D_MODEL, tn), BF16)],
        compiler_params=_params(2), name="out_proj",
    )(att, ug, w_out, x)


def _ffn1_kernel(h_ref, wg_ref, wu_ref, a_ref, wgb_ref, wub_ref):
    @pl.when(pl.program_id(1) == 0)
    def _():
        _cast_weight(wg_ref, wgb_ref)
        _cast_weight(wu_ref, wub_ref)
    h = h_ref[...]
    g = jnp.dot(h, wgb_ref[...], preferred_element_type=F32)
    u = jnp.dot(h, wub_ref[...], preferred_element_type=F32)
    a_ref[...] = (g / (1.0 + jnp.exp(-g)) * u).astype(BF16)


def ffn_gate_up(h, w_gate, w_up, layer):
    m, k = h.shape
    d_ff = w_gate.shape[2]
    tn = 256
    tm = min(m, 512)
    wspec = pl.BlockSpec((None, k, tn), lambda j, i: (layer, 0, j))
    return pl.pallas_call(
        _ffn1_kernel,
        out_shape=jax.ShapeDtypeStruct((m, d_ff), BF16),
        grid=(d_ff // tn, m // tm),
        in_specs=[pl.BlockSpec((tm, k), lambda j, i: (i, 0)), wspec, wspec],
        out_specs=pl.BlockSpec((tm, tn), lambda j, i: (i, j)),
        scratch_shapes=[pltpu.VMEM((k, tn), BF16)] * 2,
        compiler_params=_params(2), name="ffn_gate_up",
    )(h, w_gate, w_up)


def _ffn2_kernel(a_ref, w_ref, x_ref, o_ref, wb_ref):
    @pl.when(pl.program_id(1) == 0)
    def _():
        _cast_weight(w_ref, wb_ref)
    o_ref[...] = x_ref[...] + jnp.dot(a_ref[...], wb_ref[...], preferred_element_type=F32)


def ffn_down(a, w_down, x, layer):
    m, k = a.shape
    tn = 512
    tm = min(m, 256)
    res = pl.BlockSpec((tm, tn), lambda j, i: (i, j))
    return pl.pallas_call(
        _ffn2_kernel,
        out_shape=jax.ShapeDtypeStruct((m, D_MODEL), F32),
        grid=(D_MODEL // tn, m // tm),
        in_specs=[pl.BlockSpec((tm, k), lambda j, i: (i, 0)),
                  pl.BlockSpec((None, k, tn), lambda j, i: (layer, 0, j),
                               pipeline_mode=pl.Buffered(1)),
                  res],
        out_specs=res,
        scratch_shapes=[pltpu.VMEM((k, tn), BF16)],
        compiler_params=_params(2), name="ffn_down",
    )(a, w_down, x)


SQ = 16


def _multiplicity(d):
    nonneg = d >= 0
    n = (nonneg & (d <= 128)).astype(F32)
    n = n + (nonneg & (d <= 512) & ((d & 3) == 0)).astype(F32)
    n = n + (nonneg & (d <= 2048) & ((d & 15) == 0)).astype(F32)
    return n


def _sattn_kernel(sl_ref, q_ref, kn_ref, vn_ref, kc_ref, vc_ref, o_ref, acc, m_s, l_s, *, tc, wb, ts):
    c = pl.program_id(1)
    last = pl.num_programs(1) - 1

    @pl.when(c == 0)
    def _():
        m_s[...] = jnp.full(m_s.shape, -MASKED, F32)
        l_s[...] = jnp.zeros(l_s.shape, F32)
        acc[...] = jnp.zeros(acc.shape, F32)

    def pad_rows(x, n):
        return jnp.concatenate([x, jnp.zeros((n - x.shape[0], x.shape[1]), F32)], axis=0)

    def update(h, q, k, v, d):
        mult = _multiplicity(d)
        dist = jnp.where(mult > 0.0, d.astype(F32), MASKED)
        s = lax.dot_general(q, k, (((1,), (1,)), ((), ())), preferred_element_type=F32)
        s = s * SCALE - sl_ref[h] * dist
        m_prev = m_s[h]
        m_new = jnp.maximum(m_prev, jnp.max(s, axis=-1, keepdims=True))
        alpha = jnp.exp(m_prev - m_new)
        p = mult * jnp.exp(s - jnp.concatenate([m_new] * (s.shape[1] // LANES), axis=1))
        l_s[h] = alpha * l_s[h] + jnp.sum(p, axis=-1, keepdims=True)
        acc[h] = alpha * acc[h] + jnp.dot(p.astype(BF16), v, preferred_element_type=F32)
        m_s[h] = m_new

    t_c = lax.broadcasted_iota(jnp.int32, (SQ, tc), 0)
    c_c = lax.broadcasted_iota(jnp.int32, (SQ, tc), 1)
    d_cache = (wb + t_c) - (c * tc + c_c)
    for h in range(N_HEADS):
        cols = slice(h * HEAD_DIM, (h + 1) * HEAD_DIM)
        q = pad_rows(q_ref[:, cols], SQ).astype(BF16)
        k = kc_ref[pl.ds(h, tc, stride=N_HEADS), :].astype(BF16)
        v = vc_ref[pl.ds(h, tc, stride=N_HEADS), :].astype(BF16)
        update(h, q, k, v, d_cache)

    @pl.when(c == last)
    def _():
        t_n = lax.broadcasted_iota(jnp.int32, (SQ, LANES), 0)
        c_n = lax.broadcasted_iota(jnp.int32, (SQ, LANES), 1)
        d_new = jnp.where(c_n < ts, t_n - c_n, -1)
        for h in range(N_HEADS):
            cols = slice(h * HEAD_DIM, (h + 1) * HEAD_DIM)
            q = pad_rows(q_ref[:, cols], SQ).astype(BF16)
            k = pad_rows(kn_ref[:, cols], LANES).astype(BF16)
            v = pad_rows(vn_ref[:, cols], LANES).astype(BF16)
            update(h, q, k, v, d_new)
            o_ref[:, cols] = (acc[h] / l_s[h])[:ts, :]


def sample_attention(slopes, q, k_new, v_new, cache_k, cache_v, layer):
    bs, ts, _ = q.shape
    wb = cache_k.shape[2] // N_HEADS
    tc = 512
    assert wb % tc == 0 and ts <= SQ and wb == WINDOW_MAX
    new = pl.BlockSpec((None, ts, D_ATT), lambda b, c: (b, 0, 0))
    cache = pl.BlockSpec((None, None, tc * N_HEADS, HEAD_DIM), lambda b, c: (layer, b, c, 0))
    return pl.pallas_call(
        functools.partial(_sattn_kernel, tc=tc, wb=wb, ts=ts),
        out_shape=jax.ShapeDtypeStruct((bs, ts, D_ATT), F32),
        grid=(bs, wb // tc),
        in_specs=[pl.BlockSpec(memory_space=pltpu.SMEM), new, new, new, cache, cache],
        out_specs=new,
        scratch_shapes=[pltpu.VMEM((N_HEADS, SQ, LANES), F32)] * 3,
        compiler_params=_params(2), name="sample_attention",
    )(slopes, q, k_new, v_new, cache_k, cache_v)


def _ssgu_kernel(att_ref, g_ref, u_ref, wt_ref, b_ref, att_o, ug_o, *, ts):
    att_o[...] = att_ref[...].astype(BF16)
    n = att_ref.shape[0]
    r_i = lax.broadcasted_iota(jnp.int32, (n, n), 0)
    c_i = lax.broadcasted_iota(jnp.int32, (n, n), 1)
    keep = ((r_i & -ts) == (c_i & -ts)) & ((c_i & (ts - 1)) <= (r_i & (ts - 1)))
    for gi in range(N_GROUPS):
        cols = slice(gi * HEAD_DIM, (gi + 1) * HEAD_DIM)
        w = jnp.where(keep, wt_ref[gi], 0.0).astype(BF16)
        gate = jnp.dot(w, g_ref[:, cols].astype(BF16), preferred_element_type=F32) + b_ref[:, gi:gi + 1]
        ug_o[:, cols] = (u_ref[:, cols] * gate).astype(BF16)


def sample_sgu(att, g, u, w_tiled, b_rows, ts):
    n = att.shape[0]
    return pl.pallas_call(
        functools.partial(_ssgu_kernel, ts=ts),
        out_shape=[jax.ShapeDtypeStruct((n, D_ATT), BF16), jax.ShapeDtypeStruct((n, D_SGU), BF16)],
        compiler_params=pltpu.CompilerParams(vmem_limit_bytes=VMEM_LIMIT),
        name="sample_sgu",
    )(att, g, u, w_tiled, b_rows)


def kernel(x_prompt, x_sample, cache_k_win, cache_v_win, norm1, w_in, q_gain, k_gain, sgu_gain,
           w_spatial, b_spatial, w_out, norm2, w_gate, w_up, w_down):
    bp, sp, _ = x_prompt.shape
    bs, ts, _ = x_sample.shape
    assert ts & (ts - 1) == 0
    depth = w_in.shape[0]
    wb = cache_k_win.shape[2]
    assert sp == WINDOW_MAX and ts <= 8 and wb == WINDOW_MAX
    mp, ms = bp * sp, bs * ts

    xp = x_prompt.reshape(mp, D_MODEL)
    xs = x_sample.reshape(ms, D_MODEL)
    cache_k = cache_k_win.reshape(depth, bs, wb * N_HEADS, HEAD_DIM)
    cache_v = cache_v_win.reshape(depth, bs, wb * N_HEADS, HEAD_DIM)
    slopes = jnp.exp2(-8.0 * jnp.arange(1, N_HEADS + 1, dtype=F32) / N_HEADS)
    b_t = jnp.swapaxes(b_spatial, 1, 2)

    k_stack = v_stack = None
    ks_rows, vs_rows, gs_rows = [], [], []
    for l in range(depth):
        qg = jnp.tile(q_gain[l], N_HEADS)
        kg = jnp.tile(k_gain[l], N_HEADS)
        sg = sgu_gain[l].reshape(D_SGU)

        hp = rms_rows(xp, norm1[l])
        sec = functools.partial(in_proj_section, hp, w_in, l, seq=sp, depth=depth)
        q_p = sec(0 * D_ATT, qg, want_f32=False, want_bf16=True, regroup=True)
        k_stack, *k_p = sec(1 * D_ATT, kg, want_f32=True, want_bf16=True, regroup=True, stack=k_stack)
        v_stack, *v_p = sec(2 * D_ATT, None, want_f32=True, want_bf16=True, regroup=True, stack=v_stack)
        (u_p,) = in_proj_section(hp, w_in, l, 3 * D_ATT, None, want_f32=True, want_bf16=False)
        (g_p,) = in_proj_section(hp, w_in, l, 3 * D_ATT + D_SGU, sg, want_f32=False, want_bf16=True)
        att_p = prompt_attention(slopes, q_p, k_p, v_p, bp, sp).reshape(mp, D_ATT)
        ug_p = prompt_sgu(g_p, u_p, w_spatial, b_t, l)

        hs = rms_rows(xs, norm1[l])
        secs = functools.partial(in_proj_section, hs, w_in, l, want_f32=True, want_bf16=False)
        (q_s,) = secs(0 * D_ATT, qg)
        (k_s,) = secs(1 * D_ATT, kg)
        (v_s,) = secs(2 * D_ATT, None)
        (u_s,) = secs(3 * D_ATT, None)
        (g_s,) = secs(3 * D_ATT + D_SGU, sg)
        r3 = lambda a: a.reshape(bs, ts, D_ATT)
        att_s = sample_attention(slopes, r3(q_s), r3(k_s), r3(v_s), cache_k, cache_v, l)
        w_tiled = jnp.tile(w_spatial[l][:, :ts, :ts], (1, bs, bs))
        b_rows = jnp.tile(b_spatial[l][:, :ts].T, (bs, 1))
        att_sb, ug_s = sample_sgu(att_s.reshape(ms, D_ATT), g_s, u_s, w_tiled, b_rows, ts)

        xp = out_proj(att_p, ug_p, w_out, xp, l)
        xs = out_proj(att_sb, ug_s, w_out, xs, l)
        xp = ffn_down(ffn_gate_up(rms_rows(xp, norm2[l]), w_gate, w_up, l), w_down, xp, l)
        xs = ffn_down(ffn_gate_up(rms_rows(xs, norm2[l]), w_gate, w_up, l), w_down, xs, l)

        ks_rows.append(k_s)
        vs_rows.append(v_s)
        gs_rows.append(g_s)

    heads = lambda a, b, t: a.reshape(depth, b, t, N_HEADS, HEAD_DIM)
    return (xp.reshape(bp, sp, D_MODEL), xs.reshape(bs, ts, D_MODEL),
            heads(k_stack, bp, sp), heads(v_stack, bp, sp),
            heads(jnp.stack(ks_rows), bs, ts), heads(jnp.stack(vs_rows), bs, ts),
            heads(jnp.stack(gs_rows), bs, ts))
```

```python
import functools

import jax
import jax.numpy as jnp
from jax import lax
from jax.experimental import pallas as pl
from jax.experimental.pallas import tpu as pltpu

F32 = jnp.float32
BF16 = jnp.bfloat16

D_MODEL = 4096
HEAD_DIM = 128
N_HEADS = 16
D_ATT = N_HEADS * HEAD_DIM
D_SGU = D_MODEL - D_ATT
N_GROUPS = D_SGU // HEAD_DIM
CHUNK = 128
WINDOW_MAX = 2048
RMS_EPS = 1e-6
SCALE = HEAD_DIM ** -0.5
MASKED = 1e30
LANES = 128
VMEM_LIMIT = 60 * 1024 * 1024

HEADS_PER_STEP = 4
QB = 128

TM_PROJ = 1024
TN_PROJ = 512
TN_FFN = 256
TM_DOWN = 256
TN_DOWN = 512


def _params(n_axes):
    return pltpu.CompilerParams(dimension_semantics=("arbitrary",) * n_axes,
                                vmem_limit_bytes=VMEM_LIMIT)


def _cast_weight(w_ref, wb_ref):
    k = w_ref.shape[0]
    ck = 256
    def body(c, carry):
        r = pl.multiple_of(c * ck, ck)
        wb_ref[pl.ds(r, ck), :] = w_ref[pl.ds(r, ck), :].astype(BF16)
        return carry
    lax.fori_loop(0, k // ck, body, 0)


def _is_first_row_tile():
    return pl.program_id(1) == 0


def _is_last_row_tile():
    return pl.program_id(1) == pl.num_programs(1) - 1


def _rms_rows_kernel(x_ref, g_ref, o_ref):
    x = x_ref[...]
    ms = jnp.mean(x * x, axis=-1, keepdims=True)
    o_ref[...] = (x * lax.rsqrt(ms + RMS_EPS) * g_ref[...]).astype(o_ref.dtype)


def rms_rows(x, gain):
    m, d = x.shape
    tm = min(m, 256)
    return pl.pallas_call(
        _rms_rows_kernel,
        out_shape=jax.ShapeDtypeStruct((m, d), BF16),
        grid=(m // tm,),
        in_specs=[pl.BlockSpec((tm, d), lambda i: (i, 0)),
                  pl.BlockSpec((1, d), lambda i: (0, 0))],
        out_specs=pl.BlockSpec((tm, d), lambda i: (i, 0)),
        compiler_params=_params(1),
        name="rms_rows",
    )(x, gain.reshape(1, d))


def _head_norm(z, gain):
    parts = []
    for hh in range(z.shape[1] // HEAD_DIM):
        cols = slice(hh * HEAD_DIM, (hh + 1) * HEAD_DIM)
        zh = z[:, cols]
        ms = jnp.mean(zh * zh, axis=-1, keepdims=True)
        parts.append(zh * lax.rsqrt(ms + RMS_EPS) * gain[:, cols])
    return jnp.concatenate(parts, axis=1)


def _inproj_kernel(*refs, norm, want_f32, want_bf16, regroup, has_alias):
    it = iter(refs)
    h_ref = next(it)
    hs_ref = next(it)
    w_ref = next(it)
    gain_ref = next(it) if norm else None
    if has_alias:
        next(it)
    s_ref = next(it)
    f32_ref = next(it) if want_f32 else None
    bf_ref = next(it) if want_bf16 else None
    o4_ref = next(it) if regroup else None
    o16_ref = next(it) if regroup else None
    wb_ref = next(it)
    zs_ref = next(it) if regroup else None
    z4_ref = next(it) if regroup else None

    @pl.when(_is_first_row_tile())
    def _():
        _cast_weight(w_ref, wb_ref)

    tm = h_ref.shape[0]
    half = 2 * HEAD_DIM
    for c in range(wb_ref.shape[1] // half):
        cols = slice(c * half, (c + 1) * half)
        z = jnp.dot(h_ref[...], wb_ref[:, cols], preferred_element_type=F32)
        if norm:
            z = _head_norm(z, gain_ref[:, cols])
        if want_f32:
            f32_ref[:, cols] = z
        if want_bf16:
            bf_ref[:, cols] = z.astype(BF16)
        if regroup:
            for s in range(half // LANES):
                lc = slice(c * half + s * LANES, c * half + (s + 1) * LANES)
                slab = c * (half // LANES) + s
                zs_ref[slab] = z[:, s * LANES:(s + 1) * LANES]
                for r in range(4):
                    z4 = zs_ref[slab, pl.ds(r, tm // 4, stride=4), :]
                    o4_ref[r, :, lc] = z4.astype(BF16)
                    z4_ref[slab, r] = z4
                for r in range(4):
                    for c4 in range(4):
                        o16_ref[r + 4 * c4, :, lc] = (
                            z4_ref[slab, r, pl.ds(c4, tm // 16, stride=4), :].astype(BF16))

    @pl.when(_is_last_row_tile())
    def _():
        z = jnp.dot(hs_ref[...], wb_ref[...], preferred_element_type=F32)
        if norm:
            z = _head_norm(z, gain_ref[...])
        s_ref[...] = z


def in_proj_section(h, hs, w_in, layer, col0, gain, *, want_f32, want_bf16, regroup=False,
                    seq=None, stack=None, depth=1):
    m, k = h.shape
    ms = hs.shape[0]
    n_sec = D_ATT
    tn, tm = TN_PROJ, TM_PROJ
    ni, nj = m // tm, n_sec // tn
    jb0 = col0 // tn
    norm = gain is not None
    stacked = want_f32 and stack != "unstacked"
    has_alias = stacked and layer > 0

    in_specs = [pl.BlockSpec((tm, k), lambda j, i: (i, 0)),
                pl.BlockSpec((ms, k), lambda j, i: (0, 0)),
                pl.BlockSpec((None, k, tn), lambda j, i: (layer, 0, jb0 + j))]
    args = [h, hs, w_in]
    if norm:
        in_specs.append(pl.BlockSpec((1, tn), lambda j, i: (0, j)))
        args.append(gain.reshape(1, n_sec))
    if has_alias:
        in_specs.append(pl.BlockSpec(memory_space=pl.ANY))
        args.append(stack)

    out_shape = [jax.ShapeDtypeStruct((ms, n_sec), F32)]
    out_specs = [pl.BlockSpec((ms, tn), lambda j, i: (0, j))]
    if want_f32:
        row0 = layer * ni if stacked else 0
        out_shape.append(jax.ShapeDtypeStruct(((depth if stacked else 1) * m, n_sec), F32))
        out_specs.append(pl.BlockSpec((tm, tn), lambda j, i: (row0 + i, j)))
    if want_bf16:
        out_shape.append(jax.ShapeDtypeStruct((m, n_sec), BF16))
        out_specs.append(pl.BlockSpec((tm, tn), lambda j, i: (i, j)))
    if regroup:
        b = m // seq
        tpb = seq // tm
        out_shape.append(jax.ShapeDtypeStruct((b, 4, seq // 4, n_sec), BF16))
        out_specs.append(pl.BlockSpec((None, 4, tm // 4, tn),
                                      lambda j, i: (i // tpb, 0, i % tpb, j)))
        out_shape.append(jax.ShapeDtypeStruct((b, 16, seq // 16, n_sec), BF16))
        out_specs.append(pl.BlockSpec((None, 16, tm // 16, tn),
                                      lambda j, i: (i // tpb, 0, i % tpb, j)))
    scratch = [pltpu.VMEM((k, tn), BF16)]
    if regroup:
        scratch.append(pltpu.VMEM((tn // LANES, tm, LANES), F32))
        scratch.append(pltpu.VMEM((tn // LANES, 4, tm // 4, LANES), F32))

    kern = functools.partial(_inproj_kernel, norm=norm, want_f32=want_f32, want_bf16=want_bf16,
                             regroup=regroup, has_alias=has_alias)
    return pl.pallas_call(
        kern, out_shape=out_shape, grid=(nj, ni), in_specs=in_specs, out_specs=out_specs,
        scratch_shapes=scratch,
        input_output_aliases=({len(args) - 1: 1} if has_alias else {}),
        compiler_params=_params(2), name="in_proj",
    )(*args)


def _attn_tiles(qkv, dist, slopes, prev):
    scores = [lax.dot_general(q, k, (((1,), (1,)), ((), ())), preferred_element_type=F32)
              for q, k, _ in qkv]
    probs = []
    for hh, s in enumerate(scores):
        s = s * SCALE - slopes[hh] * dist
        m_cur = jnp.max(s, axis=-1, keepdims=True)
        if prev is None:
            probs.append((jnp.exp(s - m_cur).astype(BF16), jnp.broadcast_to(m_cur, (QB, LANES)), None))
        else:
            m_prev = prev[hh][1]
            m_new = jnp.maximum(m_prev, m_cur)
            p = jnp.exp(s - jnp.concatenate([m_new] * (s.shape[1] // LANES), axis=1))
            probs.append((p.astype(BF16), m_new, jnp.exp(m_prev - m_new)))
    new = []
    for hh, (p, m_new, alpha) in enumerate(probs):
        v = qkv[hh][2]
        v_ones = jnp.concatenate([v, jnp.ones(v.shape, BF16)], axis=1)
        pv = jnp.dot(p, v_ones, preferred_element_type=F32)
        if prev is None:
            new.append((pv[:, :LANES], m_new, pv[:, LANES:]))
        else:
            new.append((alpha * prev[hh][0] + pv[:, :LANES], m_new, alpha * prev[hh][2] + pv[:, LANES:]))
    return new


def _window_dist(d, dil):
    return jnp.where((d >= 0) & (d <= 128), (d * dil).astype(F32), MASKED)


def _attn_kernel(sl_ref, q4, k4, v4, q16, k16, v16, o_ref, acc, m_s, l_s, dist_s, nat_s, *, seq):
    g = pl.program_id(1)
    s4 = seq // 4
    state = (acc, m_s, l_s)

    o_i = lax.broadcasted_iota(jnp.int32, (QB, 2 * QB), 0)
    c_i = lax.broadcasted_iota(jnp.int32, (QB, 2 * QB), 1)
    d1 = 4 * ((o_i & 31) - (c_i & 63)) + ((o_i >> 5) - (c_i >> 6))
    dist_s[0] = _window_dist(d1, 1)
    dist_s[1] = _window_dist(d1 + 128, 1)
    dist_s[2] = _window_dist(o_i - c_i, 4)
    dist_s[3] = _window_dist(o_i - c_i + 128, 4)
    dist_s[4] = _window_dist(o_i - c_i, 16)

    heads = range(HEADS_PER_STEP)

    def load_state(rows):
        return [tuple(ref[hh, rows, :] for ref in state) for hh in heads]

    def store_state(rows_list, new):
        for hh in heads:
            for ref, val in zip(state, new[hh]):
                n = val.shape[0] // len(rows_list)
                for idx, rows in enumerate(rows_list):
                    ref[hh, rows, :] = val[idx * n:(idx + 1) * n, :]

    def head_cols(hh):
        return slice(hh * HEAD_DIM, (hh + 1) * HEAD_DIM)

    slopes = [sl_ref[g * HEADS_PER_STEP + hh] for hh in heads]

    def body1(qb, carry):
        t0 = pl.multiple_of(qb * 32, 32)
        kt0 = pl.multiple_of(jnp.maximum(qb - 1, 0) * 32, 32)
        dist = dist_s[jnp.minimum(qb, 1)]
        def slabs(ref, start, n, hh):
            return jnp.concatenate([ref[r, pl.ds(start, n), head_cols(hh)] for r in range(4)], axis=0)
        qkv = [(slabs(q4, t0, 32, hh), slabs(k4, kt0, 64, hh), slabs(v4, kt0, 64, hh)) for hh in heads]
        new = _attn_tiles(qkv, dist, slopes, None)
        store_state([pl.ds(r * s4 + t0, 32) for r in range(4)], new)
        return carry
    lax.fori_loop(0, seq // QB, body1, 0, unroll=2)

    n_qb4 = s4 // QB
    assert n_qb4 == 4
    def body4(idx, carry):
        r = idx >> 2
        qb = idx & 3
        q0 = pl.multiple_of(qb * QB, QB)
        k0 = pl.multiple_of(jnp.maximum(qb - 1, 0) * QB, QB)
        dist = dist_s[2 + jnp.minimum(qb, 1)]
        rows = pl.ds(r * s4 + q0, QB)
        qkv = [(q4[r, pl.ds(q0, QB), head_cols(hh)], k4[r, pl.ds(k0, 2 * QB), head_cols(hh)],
                v4[r, pl.ds(k0, 2 * QB), head_cols(hh)]) for hh in heads]
        new = _attn_tiles(qkv, dist, slopes, load_state(rows))
        store_state([rows], new)
        return carry
    lax.fori_loop(0, 4 * n_qb4, body4, 0, unroll=2)

    def body16(r16, carry):
        dist = dist_s[4][:, :QB]
        rows = pl.ds((r16 & 3) * s4 + (r16 >> 2), QB, stride=4)
        qkv = [(q16[r16, :, head_cols(hh)], k16[r16, :, head_cols(hh)], v16[r16, :, head_cols(hh)])
               for hh in heads]
        new = _attn_tiles(qkv, dist, slopes, load_state(rows))
        store_state([rows], new)
        return carry
    lax.fori_loop(0, 16, body16, 0, unroll=2)

    tn = nat_s.shape[1] // 4
    def body_out(c, carry):
        t0 = pl.multiple_of(c * tn, tn)
        for hh in range(HEADS_PER_STEP):
            for r in range(4):
                rows = pl.ds(r * s4 + t0, tn)
                nat_s[hh, pl.ds(r, tn, stride=4), :] = acc[hh, rows, :] / l_s[hh, rows, :]
            o_ref[pl.ds(pl.multiple_of(4 * t0, 4 * tn), 4 * tn), head_cols(hh)] = (
                nat_s[hh].astype(o_ref.dtype))
        return carry
    lax.fori_loop(0, s4 // tn, body_out, 0)


def prompt_attention(slopes, qs, ks, vs, batch, seq):
    assert seq // 16 == QB and seq % (4 * QB) == 0
    tw = HEADS_PER_STEP * HEAD_DIM
    r4 = pl.BlockSpec((None, 4, seq // 4, tw), lambda b, g: (b, 0, 0, g))
    r16 = pl.BlockSpec((None, 16, seq // 16, tw), lambda b, g: (b, 0, 0, g))
    state = pltpu.VMEM((HEADS_PER_STEP, seq, LANES), F32)
    return pl.pallas_call(
        functools.partial(_attn_kernel, seq=seq),
        out_shape=jax.ShapeDtypeStruct((batch, seq, D_ATT), BF16),
        grid=(batch, N_HEADS // HEADS_PER_STEP),
        in_specs=[pl.BlockSpec(memory_space=pltpu.SMEM), r4, r4, r4, r16, r16, r16],
        out_specs=pl.BlockSpec((None, seq, tw), lambda b, g: (b, 0, g)),
        scratch_shapes=[state, state, state,
                        pltpu.VMEM((5, QB, 2 * QB), F32),
                        pltpu.VMEM((HEADS_PER_STEP, 256, LANES), F32)],
        compiler_params=_params(2), name="prompt_attention",
    )(slopes, qs[0], ks[0], vs[0], qs[1], ks[1], vs[1])


def _sgu_kernel(g_ref, u_ref, w_ref, bt_ref, o_ref, wt_ref):
    @pl.when(pl.program_id(0) == 0)
    def _():
        r_i = lax.broadcasted_iota(jnp.int32, (CHUNK, CHUNK), 0)
        c_i = lax.broadcasted_iota(jnp.int32, (CHUNK, CHUNK), 1)
        for gi in range(N_GROUPS):
            wt_ref[gi] = jnp.where(r_i >= c_i, w_ref[gi], 0.0).astype(BF16)

    for gi in range(N_GROUPS):
        cols = slice(gi * HEAD_DIM, (gi + 1) * HEAD_DIM)
        gate = jnp.dot(wt_ref[gi], g_ref[:, cols], preferred_element_type=F32) + bt_ref[:, gi:gi + 1]
        o_ref[:, cols] = (u_ref[:, cols] * gate).astype(BF16)


def prompt_sgu(g, u, w_spatial, b_t, layer):
    m = g.shape[0]
    row = pl.BlockSpec((CHUNK, D_SGU), lambda c: (c, 0))
    return pl.pallas_call(
        _sgu_kernel,
        out_shape=jax.ShapeDtypeStruct((m, D_SGU), BF16),
        grid=(m // CHUNK,),
        in_specs=[row, row,
                  pl.BlockSpec((None, N_GROUPS, CHUNK, CHUNK), lambda c: (layer, 0, 0, 0)),
                  pl.BlockSpec((None, CHUNK, N_GROUPS), lambda c: (layer, 0, 0))],
        out_specs=row,
        scratch_shapes=[pltpu.VMEM((N_GROUPS, CHUNK, CHUNK), BF16)],
        compiler_params=_params(1), name="prompt_sgu",
    )(g, u, w_spatial, b_t)


def _outproj_kernel(a_ref, ug_ref, as_ref, ugs_ref, w_ref, x_ref, xs_ref, o_ref, os_ref, wb_ref):
    @pl.when(_is_first_row_tile())
    def _():
        _cast_weight(w_ref, wb_ref)

    def project(a, ug):
        z = jnp.dot(a, wb_ref[:D_ATT, :], preferred_element_type=F32)
        return z + jnp.dot(ug, wb_ref[D_ATT:, :], preferred_element_type=F32)

    o_ref[...] = x_ref[...] + project(a_ref[...], ug_ref[...])

    @pl.when(_is_last_row_tile())
    def _():
        os_ref[...] = xs_ref[...] + project(as_ref[...], ugs_ref[...])


def out_proj(att, ug, att_s, ug_s, w_out, x, xs, layer):
    m, ms = x.shape[0], xs.shape[0]
    tn, tm = TN_PROJ, TM_PROJ
    lhs = pl.BlockSpec((tm, D_ATT), lambda j, i: (i, 0))
    lhs_s = pl.BlockSpec((ms, D_ATT), lambda j, i: (0, 0))
    res = pl.BlockSpec((tm, tn), lambda j, i: (i, j))
    res_s = pl.BlockSpec((ms, tn), lambda j, i: (0, j))
    return pl.pallas_call(
        _outproj_kernel,
        out_shape=[jax.ShapeDtypeStruct((m, D_MODEL), F32), jax.ShapeDtypeStruct((ms, D_MODEL), F32)],
        grid=(D_MODEL // tn, m // tm),
        in_specs=[lhs, lhs, lhs_s, lhs_s,
                  pl.BlockSpec((None, D_MODEL, tn), lambda j, i: (layer, 0, j)),
                  res, res_s],
        out_specs=[res, res_s],
        scratch_shapes=[pltpu.VMEM((D_MODEL, tn), BF16)],
        compiler_params=_params(2), name="out_proj",
    )(att, ug, att_s, ug_s, w_out, x, xs)


def _ffn1_kernel(h_ref, hs_ref, wg_ref, wu_ref, a_ref, as_ref, wgb_ref, wub_ref):
    @pl.when(_is_first_row_tile())
    def _():
        _cast_weight(wg_ref, wgb_ref)
        _cast_weight(wu_ref, wub_ref)

    def gated(h):
        g = jnp.dot(h, wgb_ref[...], preferred_element_type=F32)
        u = jnp.dot(h, wub_ref[...], preferred_element_type=F32)
        return (g / (1.0 + jnp.exp(-g)) * u).astype(BF16)

    a_ref[...] = gated(h_ref[...])

    @pl.when(_is_last_row_tile())
    def _():
        as_ref[...] = gated(hs_ref[...])


def ffn_gate_up(h, hs, w_gate, w_up, layer):
    m, k = h.shape
    ms = hs.shape[0]
    d_ff = w_gate.shape[2]
    tn, tm = TN_FFN, TM_PROJ
    wspec = pl.BlockSpec((None, k, tn), lambda j, i: (layer, 0, j))
    return pl.pallas_call(
        _ffn1_kernel,
        out_shape=[jax.ShapeDtypeStruct((m, d_ff), BF16), jax.ShapeDtypeStruct((ms, d_ff), BF16)],
        grid=(d_ff // tn, m // tm),
        in_specs=[pl.BlockSpec((tm, k), lambda j, i: (i, 0)),
                  pl.BlockSpec((ms, k), lambda j, i: (0, 0)), wspec, wspec],
        out_specs=[pl.BlockSpec((tm, tn), lambda j, i: (i, j)),
                   pl.BlockSpec((ms, tn), lambda j, i: (0, j))],
        scratch_shapes=[pltpu.VMEM((k, tn), BF16)] * 2,
        compiler_params=_params(2), name="ffn_gate_up",
    )(h, hs, w_gate, w_up)


def _ffn2_kernel(a_ref, as_ref, w_ref, x_ref, xs_ref, o_ref, os_ref, wb_ref):
    @pl.when(_is_first_row_tile())
    def _():
        _cast_weight(w_ref, wb_ref)
    o_ref[...] = x_ref[...] + jnp.dot(a_ref[...], wb_ref[...], preferred_element_type=F32)

    @pl.when(_is_last_row_tile())
    def _():
        os_ref[...] = xs_ref[...] + jnp.dot(as_ref[...], wb_ref[...], preferred_element_type=F32)


def ffn_down(a, a_s, w_down, x, xs, layer):
    m, k = a.shape
    ms = a_s.shape[0]
    tn, tm = TN_DOWN, TM_DOWN
    res = pl.BlockSpec((tm, tn), lambda j, i: (i, j))
    res_s = pl.BlockSpec((ms, tn), lambda j, i: (0, j))
    return pl.pallas_call(
        _ffn2_kernel,
        out_shape=[jax.ShapeDtypeStruct((m, D_MODEL), F32), jax.ShapeDtypeStruct((ms, D_MODEL), F32)],
        grid=(D_MODEL // tn, m // tm),
        in_specs=[pl.BlockSpec((tm, k), lambda j, i: (i, 0)),
                  pl.BlockSpec((ms, k), lambda j, i: (0, 0)),
                  pl.BlockSpec((None, k, tn), lambda j, i: (layer, 0, j),
                               pipeline_mode=pl.Buffered(1)),
                  res, res_s],
        out_specs=[res, res_s],
        scratch_shapes=[pltpu.VMEM((k, tn), BF16)],
        compiler_params=_params(2), name="ffn_down",
    )(a, a_s, w_down, x, xs)


SQ = 16


def _multiplicity(d):
    nonneg = d >= 0
    n = (nonneg & (d <= 128)).astype(F32)
    n = n + (nonneg & (d <= 512) & ((d & 3) == 0)).astype(F32)
    n = n + (nonneg & (d <= 2048) & ((d & 15) == 0)).astype(F32)
    return n


def _sattn_kernel(sl_ref, q_ref, kn_ref, vn_ref, kc_ref, vc_ref, o_ref, acc, m_s, l_s, *, tc, wb, ts):
    c = pl.program_id(1)
    last = pl.num_programs(1) - 1

    @pl.when(c == 0)
    def _():
        m_s[...] = jnp.full(m_s.shape, -MASKED, F32)
        l_s[...] = jnp.zeros(l_s.shape, F32)
        acc[...] = jnp.zeros(acc.shape, F32)

    def pad_rows(x, n):
        return jnp.concatenate([x, jnp.zeros((n - x.shape[0], x.shape[1]), F32)], axis=0)

    def update(h, q, k, v, d):
        mult = _multiplicity(d)
        dist = jnp.where(mult > 0.0, d.astype(F32), MASKED)
        s = lax.dot_general(q, k, (((1,), (1,)), ((), ())), preferred_element_type=F32)
        s = s * SCALE - sl_ref[h] * dist
        m_prev = m_s[h]
        m_new = jnp.maximum(m_prev, jnp.max(s, axis=-1, keepdims=True))
        alpha = jnp.exp(m_prev - m_new)
        p = mult * jnp.exp(s - jnp.concatenate([m_new] * (s.shape[1] // LANES), axis=1))
        l_s[h] = alpha * l_s[h] + jnp.sum(p, axis=-1, keepdims=True)
        acc[h] = alpha * acc[h] + jnp.dot(p.astype(BF16), v, preferred_element_type=F32)
        m_s[h] = m_new

    t_c = lax.broadcasted_iota(jnp.int32, (SQ, tc), 0)
    c_c = lax.broadcasted_iota(jnp.int32, (SQ, tc), 1)
    d_cache = (wb + t_c) - (c * tc + c_c)
    for h in range(N_HEADS):
        cols = slice(h * HEAD_DIM, (h + 1) * HEAD_DIM)
        q = pad_rows(q_ref[:, cols], SQ).astype(BF16)
        k = kc_ref[pl.ds(h, tc, stride=N_HEADS), :].astype(BF16)
        v = vc_ref[pl.ds(h, tc, stride=N_HEADS), :].astype(BF16)
        update(h, q, k, v, d_cache)

    @pl.when(c == last)
    def _():
        t_n = lax.broadcasted_iota(jnp.int32, (SQ, LANES), 0)
        c_n = lax.broadcasted_iota(jnp.int32, (SQ, LANES), 1)
        d_new = jnp.where(c_n < ts, t_n - c_n, -1)
        for h in range(N_HEADS):
            cols = slice(h * HEAD_DIM, (h + 1) * HEAD_DIM)
            q = pad_rows(q_ref[:, cols], SQ).astype(BF16)
            k = pad_rows(kn_ref[:, cols], LANES).astype(BF16)
            v = pad_rows(vn_ref[:, cols], LANES).astype(BF16)
            update(h, q, k, v, d_new)
            o_ref[:, cols] = (acc[h] / l_s[h])[:ts, :]


def sample_attention(slopes, q, k_new, v_new, cache_k, cache_v, layer):
    bs, ts, _ = q.shape
    wb = cache_k.shape[2] // N_HEADS
    tc = 512
    assert wb % tc == 0 and ts <= SQ and wb == WINDOW_MAX
    new = pl.BlockSpec((None, ts, D_ATT), lambda b, c: (b, 0, 0))
    cache = pl.BlockSpec((None, None, tc * N_HEADS, HEAD_DIM), lambda b, c: (layer, b, c, 0))
    return pl.pallas_call(
        functools.partial(_sattn_kernel, tc=tc, wb=wb, ts=ts),
        out_shape=jax.ShapeDtypeStruct((bs, ts, D_ATT), F32),
        grid=(bs, wb // tc),
        in_specs=[pl.BlockSpec(memory_space=pltpu.SMEM), new, new, new, cache, cache],
        out_specs=new,
        scratch_shapes=[pltpu.VMEM((N_HEADS, SQ, LANES), F32)] * 3,
        compiler_params=_params(2), name="sample_attention",
    )(slopes, q, k_new, v_new, cache_k, cache_v)


def _ssgu_kernel(att_ref, g_ref, u_ref, wt_ref, b_ref, att_o, ug_o, *, ts):
    att_o[...] = att_ref[...].astype(BF16)
    n = att_ref.shape[0]
    r_i = lax.broadcasted_iota(jnp.int32, (n, n), 0)
    c_i = lax.broadcasted_iota(jnp.int32, (n, n), 1)
    keep = ((r_i & -ts) == (c_i & -ts)) & ((c_i & (ts - 1)) <= (r_i & (ts - 1)))
    for gi in range(N_GROUPS):
        cols = slice(gi * HEAD_DIM, (gi + 1) * HEAD_DIM)
        w = jnp.where(keep, wt_ref[gi], 0.0).astype(BF16)
        gate = jnp.dot(w, g_ref[:, cols].astype(BF16), preferred_element_type=F32) + b_ref[:, gi:gi + 1]
        ug_o[:, cols] = (u_ref[:, cols] * gate).astype(BF16)


def sample_sgu(att, g, u, w_tiled, b_rows, ts):
    n = att.shape[0]
    return pl.pallas_call(
        functools.partial(_ssgu_kernel, ts=ts),
        out_shape=[jax.ShapeDtypeStruct((n, D_ATT), BF16), jax.ShapeDtypeStruct((n, D_SGU), BF16)],
        compiler_params=pltpu.CompilerParams(vmem_limit_bytes=VMEM_LIMIT),
        name="sample_sgu",
    )(att, g, u, w_tiled, b_rows)


def kernel(x_prompt, x_sample, cache_k_win, cache_v_win, norm1, w_in, q_gain, k_gain, sgu_gain,
           w_spatial, b_spatial, w_out, norm2, w_gate, w_up, w_down):
    bp, sp, _ = x_prompt.shape
    bs, ts, _ = x_sample.shape
    assert ts & (ts - 1) == 0
    depth = w_in.shape[0]
    wb = cache_k_win.shape[2]
    assert sp == WINDOW_MAX and ts <= 8 and wb == WINDOW_MAX
    mp, ms = bp * sp, bs * ts

    xp = x_prompt.reshape(mp, D_MODEL)
    xs = x_sample.reshape(ms, D_MODEL)
    cache_k = cache_k_win.reshape(depth, bs, wb * N_HEADS, HEAD_DIM)
    cache_v = cache_v_win.reshape(depth, bs, wb * N_HEADS, HEAD_DIM)
    slopes = jnp.exp2(-8.0 * jnp.arange(1, N_HEADS + 1, dtype=F32) / N_HEADS)
    b_t = jnp.swapaxes(b_spatial, 1, 2)

    k_stack = v_stack = None
    ks_rows, vs_rows, gs_rows = [], [], []
    for l in range(depth):
        qg = jnp.tile(q_gain[l], N_HEADS)
        kg = jnp.tile(k_gain[l], N_HEADS)
        sg = sgu_gain[l].reshape(D_SGU)

        hp = rms_rows(xp, norm1[l])
        hs = rms_rows(xs, norm1[l])
        sec = functools.partial(in_proj_section, hp, hs, w_in, l, seq=sp, depth=depth)
        q_s, *q_p = sec(0 * D_ATT, qg, want_f32=False, want_bf16=False, regroup=True)
        k_s, k_stack, *k_p = sec(1 * D_ATT, kg, want_f32=True, want_bf16=False, regroup=True,
                                 stack=k_stack)
        v_s, v_stack, *v_p = sec(2 * D_ATT, None, want_f32=True, want_bf16=False, regroup=True,
                                 stack=v_stack)
        u_s, u_p = sec(3 * D_ATT, None, want_f32=True, want_bf16=False, stack="unstacked")
        g_s, g_p = sec(3 * D_ATT + D_SGU, sg, want_f32=False, want_bf16=True)

        att_p = prompt_attention(slopes, q_p, k_p, v_p, bp, sp).reshape(mp, D_ATT)
        ug_p = prompt_sgu(g_p, u_p, w_spatial, b_t, l)

        r3 = lambda a: a.reshape(bs, ts, D_ATT)
        att_s = sample_attention(slopes, r3(q_s), r3(k_s), r3(v_s), cache_k, cache_v, l)
        w_tiled = jnp.tile(w_spatial[l][:, :ts, :ts], (1, bs, bs))
        b_rows = jnp.tile(b_spatial[l][:, :ts].T, (bs, 1))
        att_sb, ug_s = sample_sgu(att_s.reshape(ms, D_ATT), g_s, u_s, w_tiled, b_rows, ts)

        xp, xs = out_proj(att_p, ug_p, att_sb, ug_s, w_out, xp, xs, l)
        a_p, a_s = ffn_gate_up(rms_rows(xp, norm2[l]), rms_rows(xs, norm2[l]), w_gate, w_up, l)
        xp, xs = ffn_down(a_p, a_s, w_down, xp, xs, l)

        ks_rows.append(k_s)
        vs_rows.append(v_s)
        gs_rows.append(g_s)

    heads = lambda a, b, t: a.reshape(depth, b, t, N_HEADS, HEAD_DIM)
    return (xp.reshape(bp, sp, D_MODEL), xs.reshape(bs, ts, D_MODEL),
            heads(k_stack, bp, sp), heads(v_stack, bp, sp),
            heads(jnp.stack(ks_rows), bs, ts), heads(jnp.stack(vs_rows), bs, ts),
            heads(jnp.stack(gs_rows), bs, ts))
```

```python
import functools

import jax
import jax.numpy as jnp
from jax import lax
from jax.experimental import pallas as pl
from jax.experimental.pallas import tpu as pltpu

F32 = jnp.float32
BF16 = jnp.bfloat16

D_MODEL = 4096
HEAD_DIM = 128
N_HEADS = 16
D_ATT = N_HEADS * HEAD_DIM
D_SGU = D_MODEL - D_ATT
N_GROUPS = D_SGU // HEAD_DIM
CHUNK = 128
WINDOW_MAX = 2048
RMS_EPS = 1e-6
SCALE = HEAD_DIM ** -0.5
MASKED = 1e30
LANES = 128
VMEM_LIMIT = 60 * 1024 * 1024

HEADS_PER_STEP = 4
QB = 128

TM_PROJ = 1024
TN_PROJ = 512
TN_FFN = 256
TM_DOWN = 256
TN_DOWN = 512


def _params(n_axes):
    return pltpu.CompilerParams(dimension_semantics=("arbitrary",) * n_axes,
                                vmem_limit_bytes=VMEM_LIMIT)


def _cast_weight(w_ref, wb_ref):
    k = w_ref.shape[0]
    ck = 256
    def body(c, carry):
        r = pl.multiple_of(c * ck, ck)
        wb_ref[pl.ds(r, ck), :] = w_ref[pl.ds(r, ck), :].astype(BF16)
        return carry
    lax.fori_loop(0, k // ck, body, 0)


def _is_first_row_tile():
    return pl.program_id(1) == 0


def _is_last_row_tile():
    return pl.program_id(1) == pl.num_programs(1) - 1


def _rms_rows_kernel(x_ref, g_ref, o_ref):
    x = x_ref[...]
    ms = jnp.mean(x * x, axis=-1, keepdims=True)
    o_ref[...] = (x * lax.rsqrt(ms + RMS_EPS) * g_ref[...]).astype(o_ref.dtype)


def rms_rows(x, gain):
    m, d = x.shape
    tm = min(m, 256)
    return pl.pallas_call(
        _rms_rows_kernel,
        out_shape=jax.ShapeDtypeStruct((m, d), BF16),
        grid=(m // tm,),
        in_specs=[pl.BlockSpec((tm, d), lambda i: (i, 0)),
                  pl.BlockSpec((1, d), lambda i: (0, 0))],
        out_specs=pl.BlockSpec((tm, d), lambda i: (i, 0)),
        compiler_params=_params(1),
        name="rms_rows",
    )(x, gain.reshape(1, d))


def _head_norm(z, gain):
    parts = []
    for hh in range(z.shape[1] // HEAD_DIM):
        cols = slice(hh * HEAD_DIM, (hh + 1) * HEAD_DIM)
        zh = z[:, cols]
        ms = jnp.mean(zh * zh, axis=-1, keepdims=True)
        parts.append(zh * lax.rsqrt(ms + RMS_EPS) * gain[:, cols])
    return jnp.concatenate(parts, axis=1)


def _inproj_kernel(*refs, norm, want_f32, want_bf16, regroup, has_alias, ni, n_tiles):
    it = iter(refs)
    h_ref = next(it)
    hs_ref = next(it)
    w_ref = next(it)
    gain_ref = next(it) if norm else None
    if has_alias:
        next(it)
    s_ref = next(it)
    f32_ref = next(it) if want_f32 else None
    bf_ref = next(it) if want_bf16 else None
    o4_ref = next(it) if regroup else None
    o16_ref = next(it) if regroup else None
    wb_ref = next(it)
    z_refs = (next(it), next(it))
    z4_ref = next(it) if regroup else None

    s = pl.program_id(0)
    tm = h_ref.shape[0]
    n_slabs = wb_ref.shape[1] // LANES
    active = s < n_tiles

    def finish(z_ref):
        for c in range(n_slabs):
            cols = slice(c * LANES, (c + 1) * LANES)
            z = z_ref[c]
            if norm:
                z = _head_norm(z, gain_ref[:, cols])
            if want_f32:
                f32_ref[:, cols] = z
            if want_bf16:
                bf_ref[:, cols] = z.astype(BF16)
            if regroup:
                if norm:
                    z_ref[c] = z
                for r in range(4):
                    z4 = z_ref[c, pl.ds(r, tm // 4, stride=4), :]
                    o4_ref[r, :, cols] = z4.astype(BF16)
                    z4_ref[c, r] = z4
                for r in range(4):
                    for c4 in range(4):
                        o16_ref[r + 4 * c4, :, cols] = (
                            z4_ref[c, r, pl.ds(c4, tm // 16, stride=4), :].astype(BF16))

    def multiply(z_ref):
        wide = 2 * LANES
        for c in range(wb_ref.shape[1] // wide):
            z = jnp.dot(h_ref[...], wb_ref[:, c * wide:(c + 1) * wide], preferred_element_type=F32)
            z_ref[2 * c] = z[:, :LANES]
            z_ref[2 * c + 1] = z[:, LANES:]

    @pl.when(s == 0)
    def _():
        z_refs[1][...] = jnp.zeros(z_refs[1].shape, F32)

    @pl.when(active & (s % ni == 0))
    def _():
        _cast_weight(w_ref, wb_ref)

    for par in range(2):
        @pl.when(active & (s % 2 == par))
        def _():
            finish(z_refs[1 - par])
            multiply(z_refs[par])

    @pl.when(s == n_tiles)
    def _():
        finish(z_refs[(n_tiles - 1) % 2])

    @pl.when(active & (s % ni == ni - 1))
    def _():
        z = jnp.dot(hs_ref[...], wb_ref[...], preferred_element_type=F32)
        if norm:
            z = _head_norm(z, gain_ref[...])
        s_ref[...] = z


def in_proj_section(h, hs, w_in, layer, col0, gain, *, want_f32, want_bf16, regroup=False,
                    seq=None, stack=None, depth=1):
    m, k = h.shape
    ms = hs.shape[0]
    n_sec = D_ATT
    tn, tm = TN_PROJ, TM_PROJ
    ni, nj = m // tm, n_sec // tn
    n_tiles = ni * nj
    assert ni >= 2
    jb0 = col0 // tn
    norm = gain is not None
    stacked = want_f32 and stack != "unstacked"
    has_alias = stacked and layer > 0

    def cur(s):
        t = jnp.minimum(s, n_tiles - 1)
        return t // ni, t % ni
    def prev(s):
        t = jnp.maximum(s - 1, 0)
        return t // ni, t % ni

    in_specs = [pl.BlockSpec((tm, k), lambda s: (cur(s)[1], 0)),
                pl.BlockSpec((ms, k), lambda s: (0, 0)),
                pl.BlockSpec((None, k, tn), lambda s: (layer, 0, jb0 + cur(s)[0]))]
    args = [h, hs, w_in]
    if norm:
        in_specs.append(pl.BlockSpec((1, tn), lambda s: (0, prev(s)[0])))
        args.append(gain.reshape(1, n_sec))
    if has_alias:
        in_specs.append(pl.BlockSpec(memory_space=pl.ANY))
        args.append(stack)

    out_shape = [jax.ShapeDtypeStruct((ms, n_sec), F32)]
    out_specs = [pl.BlockSpec((ms, tn), lambda s: (0, cur(s)[0]))]
    if want_f32:
        row0 = layer * ni if stacked else 0
        out_shape.append(jax.ShapeDtypeStruct(((depth if stacked else 1) * m, n_sec), F32))
        out_specs.append(pl.BlockSpec((tm, tn), lambda s: (row0 + prev(s)[1], prev(s)[0])))
    if want_bf16:
        out_shape.append(jax.ShapeDtypeStruct((m, n_sec), BF16))
        out_specs.append(pl.BlockSpec((tm, tn), lambda s: (prev(s)[1], prev(s)[0])))
    if regroup:
        b = m // seq
        tpb = seq // tm
        grouped = lambda s: (prev(s)[1] // tpb, 0, prev(s)[1] % tpb, prev(s)[0])
        out_shape.append(jax.ShapeDtypeStruct((b, 4, seq // 4, n_sec), BF16))
        out_specs.append(pl.BlockSpec((None, 4, tm // 4, tn), grouped))
        out_shape.append(jax.ShapeDtypeStruct((b, 16, seq // 16, n_sec), BF16))
        out_specs.append(pl.BlockSpec((None, 16, tm // 16, tn), grouped))
    scratch = [pltpu.VMEM((k, tn), BF16)] + [pltpu.VMEM((tn // LANES, tm, LANES), F32)] * 2
    if regroup:
        scratch.append(pltpu.VMEM((tn // LANES, 4, tm // 4, LANES), F32))

    kern = functools.partial(_inproj_kernel, norm=norm, want_f32=want_f32, want_bf16=want_bf16,
                             regroup=regroup, has_alias=has_alias, ni=ni, n_tiles=n_tiles)
    return pl.pallas_call(
        kern, out_shape=out_shape, grid=(n_tiles + 1,), in_specs=in_specs, out_specs=out_specs,
        scratch_shapes=scratch,
        input_output_aliases=({len(args) - 1: 1} if has_alias else {}),
        compiler_params=_params(1), name="in_proj",
    )(*args)


def _attn_tiles(qkv, dist, slopes, prev):
    scores = [lax.dot_general(q, k, (((1,), (1,)), ((), ())), preferred_element_type=F32)
              for q, k, _ in qkv]
    probs = []
    for hh, s in enumerate(scores):
        s = s * SCALE - slopes[hh] * dist
        m_cur = jnp.max(s, axis=-1, keepdims=True)
        if prev is None:
            probs.append((jnp.exp(s - m_cur).astype(BF16), jnp.broadcast_to(m_cur, (QB, LANES)), None))
        else:
            m_prev = prev[hh][1]
            m_new = jnp.maximum(m_prev, m_cur)
            p = jnp.exp(s - jnp.concatenate([m_new] * (s.shape[1] // LANES), axis=1))
            probs.append((p.astype(BF16), m_new, jnp.exp(m_prev - m_new)))
    new = []
    for hh, (p, m_new, alpha) in enumerate(probs):
        v = qkv[hh][2]
        v_ones = jnp.concatenate([v, jnp.ones(v.shape, BF16)], axis=1)
        pv = jnp.dot(p, v_ones, preferred_element_type=F32)
        if prev is None:
            new.append((pv[:, :LANES], m_new, pv[:, LANES:]))
        else:
            new.append((alpha * prev[hh][0] + pv[:, :LANES], m_new, alpha * prev[hh][2] + pv[:, LANES:]))
    return new


def _window_dist(d, dil):
    return jnp.where((d >= 0) & (d <= 128), (d * dil).astype(F32), MASKED)


def _attn_kernel(sl_ref, q4, k4, v4, q16, k16, v16, o_ref, acc, m_s, l_s, dist_s, nat_s, *, seq):
    g = pl.program_id(1)
    s4 = seq // 4
    state = (acc, m_s, l_s)

    o_i = lax.broadcasted_iota(jnp.int32, (QB, 2 * QB), 0)
    c_i = lax.broadcasted_iota(jnp.int32, (QB, 2 * QB), 1)
    d1 = 4 * ((o_i & 31) - (c_i & 63)) + ((o_i >> 5) - (c_i >> 6))
    dist_s[0] = _window_dist(d1, 1)
    dist_s[1] = _window_dist(d1 + 128, 1)
    dist_s[2] = _window_dist(o_i - c_i, 4)
    dist_s[3] = _window_dist(o_i - c_i + 128, 4)
    dist_s[4] = _window_dist(o_i - c_i, 16)

    heads = range(HEADS_PER_STEP)

    def load_state(rows):
        return [tuple(ref[hh, rows, :] for ref in state) for hh in heads]

    def store_state(rows_list, new):
        for hh in heads:
            for ref, val in zip(state, new[hh]):
                n = val.shape[0] // len(rows_list)
                for idx, rows in enumerate(rows_list):
                    ref[hh, rows, :] = val[idx * n:(idx + 1) * n, :]

    def head_cols(hh):
        return slice(hh * HEAD_DIM, (hh + 1) * HEAD_DIM)

    slopes = [sl_ref[g * HEADS_PER_STEP + hh] for hh in heads]

    def body1(qb, carry):
        t0 = pl.multiple_of(qb * 32, 32)
        kt0 = pl.multiple_of(jnp.maximum(qb - 1, 0) * 32, 32)
        dist = dist_s[jnp.minimum(qb, 1)]
        def slabs(ref, start, n, hh):
            return jnp.concatenate([ref[r, pl.ds(start, n), head_cols(hh)] for r in range(4)], axis=0)
        qkv = [(slabs(q4, t0, 32, hh), slabs(k4, kt0, 64, hh), slabs(v4, kt0, 64, hh)) for hh in heads]
        new = _attn_tiles(qkv, dist, slopes, None)
        store_state([pl.ds(r * s4 + t0, 32) for r in range(4)], new)
        return carry
    lax.fori_loop(0, seq // QB, body1, 0, unroll=2)

    n_qb4 = s4 // QB
    assert n_qb4 == 4
    def body4(idx, carry):
        r = idx >> 2
        qb = idx & 3
        q0 = pl.multiple_of(qb * QB, QB)
        k0 = pl.multiple_of(jnp.maximum(qb - 1, 0) * QB, QB)
        dist = dist_s[2 + jnp.minimum(qb, 1)]
        rows = pl.ds(r * s4 + q0, QB)
        qkv = [(q4[r, pl.ds(q0, QB), head_cols(hh)], k4[r, pl.ds(k0, 2 * QB), head_cols(hh)],
                v4[r, pl.ds(k0, 2 * QB), head_cols(hh)]) for hh in heads]
        new = _attn_tiles(qkv, dist, slopes, load_state(rows))
        store_state([rows], new)
        return carry
    lax.fori_loop(0, 4 * n_qb4, body4, 0, unroll=2)

    def body16(r16, carry):
        dist = dist_s[4][:, :QB]
        rows = pl.ds((r16 & 3) * s4 + (r16 >> 2), QB, stride=4)
        qkv = [(q16[r16, :, head_cols(hh)], k16[r16, :, head_cols(hh)], v16[r16, :, head_cols(hh)])
               for hh in heads]
        new = _attn_tiles(qkv, dist, slopes, load_state(rows))
        store_state([rows], new)
        return carry
    lax.fori_loop(0, 16, body16, 0, unroll=2)

    tn = nat_s.shape[1] // 4
    def body_out(c, carry):
        t0 = pl.multiple_of(c * tn, tn)
        for hh in range(HEADS_PER_STEP):
            for r in range(4):
                rows = pl.ds(r * s4 + t0, tn)
                nat_s[hh, pl.ds(r, tn, stride=4), :] = acc[hh, rows, :] / l_s[hh, rows, :]
            o_ref[pl.ds(pl.multiple_of(4 * t0, 4 * tn), 4 * tn), head_cols(hh)] = (
                nat_s[hh].astype(o_ref.dtype))
        return carry
    lax.fori_loop(0, s4 // tn, body_out, 0)


def prompt_attention(slopes, qs, ks, vs, batch, seq):
    assert seq // 16 == QB and seq % (4 * QB) == 0
    tw = HEADS_PER_STEP * HEAD_DIM
    r4 = pl.BlockSpec((None, 4, seq // 4, tw), lambda b, g: (b, 0, 0, g))
    r16 = pl.BlockSpec((None, 16, seq // 16, tw), lambda b, g: (b, 0, 0, g))
    state = pltpu.VMEM((HEADS_PER_STEP, seq, LANES), F32)
    return pl.pallas_call(
        functools.partial(_attn_kernel, seq=seq),
        out_shape=jax.ShapeDtypeStruct((batch, seq, D_ATT), BF16),
        grid=(batch, N_HEADS // HEADS_PER_STEP),
        in_specs=[pl.BlockSpec(memory_space=pltpu.SMEM), r4, r4, r4, r16, r16, r16],
        out_specs=pl.BlockSpec((None, seq, tw), lambda b, g: (b, 0, g)),
        scratch_shapes=[state, state, state,
                        pltpu.VMEM((5, QB, 2 * QB), F32),
                        pltpu.VMEM((HEADS_PER_STEP, 256, LANES), F32)],
        compiler_params=_params(2), name="prompt_attention",
    )(slopes, qs[0], ks[0], vs[0], qs[1], ks[1], vs[1])


def _sgu_kernel(g_ref, u_ref, w_ref, bt_ref, o_ref, wt_ref):
    @pl.when(pl.program_id(0) == 0)
    def _():
        r_i = lax.broadcasted_iota(jnp.int32, (CHUNK, CHUNK), 0)
        c_i = lax.broadcasted_iota(jnp.int32, (CHUNK, CHUNK), 1)
        for gi in range(N_GROUPS):
            wt_ref[gi] = jnp.where(r_i >= c_i, w_ref[gi], 0.0).astype(BF16)

    for gi in range(N_GROUPS):
        cols = slice(gi * HEAD_DIM, (gi + 1) * HEAD_DIM)
        gate = jnp.dot(wt_ref[gi], g_ref[:, cols], preferred_element_type=F32) + bt_ref[:, gi:gi + 1]
        o_ref[:, cols] = (u_ref[:, cols] * gate).astype(BF16)


def prompt_sgu(g, u, w_spatial, b_t, layer):
    m = g.shape[0]
    row = pl.BlockSpec((CHUNK, D_SGU), lambda c: (c, 0))
    return pl.pallas_call(
        _sgu_kernel,
        out_shape=jax.ShapeDtypeStruct((m, D_SGU), BF16),
        grid=(m // CHUNK,),
        in_specs=[row, row,
                  pl.BlockSpec((None, N_GROUPS, CHUNK, CHUNK), lambda c: (layer, 0, 0, 0)),
                  pl.BlockSpec((None, CHUNK, N_GROUPS), lambda c: (layer, 0, 0))],
        out_specs=row,
        scratch_shapes=[pltpu.VMEM((N_GROUPS, CHUNK, CHUNK), BF16)],
        compiler_params=_params(1), name="prompt_sgu",
    )(g, u, w_spatial, b_t)


def _outproj_kernel(a_ref, ug_ref, as_ref, ugs_ref, w_ref, x_ref, xs_ref, o_ref, os_ref, wb_ref):
    @pl.when(_is_first_row_tile())
    def _():
        _cast_weight(w_ref, wb_ref)

    def project(a, ug):
        z = jnp.dot(a, wb_ref[:D_ATT, :], preferred_element_type=F32)
        return z + jnp.dot(ug, wb_ref[D_ATT:, :], preferred_element_type=F32)

    o_ref[...] = x_ref[...] + project(a_ref[...], ug_ref[...])

    @pl.when(_is_last_row_tile())
    def _():
        os_ref[...] = xs_ref[...] + project(as_ref[...], ugs_ref[...])


def out_proj(att, ug, att_s, ug_s, w_out, x, xs, layer):
    m, ms = x.shape[0], xs.shape[0]
    tn, tm = TN_PROJ, TM_PROJ
    lhs = pl.BlockSpec((tm, D_ATT), lambda j, i: (i, 0))
    lhs_s = pl.BlockSpec((ms, D_ATT), lambda j, i: (0, 0))
    res = pl.BlockSpec((tm, tn), lambda j, i: (i, j))
    res_s = pl.BlockSpec((ms, tn), lambda j, i: (0, j))
    return pl.pallas_call(
        _outproj_kernel,
        out_shape=[jax.ShapeDtypeStruct((m, D_MODEL), F32), jax.ShapeDtypeStruct((ms, D_MODEL), F32)],
        grid=(D_MODEL // tn, m // tm),
        in_specs=[lhs, lhs, lhs_s, lhs_s,
                  pl.BlockSpec((None, D_MODEL, tn), lambda j, i: (layer, 0, j)),
                  res, res_s],
        out_specs=[res, res_s],
        scratch_shapes=[pltpu.VMEM((D_MODEL, tn), BF16)],
        compiler_params=_params(2), name="out_proj",
    )(att, ug, att_s, ug_s, w_out, x, xs)


def _ffn1_kernel(h_ref, hs_ref, wg_ref, wu_ref, a_ref, as_ref, wgb_ref, wub_ref):
    @pl.when(_is_first_row_tile())
    def _():
        _cast_weight(wg_ref, wgb_ref)
        _cast_weight(wu_ref, wub_ref)

    def gated(h):
        g = jnp.dot(h, wgb_ref[...], preferred_element_type=F32)
        u = jnp.dot(h, wub_ref[...], preferred_element_type=F32)
        return (g / (1.0 + jnp.exp(-g)) * u).astype(BF16)

    a_ref[...] = gated(h_ref[...])

    @pl.when(_is_last_row_tile())
    def _():
        as_ref[...] = gated(hs_ref[...])


def ffn_gate_up(h, hs, w_gate, w_up, layer):
    m, k = h.shape
    ms = hs.shape[0]
    d_ff = w_gate.shape[2]
    tn, tm = TN_FFN, TM_PROJ
    wspec = pl.BlockSpec((None, k, tn), lambda j, i: (layer, 0, j))
    return pl.pallas_call(
        _ffn1_kernel,
        out_shape=[jax.ShapeDtypeStruct((m, d_ff), BF16), jax.ShapeDtypeStruct((ms, d_ff), BF16)],
        grid=(d_ff // tn, m // tm),
        in_specs=[pl.BlockSpec((tm, k), lambda j, i: (i, 0)),
                  pl.BlockSpec((ms, k), lambda j, i: (0, 0)), wspec, wspec],
        out_specs=[pl.BlockSpec((tm, tn), lambda j, i: (i, j)),
                   pl.BlockSpec((ms, tn), lambda j, i: (0, j))],
        scratch_shapes=[pltpu.VMEM((k, tn), BF16)] * 2,
        compiler_params=_params(2), name="ffn_gate_up",
    )(h, hs, w_gate, w_up)


def _ffn2_kernel(a_ref, as_ref, w_ref, x_ref, xs_ref, o_ref, os_ref, wb_ref):
    @pl.when(_is_first_row_tile())
    def _():
        _cast_weight(w_ref, wb_ref)
    o_ref[...] = x_ref[...] + jnp.dot(a_ref[...], wb_ref[...], preferred_element_type=F32)

    @pl.when(_is_last_row_tile())
    def _():
        os_ref[...] = xs_ref[...] + jnp.dot(as_ref[...], wb_ref[...], preferred_element_type=F32)


def ffn_down(a, a_s, w_down, x, xs, layer):
    m, k = a.shape
    ms = a_s.shape[0]
    tn, tm = TN_DOWN, TM_DOWN
    res = pl.BlockSpec((tm, tn), lambda j, i: (i, j))
    res_s = pl.BlockSpec((ms, tn), lambda j, i: (0, j))
    return pl.pallas_call(
        _ffn2_kernel,
        out_shape=[jax.ShapeDtypeStruct((m, D_MODEL), F32), jax.ShapeDtypeStruct((ms, D_MODEL), F32)],
        grid=(D_MODEL // tn, m // tm),
        in_specs=[pl.BlockSpec((tm, k), lambda j, i: (i, 0)),
                  pl.BlockSpec((ms, k), lambda j, i: (0, 0)),
                  pl.BlockSpec((None, k, tn), lambda j, i: (layer, 0, j),
                               pipeline_mode=pl.Buffered(1)),
                  res, res_s],
        out_specs=[res, res_s],
        scratch_shapes=[pltpu.VMEM((k, tn), BF16)],
        compiler_params=_params(2), name="ffn_down",
    )(a, a_s, w_down, x, xs)


SQ = 16


def _multiplicity(d):
    nonneg = d >= 0
    n = (nonneg & (d <= 128)).astype(F32)
    n = n + (nonneg & (d <= 512) & ((d & 3) == 0)).astype(F32)
    n = n + (nonneg & (d <= 2048) & ((d & 15) == 0)).astype(F32)
    return n


def _sattn_kernel(sl_ref, q_ref, kn_ref, vn_ref, kc_ref, vc_ref, o_ref, acc, m_s, l_s, xk_ref, xv_ref,
                  *, tc, wb, ts):
    c = pl.program_id(1)
    last = pl.num_programs(1) - 1

    @pl.when(c == 0)
    def _():
        m_s[...] = jnp.full(m_s.shape, -MASKED, F32)
        l_s[...] = jnp.zeros(l_s.shape, F32)
        acc[...] = jnp.zeros(acc.shape, F32)

    def pad_rows(x, n):
        return jnp.concatenate([x, jnp.zeros((n - x.shape[0], x.shape[1]), F32)], axis=0)

    def head_cols(h):
        return slice(h * HEAD_DIM, (h + 1) * HEAD_DIM)

    def query(h):
        return pad_rows(q_ref[:, head_cols(h)], SQ).astype(BF16)

    def update(key, value, d):
        mult = _multiplicity(d)
        dist = jnp.where(mult > 0.0, d.astype(F32), MASKED)
        n_rep = d.shape[1] // LANES
        scores = [lax.dot_general(query(h), key(h), (((1,), (1,)), ((), ())),
                                  preferred_element_type=F32) for h in range(N_HEADS)]
        staged = []
        for h, s in enumerate(scores):
            s = s * SCALE - sl_ref[h] * dist
            m_prev = m_s[h]
            m_new = jnp.maximum(m_prev, jnp.max(s, axis=-1, keepdims=True))
            p = mult * jnp.exp(s - jnp.concatenate([m_new] * n_rep, axis=1))
            staged.append((p, m_new, jnp.exp(m_prev - m_new)))
        for h, (p, m_new, alpha) in enumerate(staged):
            l_s[h] = alpha * l_s[h] + jnp.sum(p, axis=-1, keepdims=True)
            acc[h] = alpha * acc[h] + jnp.dot(p.astype(BF16), value(h), preferred_element_type=F32)
            m_s[h] = m_new

    t_c = lax.broadcasted_iota(jnp.int32, (SQ, tc), 0)
    c_c = lax.broadcasted_iota(jnp.int32, (SQ, tc), 1)
    d_cache = (wb + t_c) - (c * tc + c_c)
    for r in range(4):
        xk_ref[r] = kc_ref[pl.ds(r, 4 * tc, stride=4), :]
        xv_ref[r] = vc_ref[pl.ds(r, 4 * tc, stride=4), :]
    update(lambda h: xk_ref[h % 4, pl.ds(h // 4, tc, stride=4), :].astype(BF16),
           lambda h: xv_ref[h % 4, pl.ds(h // 4, tc, stride=4), :].astype(BF16), d_cache)

    @pl.when(c == last)
    def _():
        t_n = lax.broadcasted_iota(jnp.int32, (SQ, LANES), 0)
        c_n = lax.broadcasted_iota(jnp.int32, (SQ, LANES), 1)
        d_new = jnp.where(c_n < ts, t_n - c_n, -1)
        update(lambda h: pad_rows(kn_ref[:, head_cols(h)], LANES).astype(BF16),
               lambda h: pad_rows(vn_ref[:, head_cols(h)], LANES).astype(BF16), d_new)
        for h in range(N_HEADS):
            o_ref[:, head_cols(h)] = (acc[h] / l_s[h])[:ts, :]


def sample_attention(slopes, q, k_new, v_new, cache_k, cache_v, layer):
    bs, ts, _ = q.shape
    wb = cache_k.shape[2] // N_HEADS
    tc = 512
    assert wb % tc == 0 and ts <= SQ and wb == WINDOW_MAX
    new = pl.BlockSpec((None, ts, D_ATT), lambda b, c: (b, 0, 0))
    cache = pl.BlockSpec((None, None, tc * N_HEADS, HEAD_DIM), lambda b, c: (layer, b, c, 0))
    return pl.pallas_call(
        functools.partial(_sattn_kernel, tc=tc, wb=wb, ts=ts),
        out_shape=jax.ShapeDtypeStruct((bs, ts, D_ATT), F32),
        grid=(bs, wb // tc),
        in_specs=[pl.BlockSpec(memory_space=pltpu.SMEM), new, new, new, cache, cache],
        out_specs=new,
        scratch_shapes=[pltpu.VMEM((N_HEADS, SQ, LANES), F32)] * 3
                       + [pltpu.VMEM((4, 4 * tc, HEAD_DIM), F32)] * 2,
        compiler_params=_params(2), name="sample_attention",
    )(slopes, q, k_new, v_new, cache_k, cache_v)


def _ssgu_kernel(att_ref, g_ref, u_ref, wt_ref, b_ref, att_o, ug_o, *, ts):
    att_o[...] = att_ref[...].astype(BF16)
    n = att_ref.shape[0]
    r_i = lax.broadcasted_iota(jnp.int32, (n, n), 0)
    c_i = lax.broadcasted_iota(jnp.int32, (n, n), 1)
    keep = ((r_i & -ts) == (c_i & -ts)) & ((c_i & (ts - 1)) <= (r_i & (ts - 1)))
    for gi in range(N_GROUPS):
        cols = slice(gi * HEAD_DIM, (gi + 1) * HEAD_DIM)
        w = jnp.where(keep, wt_ref[gi], 0.0).astype(BF16)
        gate = jnp.dot(w, g_ref[:, cols].astype(BF16), preferred_element_type=F32) + b_ref[:, gi:gi + 1]
        ug_o[:, cols] = (u_ref[:, cols] * gate).astype(BF16)


def sample_sgu(att, g, u, w_tiled, b_rows, ts):
    n = att.shape[0]
    return pl.pallas_call(
        functools.partial(_ssgu_kernel, ts=ts),
        out_shape=[jax.ShapeDtypeStruct((n, D_ATT), BF16), jax.ShapeDtypeStruct((n, D_SGU), BF16)],
        compiler_params=pltpu.CompilerParams(vmem_limit_bytes=VMEM_LIMIT),
        name="sample_sgu",
    )(att, g, u, w_tiled, b_rows)


def kernel(x_prompt, x_sample, cache_k_win, cache_v_win, norm1, w_in, q_gain, k_gain, sgu_gain,
           w_spatial, b_spatial, w_out, norm2, w_gate, w_up, w_down):
    bp, sp, _ = x_prompt.shape
    bs, ts, _ = x_sample.shape
    assert ts & (ts - 1) == 0
    depth = w_in.shape[0]
    wb = cache_k_win.shape[2]
    assert sp == WINDOW_MAX and ts <= 8 and wb == WINDOW_MAX
    mp, ms = bp * sp, bs * ts

    xp = x_prompt.reshape(mp, D_MODEL)
    xs = x_sample.reshape(ms, D_MODEL)
    cache_k = cache_k_win.reshape(depth, bs, wb * N_HEADS, HEAD_DIM)
    cache_v = cache_v_win.reshape(depth, bs, wb * N_HEADS, HEAD_DIM)
    slopes = jnp.exp2(-8.0 * jnp.arange(1, N_HEADS + 1, dtype=F32) / N_HEADS)
    b_t = jnp.swapaxes(b_spatial, 1, 2)

    k_stack = v_stack = None
    ks_rows, vs_rows, gs_rows = [], [], []
    for l in range(depth):
        qg = jnp.tile(q_gain[l], N_HEADS)
        kg = jnp.tile(k_gain[l], N_HEADS)
        sg = sgu_gain[l].reshape(D_SGU)

        hp = rms_rows(xp, norm1[l])
        hs = rms_rows(xs, norm1[l])
        sec = functools.partial(in_proj_section, hp, hs, w_in, l, seq=sp, depth=depth)
        q_s, *q_p = sec(0 * D_ATT, qg, want_f32=False, want_bf16=False, regroup=True)
        k_s, k_stack, *k_p = sec(1 * D_ATT, kg, want_f32=True, want_bf16=False, regroup=True,
                                 stack=k_stack)
        v_s, v_stack, *v_p = sec(2 * D_ATT, None, want_f32=True, want_bf16=False, regroup=True,
                                 stack=v_stack)
        u_s, u_p = sec(3 * D_ATT, None, want_f32=True, want_bf16=False, stack="unstacked")
        g_s, g_p = sec(3 * D_ATT + D_SGU, sg, want_f32=False, want_bf16=True)

        att_p = prompt_attention(slopes, q_p, k_p, v_p, bp, sp).reshape(mp, D_ATT)
        ug_p = prompt_sgu(g_p, u_p, w_spatial, b_t, l)

        r3 = lambda a: a.reshape(bs, ts, D_ATT)
        att_s = sample_attention(slopes, r3(q_s), r3(k_s), r3(v_s), cache_k, cache_v, l)
        w_tiled = jnp.tile(w_spatial[l][:, :ts, :ts], (1, bs, bs))
        b_rows = jnp.tile(b_spatial[l][:, :ts].T, (bs, 1))
        att_sb, ug_s = sample_sgu(att_s.reshape(ms, D_ATT), g_s, u_s, w_tiled, b_rows, ts)

        xp, xs = out_proj(att_p, ug_p, att_sb, ug_s, w_out, xp, xs, l)
        a_p, a_s = ffn_gate_up(rms_rows(xp, norm2[l]), rms_rows(xs, norm2[l]), w_gate, w_up, l)
        xp, xs = ffn_down(a_p, a_s, w_down, xp, xs, l)

        ks_rows.append(k_s)
        vs_rows.append(v_s)
        gs_rows.append(g_s)

    heads = lambda a, b, t: a.reshape(depth, b, t, N_HEADS, HEAD_DIM)
    return (xp.reshape(bp, sp, D_MODEL), xs.reshape(bs, ts, D_MODEL),
            heads(k_stack, bp, sp), heads(v_stack, bp, sp),
            heads(jnp.stack(ks_rows), bs, ts), heads(jnp.stack(vs_rows), bs, ts),
            heads(jnp.stack(gs_rows), bs, ts))
```

```python
import functools

import jax
import jax.numpy as jnp
from jax import lax
from jax.experimental import pallas as pl
from jax.experimental.pallas import tpu as pltpu

F32 = jnp.float32
BF16 = jnp.bfloat16

D_MODEL = 4096
HEAD_DIM = 128
N_HEADS = 16
D_ATT = N_HEADS * HEAD_DIM
D_SGU = D_MODEL - D_ATT
N_GROUPS = D_SGU // HEAD_DIM
CHUNK = 128
WINDOW_MAX = 2048
RMS_EPS = 1e-6
SCALE = HEAD_DIM ** -0.5
MASKED = 1e30
LANES = 128
VMEM_LIMIT = 60 * 1024 * 1024

HEADS_PER_STEP = 4
QB = 128

TM_PROJ = 1024
TN_PROJ = 512
TM_IN = 512
TN_IN = 1024
TN_FFN = 256
TM_DOWN = 256
TN_DOWN = 512


def _params(n_axes):
    return pltpu.CompilerParams(dimension_semantics=("arbitrary",) * n_axes,
                                vmem_limit_bytes=VMEM_LIMIT)


def _cast_weight(w_ref, wb_ref):
    k = w_ref.shape[0]
    ck = 256
    def body(c, carry):
        r = pl.multiple_of(c * ck, ck)
        wb_ref[pl.ds(r, ck), :] = w_ref[pl.ds(r, ck), :].astype(BF16)
        return carry
    lax.fori_loop(0, k // ck, body, 0)


def _is_first_row_tile():
    return pl.program_id(1) == 0


def _is_last_row_tile():
    return pl.program_id(1) == pl.num_programs(1) - 1


def _rms_rows_kernel(x_ref, g_ref, o_ref):
    x = x_ref[...]
    ms = jnp.mean(x * x, axis=-1, keepdims=True)
    o_ref[...] = (x * lax.rsqrt(ms + RMS_EPS) * g_ref[...]).astype(o_ref.dtype)


def rms_rows(x, gain):
    m, d = x.shape
    tm = min(m, 256)
    return pl.pallas_call(
        _rms_rows_kernel,
        out_shape=jax.ShapeDtypeStruct((m, d), BF16),
        grid=(m // tm,),
        in_specs=[pl.BlockSpec((tm, d), lambda i: (i, 0)),
                  pl.BlockSpec((1, d), lambda i: (0, 0))],
        out_specs=pl.BlockSpec((tm, d), lambda i: (i, 0)),
        compiler_params=_params(1),
        name="rms_rows",
    )(x, gain.reshape(1, d))


def _head_norm(z, gain):
    parts = []
    for hh in range(z.shape[1] // HEAD_DIM):
        cols = slice(hh * HEAD_DIM, (hh + 1) * HEAD_DIM)
        zh = z[:, cols]
        ms = jnp.mean(zh * zh, axis=-1, keepdims=True)
        parts.append(zh * lax.rsqrt(ms + RMS_EPS) * gain[:, cols])
    return jnp.concatenate(parts, axis=1)


def _inproj_kernel(*refs, norm, want_f32, want_bf16, regroup, has_alias, ni, n_tiles):
    it = iter(refs)
    h_ref = next(it)
    hs_ref = next(it)
    w_ref = next(it)
    gain_ref = next(it) if norm else None
    if has_alias:
        next(it)
    s_ref = next(it)
    f32_ref = next(it) if want_f32 else None
    bf_ref = next(it) if want_bf16 else None
    o4_ref = next(it) if regroup else None
    o16_ref = next(it) if regroup else None
    wb_ref = next(it)
    z_refs = (next(it), next(it))
    z4_ref = next(it) if regroup else None

    s = pl.program_id(0)
    tm = h_ref.shape[0]
    n_slabs = wb_ref.shape[1] // LANES
    active = s < n_tiles

    def finish(z_ref):
        for c in range(n_slabs):
            cols = slice(c * LANES, (c + 1) * LANES)
            z = z_ref[c]
            if norm:
                z = _head_norm(z, gain_ref[:, cols])
            if want_f32:
                f32_ref[:, cols] = z
            if want_bf16:
                bf_ref[:, cols] = z.astype(BF16)
            if regroup:
                if norm:
                    z_ref[c] = z
                for r in range(4):
                    z4 = z_ref[c, pl.ds(r, tm // 4, stride=4), :]
                    o4_ref[r, :, cols] = z4.astype(BF16)
                    z4_ref[c, r] = z4
                for r in range(4):
                    for c4 in range(4):
                        o16_ref[r + 4 * c4, :, cols] = (
                            z4_ref[c, r, pl.ds(c4, tm // 16, stride=4), :].astype(BF16))

    def multiply(z_ref):
        wide = 2 * LANES
        for c in range(wb_ref.shape[1] // wide):
            z = jnp.dot(h_ref[...], wb_ref[:, c * wide:(c + 1) * wide], preferred_element_type=F32)
            z_ref[2 * c] = z[:, :LANES]
            z_ref[2 * c + 1] = z[:, LANES:]

    @pl.when(s == 0)
    def _():
        z_refs[1][...] = jnp.zeros(z_refs[1].shape, F32)

    @pl.when(active & (s % ni == 0))
    def _():
        _cast_weight(w_ref, wb_ref)

    for par in range(2):
        @pl.when(active & (s % 2 == par))
        def _():
            finish(z_refs[1 - par])
            multiply(z_refs[par])

    @pl.when(s == n_tiles)
    def _():
        finish(z_refs[(n_tiles - 1) % 2])

    @pl.when(active & (s % ni == ni - 1))
    def _():
        z = jnp.dot(hs_ref[...], wb_ref[...], preferred_element_type=F32)
        if norm:
            z = _head_norm(z, gain_ref[...])
        s_ref[...] = z


def in_proj_section(h, hs, w_in, layer, col0, gain, *, want_f32, want_bf16, regroup=False,
                    seq=None, stack=None, depth=1):
    m, k = h.shape
    ms = hs.shape[0]
    n_sec = D_ATT
    tn, tm = TN_IN, TM_IN
    ni, nj = m // tm, n_sec // tn
    n_tiles = ni * nj
    assert ni >= 2
    jb0 = col0 // tn
    norm = gain is not None
    stacked = want_f32 and stack != "unstacked"
    has_alias = stacked and layer > 0

    def cur(s):
        t = jnp.minimum(s, n_tiles - 1)
        return t // ni, t % ni
    def prev(s):
        t = jnp.maximum(s - 1, 0)
        return t // ni, t % ni

    in_specs = [pl.BlockSpec((tm, k), lambda s: (cur(s)[1], 0)),
                pl.BlockSpec((ms, k), lambda s: (0, 0)),
                pl.BlockSpec((None, k, tn), lambda s: (layer, 0, jb0 + cur(s)[0]),
                             pipeline_mode=pl.Buffered(1))]
    args = [h, hs, w_in]
    if norm:
        in_specs.append(pl.BlockSpec((1, tn), lambda s: (0, prev(s)[0])))
        args.append(gain.reshape(1, n_sec))
    if has_alias:
        in_specs.append(pl.BlockSpec(memory_space=pl.ANY))
        args.append(stack)

    out_shape = [jax.ShapeDtypeStruct((ms, n_sec), F32)]
    out_specs = [pl.BlockSpec((ms, tn), lambda s: (0, cur(s)[0]))]
    if want_f32:
        row0 = layer * ni if stacked else 0
        out_shape.append(jax.ShapeDtypeStruct(((depth if stacked else 1) * m, n_sec), F32))
        out_specs.append(pl.BlockSpec((tm, tn), lambda s: (row0 + prev(s)[1], prev(s)[0])))
    if want_bf16:
        out_shape.append(jax.ShapeDtypeStruct((m, n_sec), BF16))
        out_specs.append(pl.BlockSpec((tm, tn), lambda s: (prev(s)[1], prev(s)[0])))
    if regroup:
        b = m // seq
        tpb = seq // tm
        grouped = lambda s: (prev(s)[1] // tpb, 0, prev(s)[1] % tpb, prev(s)[0])
        out_shape.append(jax.ShapeDtypeStruct((b, 4, seq // 4, n_sec), BF16))
        out_specs.append(pl.BlockSpec((None, 4, tm // 4, tn), grouped))
        out_shape.append(jax.ShapeDtypeStruct((b, 16, seq // 16, n_sec), BF16))
        out_specs.append(pl.BlockSpec((None, 16, tm // 16, tn), grouped))
    scratch = [pltpu.VMEM((k, tn), BF16)] + [pltpu.VMEM((tn // LANES, tm, LANES), F32)] * 2
    if regroup:
        scratch.append(pltpu.VMEM((tn // LANES, 4, tm // 4, LANES), F32))

    kern = functools.partial(_inproj_kernel, norm=norm, want_f32=want_f32, want_bf16=want_bf16,
                             regroup=regroup, has_alias=has_alias, ni=ni, n_tiles=n_tiles)
    return pl.pallas_call(
        kern, out_shape=out_shape, grid=(n_tiles + 1,), in_specs=in_specs, out_specs=out_specs,
        scratch_shapes=scratch,
        input_output_aliases=({len(args) - 1: 1} if has_alias else {}),
        compiler_params=_params(1), name="in_proj",
    )(*args)


def _attn_tiles(qkv, dist, slopes, prev):
    scores = [lax.dot_general(q, k, (((1,), (1,)), ((), ())), preferred_element_type=F32)
              for q, k, _ in qkv]
    probs = []
    for hh, s in enumerate(scores):
        s = s * SCALE - slopes[hh] * dist
        m_cur = jnp.max(s, axis=-1, keepdims=True)
        if prev is None:
            probs.append((jnp.exp(s - m_cur).astype(BF16), jnp.broadcast_to(m_cur, (QB, LANES)), None))
        else:
            m_prev = prev[hh][1]
            m_new = jnp.maximum(m_prev, m_cur)
            p = jnp.exp(s - jnp.concatenate([m_new] * (s.shape[1] // LANES), axis=1))
            probs.append((p.astype(BF16), m_new, jnp.exp(m_prev - m_new)))
    new = []
    for hh, (p, m_new, alpha) in enumerate(probs):
        v = qkv[hh][2]
        v_ones = jnp.concatenate([v, jnp.ones(v.shape, BF16)], axis=1)
        pv = jnp.dot(p, v_ones, preferred_element_type=F32)
        if prev is None:
            new.append((pv[:, :LANES], m_new, pv[:, LANES:]))
        else:
            new.append((alpha * prev[hh][0] + pv[:, :LANES], m_new, alpha * prev[hh][2] + pv[:, LANES:]))
    return new


def _window_dist(d, dil):
    return jnp.where((d >= 0) & (d <= 128), (d * dil).astype(F32), MASKED)


def _attn_kernel(sl_ref, q4, k4, v4, q16, k16, v16, o_ref, acc, m_s, l_s, dist_s, nat_s, *, seq):
    g = pl.program_id(1)
    s4 = seq // 4
    state = (acc, m_s, l_s)

    o_i = lax.broadcasted_iota(jnp.int32, (QB, 2 * QB), 0)
    c_i = lax.broadcasted_iota(jnp.int32, (QB, 2 * QB), 1)
    d1 = 4 * ((o_i & 31) - (c_i & 63)) + ((o_i >> 5) - (c_i >> 6))
    dist_s[0] = _window_dist(d1, 1)
    dist_s[1] = _window_dist(d1 + 128, 1)
    dist_s[2] = _window_dist(o_i - c_i, 4)
    dist_s[3] = _window_dist(o_i - c_i + 128, 4)
    dist_s[4] = _window_dist(o_i - c_i, 16)

    heads = range(HEADS_PER_STEP)

    def load_state(rows):
        return [tuple(ref[hh, rows, :] for ref in state) for hh in heads]

    def store_state(rows_list, new):
        for hh in heads:
            for ref, val in zip(state, new[hh]):
                n = val.shape[0] // len(rows_list)
                for idx, rows in enumerate(rows_list):
                    ref[hh, rows, :] = val[idx * n:(idx + 1) * n, :]

    def head_cols(hh):
        return slice(hh * HEAD_DIM, (hh + 1) * HEAD_DIM)

    slopes = [sl_ref[g * HEADS_PER_STEP + hh] for hh in heads]

    def body1(qb, carry):
        t0 = pl.multiple_of(qb * 32, 32)
        kt0 = pl.multiple_of(jnp.maximum(qb - 1, 0) * 32, 32)
        dist = dist_s[jnp.minimum(qb, 1)]
        def slabs(ref, start, n, hh):
            return jnp.concatenate([ref[r, pl.ds(start, n), head_cols(hh)] for r in range(4)], axis=0)
        qkv = [(slabs(q4, t0, 32, hh), slabs(k4, kt0, 64, hh), slabs(v4, kt0, 64, hh)) for hh in heads]
        new = _attn_tiles(qkv, dist, slopes, None)
        store_state([pl.ds(r * s4 + t0, 32) for r in range(4)], new)
        return carry
    lax.fori_loop(0, seq // QB, body1, 0, unroll=2)

    n_qb4 = s4 // QB
    assert n_qb4 == 4
    def body4(idx, carry):
        r = idx >> 2
        qb = idx & 3
        q0 = pl.multiple_of(qb * QB, QB)
        k0 = pl.multiple_of(jnp.maximum(qb - 1, 0) * QB, QB)
        dist = dist_s[2 + jnp.minimum(qb, 1)]
        rows = pl.ds(r * s4 + q0, QB)
        qkv = [(q4[r, pl.ds(q0, QB), head_cols(hh)], k4[r, pl.ds(k0, 2 * QB), head_cols(hh)],
                v4[r, pl.ds(k0, 2 * QB), head_cols(hh)]) for hh in heads]
        new = _attn_tiles(qkv, dist, slopes, load_state(rows))
        store_state([rows], new)
        return carry
    lax.fori_loop(0, 4 * n_qb4, body4, 0, unroll=2)

    def body16(r16, carry):
        dist = dist_s[4][:, :QB]
        rows = pl.ds((r16 & 3) * s4 + (r16 >> 2), QB, stride=4)
        qkv = [(q16[r16, :, head_cols(hh)], k16[r16, :, head_cols(hh)], v16[r16, :, head_cols(hh)])
               for hh in heads]
        new = _attn_tiles(qkv, dist, slopes, load_state(rows))
        store_state([rows], new)
        return carry
    lax.fori_loop(0, 16, body16, 0, unroll=2)

    tn = nat_s.shape[1] // 4
    def body_out(c, carry):
        t0 = pl.multiple_of(c * tn, tn)
        for hh in range(HEADS_PER_STEP):
            for r in range(4):
                rows = pl.ds(r * s4 + t0, tn)
                nat_s[hh, pl.ds(r, tn, stride=4), :] = acc[hh, rows, :] / l_s[hh, rows, :]
            o_ref[pl.ds(pl.multiple_of(4 * t0, 4 * tn), 4 * tn), head_cols(hh)] = (
                nat_s[hh].astype(o_ref.dtype))
        return carry
    lax.fori_loop(0, s4 // tn, body_out, 0)


def prompt_attention(slopes, qs, ks, vs, batch, seq):
    assert seq // 16 == QB and seq % (4 * QB) == 0
    tw = HEADS_PER_STEP * HEAD_DIM
    r4 = pl.BlockSpec((None, 4, seq // 4, tw), lambda b, g: (b, 0, 0, g))
    r16 = pl.BlockSpec((None, 16, seq // 16, tw), lambda b, g: (b, 0, 0, g))
    state = pltpu.VMEM((HEADS_PER_STEP, seq, LANES), F32)
    return pl.pallas_call(
        functools.partial(_attn_kernel, seq=seq),
        out_shape=jax.ShapeDtypeStruct((batch, seq, D_ATT), BF16),
        grid=(batch, N_HEADS // HEADS_PER_STEP),
        in_specs=[pl.BlockSpec(memory_space=pltpu.SMEM), r4, r4, r4, r16, r16, r16],
        out_specs=pl.BlockSpec((None, seq, tw), lambda b, g: (b, 0, g)),
        scratch_shapes=[state, state, state,
                        pltpu.VMEM((5, QB, 2 * QB), F32),
                        pltpu.VMEM((HEADS_PER_STEP, 256, LANES), F32)],
        compiler_params=_params(2), name="prompt_attention",
    )(slopes, qs[0], ks[0], vs[0], qs[1], ks[1], vs[1])


def _sgu_kernel(g_ref, u_ref, w_ref, bt_ref, o_ref, wt_ref):
    @pl.when(pl.program_id(0) == 0)
    def _():
        r_i = lax.broadcasted_iota(jnp.int32, (CHUNK, CHUNK), 0)
        c_i = lax.broadcasted_iota(jnp.int32, (CHUNK, CHUNK), 1)
        for gi in range(N_GROUPS):
            wt_ref[gi] = jnp.where(r_i >= c_i, w_ref[gi], 0.0).astype(BF16)

    for gi in range(N_GROUPS):
        cols = slice(gi * HEAD_DIM, (gi + 1) * HEAD_DIM)
        gate = jnp.dot(wt_ref[gi], g_ref[:, cols], preferred_element_type=F32) + bt_ref[:, gi:gi + 1]
        o_ref[:, cols] = (u_ref[:, cols] * gate).astype(BF16)


def prompt_sgu(g, u, w_spatial, b_t, layer):
    m = g.shape[0]
    row = pl.BlockSpec((CHUNK, D_SGU), lambda c: (c, 0))
    return pl.pallas_call(
        _sgu_kernel,
        out_shape=jax.ShapeDtypeStruct((m, D_SGU), BF16),
        grid=(m // CHUNK,),
        in_specs=[row, row,
                  pl.BlockSpec((None, N_GROUPS, CHUNK, CHUNK), lambda c: (layer, 0, 0, 0)),
                  pl.BlockSpec((None, CHUNK, N_GROUPS), lambda c: (layer, 0, 0))],
        out_specs=row,
        scratch_shapes=[pltpu.VMEM((N_GROUPS, CHUNK, CHUNK), BF16)],
        compiler_params=_params(1), name="prompt_sgu",
    )(g, u, w_spatial, b_t)


def _outproj_kernel(a_ref, ug_ref, as_ref, ugs_ref, w_ref, x_ref, xs_ref, o_ref, os_ref, wb_ref):
    @pl.when(_is_first_row_tile())
    def _():
        _cast_weight(w_ref, wb_ref)

    def project(a, ug):
        z = jnp.dot(a, wb_ref[:D_ATT, :], preferred_element_type=F32)
        return z + jnp.dot(ug, wb_ref[D_ATT:, :], preferred_element_type=F32)

    o_ref[...] = x_ref[...] + project(a_ref[...], ug_ref[...])

    @pl.when(_is_last_row_tile())
    def _():
        os_ref[...] = xs_ref[...] + project(as_ref[...], ugs_ref[...])


def out_proj(att, ug, att_s, ug_s, w_out, x, xs, layer):
    m, ms = x.shape[0], xs.shape[0]
    tn, tm = TN_IN, TM_IN
    lhs = pl.BlockSpec((tm, D_ATT), lambda j, i: (i, 0))
    lhs_s = pl.BlockSpec((ms, D_ATT), lambda j, i: (0, 0))
    res = pl.BlockSpec((tm, tn), lambda j, i: (i, j))
    res_s = pl.BlockSpec((ms, tn), lambda j, i: (0, j))
    return pl.pallas_call(
        _outproj_kernel,
        out_shape=[jax.ShapeDtypeStruct((m, D_MODEL), F32), jax.ShapeDtypeStruct((ms, D_MODEL), F32)],
        grid=(D_MODEL // tn, m // tm),
        in_specs=[lhs, lhs, lhs_s, lhs_s,
                  pl.BlockSpec((None, D_MODEL, tn), lambda j, i: (layer, 0, j),
                               pipeline_mode=pl.Buffered(1)),
                  res, res_s],
        out_specs=[res, res_s],
        scratch_shapes=[pltpu.VMEM((D_MODEL, tn), BF16)],
        compiler_params=_params(2), name="out_proj",
    )(att, ug, att_s, ug_s, w_out, x, xs)


def _ffn1_kernel(h_ref, hs_ref, wg_ref, wu_ref, a_ref, as_ref, wgb_ref, wub_ref):
    @pl.when(_is_first_row_tile())
    def _():
        _cast_weight(wg_ref, wgb_ref)
        _cast_weight(wu_ref, wub_ref)

    def gated(h):
        g = jnp.dot(h, wgb_ref[...], preferred_element_type=F32)
        u = jnp.dot(h, wub_ref[...], preferred_element_type=F32)
        return (g / (1.0 + jnp.exp(-g)) * u).astype(BF16)

    a_ref[...] = gated(h_ref[...])

    @pl.when(_is_last_row_tile())
    def _():
        as_ref[...] = gated(hs_ref[...])


def ffn_gate_up(h, hs, w_gate, w_up, layer):
    m, k = h.shape
    ms = hs.shape[0]
    d_ff = w_gate.shape[2]
    tn, tm = TN_FFN, TM_PROJ
    wspec = pl.BlockSpec((None, k, tn), lambda j, i: (layer, 0, j))
    return pl.pallas_call(
        _ffn1_kernel,
        out_shape=[jax.ShapeDtypeStruct((m, d_ff), BF16), jax.ShapeDtypeStruct((ms, d_ff), BF16)],
        grid=(d_ff // tn, m // tm),
        in_specs=[pl.BlockSpec((tm, k), lambda j, i: (i, 0)),
                  pl.BlockSpec((ms, k), lambda j, i: (0, 0)), wspec, wspec],
        out_specs=[pl.BlockSpec((tm, tn), lambda j, i: (i, j)),
                   pl.BlockSpec((ms, tn), lambda j, i: (0, j))],
        scratch_shapes=[pltpu.VMEM((k, tn), BF16)] * 2,
        compiler_params=_params(2), name="ffn_gate_up",
    )(h, hs, w_gate, w_up)


def _ffn2_kernel(a_ref, as_ref, w_ref, x_ref, xs_ref, o_ref, os_ref, wb_ref):
    @pl.when(_is_first_row_tile())
    def _():
        _cast_weight(w_ref, wb_ref)
    o_ref[...] = x_ref[...] + jnp.dot(a_ref[...], wb_ref[...], preferred_element_type=F32)

    @pl.when(_is_last_row_tile())
    def _():
        os_ref[...] = xs_ref[...] + jnp.dot(as_ref[...], wb_ref[...], preferred_element_type=F32)


def ffn_down(a, a_s, w_down, x, xs, layer):
    m, k = a.shape
    ms = a_s.shape[0]
    tn, tm = TN_DOWN, TM_DOWN
    res = pl.BlockSpec((tm, tn), lambda j, i: (i, j))
    res_s = pl.BlockSpec((ms, tn), lambda j, i: (0, j))
    return pl.pallas_call(
        _ffn2_kernel,
        out_shape=[jax.ShapeDtypeStruct((m, D_MODEL), F32), jax.ShapeDtypeStruct((ms, D_MODEL), F32)],
        grid=(D_MODEL // tn, m // tm),
        in_specs=[pl.BlockSpec((tm, k), lambda j, i: (i, 0)),
                  pl.BlockSpec((ms, k), lambda j, i: (0, 0)),
                  pl.BlockSpec((None, k, tn), lambda j, i: (layer, 0, j),
                               pipeline_mode=pl.Buffered(1)),
                  res, res_s],
        out_specs=[res, res_s],
        scratch_shapes=[pltpu.VMEM((k, tn), BF16)],
        compiler_params=_params(2), name="ffn_down",
    )(a, a_s, w_down, x, xs)


SQ = 16


def _multiplicity(d):
    nonneg = d >= 0
    n = (nonneg & (d <= 128)).astype(F32)
    n = n + (nonneg & (d <= 512) & ((d & 3) == 0)).astype(F32)
    n = n + (nonneg & (d <= 2048) & ((d & 15) == 0)).astype(F32)
    return n


def _sattn_kernel(sl_ref, q_ref, kn_ref, vn_ref, kc_ref, vc_ref, o_ref, acc, m_s, l_s, xk_ref, xv_ref,
                  *, tc, wb, ts):
    c = pl.program_id(1)
    last = pl.num_programs(1) - 1

    @pl.when(c == 0)
    def _():
        m_s[...] = jnp.full(m_s.shape, -MASKED, F32)
        l_s[...] = jnp.zeros(l_s.shape, F32)
        acc[...] = jnp.zeros(acc.shape, F32)

    def pad_rows(x, n):
        return jnp.concatenate([x, jnp.zeros((n - x.shape[0], x.shape[1]), F32)], axis=0)

    def head_cols(h):
        return slice(h * HEAD_DIM, (h + 1) * HEAD_DIM)

    def query(h):
        return pad_rows(q_ref[:, head_cols(h)], SQ).astype(BF16)

    def update(key, value, d):
        mult = _multiplicity(d)
        dist = jnp.where(mult > 0.0, d.astype(F32), MASKED)
        n_rep = d.shape[1] // LANES
        scores = [lax.dot_general(query(h), key(h), (((1,), (1,)), ((), ())),
                                  preferred_element_type=F32) for h in range(N_HEADS)]
        staged = []
        for h, s in enumerate(scores):
            s = s * SCALE - sl_ref[h] * dist
            m_prev = m_s[h]
            m_new = jnp.maximum(m_prev, jnp.max(s, axis=-1, keepdims=True))
            p = mult * jnp.exp(s - jnp.concatenate([m_new] * n_rep, axis=1))
            staged.append((p, m_new, jnp.exp(m_prev - m_new)))
        for h, (p, m_new, alpha) in enumerate(staged):
            l_s[h] = alpha * l_s[h] + jnp.sum(p, axis=-1, keepdims=True)
            acc[h] = alpha * acc[h] + jnp.dot(p.astype(BF16), value(h), preferred_element_type=F32)
            m_s[h] = m_new

    t_c = lax.broadcasted_iota(jnp.int32, (SQ, tc), 0)
    c_c = lax.broadcasted_iota(jnp.int32, (SQ, tc), 1)
    d_cache = (wb + t_c) - (c * tc + c_c)
    for r in range(4):
        xk_ref[r] = kc_ref[pl.ds(r, 4 * tc, stride=4), :]
        xv_ref[r] = vc_ref[pl.ds(r, 4 * tc, stride=4), :]
    update(lambda h: xk_ref[h % 4, pl.ds(h // 4, tc, stride=4), :].astype(BF16),
           lambda h: xv_ref[h % 4, pl.ds(h // 4, tc, stride=4), :].astype(BF16), d_cache)

    @pl.when(c == last)
    def _():
        t_n = lax.broadcasted_iota(jnp.int32, (SQ, LANES), 0)
        c_n = lax.broadcasted_iota(jnp.int32, (SQ, LANES), 1)
        d_new = jnp.where(c_n < ts, t_n - c_n, -1)
        update(lambda h: pad_rows(kn_ref[:, head_cols(h)], LANES).astype(BF16),
               lambda h: pad_rows(vn_ref[:, head_cols(h)], LANES).astype(BF16), d_new)
        for h in range(N_HEADS):
            o_ref[:, head_cols(h)] = (acc[h] / l_s[h])[:ts, :]


def sample_attention(slopes, q, k_new, v_new, cache_k, cache_v, layer):
    bs, ts, _ = q.shape
    wb = cache_k.shape[2] // N_HEADS
    tc = 512
    assert wb % tc == 0 and ts <= SQ and wb == WINDOW_MAX
    new = pl.BlockSpec((None, ts, D_ATT), lambda b, c: (b, 0, 0))
    cache = pl.BlockSpec((None, None, tc * N_HEADS, HEAD_DIM), lambda b, c: (layer, b, c, 0))
    return pl.pallas_call(
        functools.partial(_sattn_kernel, tc=tc, wb=wb, ts=ts),
        out_shape=jax.ShapeDtypeStruct((bs, ts, D_ATT), F32),
        grid=(bs, wb // tc),
        in_specs=[pl.BlockSpec(memory_space=pltpu.SMEM), new, new, new, cache, cache],
        out_specs=new,
        scratch_shapes=[pltpu.VMEM((N_HEADS, SQ, LANES), F32)] * 3
                       + [pltpu.VMEM((4, 4 * tc, HEAD_DIM), F32)] * 2,
        compiler_params=_params(2), name="sample_attention",
    )(slopes, q, k_new, v_new, cache_k, cache_v)


def _ssgu_kernel(att_ref, g_ref, u_ref, wt_ref, b_ref, att_o, ug_o, *, ts):
    att_o[...] = att_ref[...].astype(BF16)
    n = att_ref.shape[0]
    r_i = lax.broadcasted_iota(jnp.int32, (n, n), 0)
    c_i = lax.broadcasted_iota(jnp.int32, (n, n), 1)
    keep = ((r_i & -ts) == (c_i & -ts)) & ((c_i & (ts - 1)) <= (r_i & (ts - 1)))
    for gi in range(N_GROUPS):
        cols = slice(gi * HEAD_DIM, (gi + 1) * HEAD_DIM)
        w = jnp.where(keep, wt_ref[gi], 0.0).astype(BF16)
        gate = jnp.dot(w, g_ref[:, cols].astype(BF16), preferred_element_type=F32) + b_ref[:, gi:gi + 1]
        ug_o[:, cols] = (u_ref[:, cols] * gate).astype(BF16)


def sample_sgu(att, g, u, w_tiled, b_rows, ts):
    n = att.shape[0]
    return pl.pallas_call(
        functools.partial(_ssgu_kernel, ts=ts),
        out_shape=[jax.ShapeDtypeStruct((n, D_ATT), BF16), jax.ShapeDtypeStruct((n, D_SGU), BF16)],
        compiler_params=pltpu.CompilerParams(vmem_limit_bytes=VMEM_LIMIT),
        name="sample_sgu",
    )(att, g, u, w_tiled, b_rows)


def kernel(x_prompt, x_sample, cache_k_win, cache_v_win, norm1, w_in, q_gain, k_gain, sgu_gain,
           w_spatial, b_spatial, w_out, norm2, w_gate, w_up, w_down):
    bp, sp, _ = x_prompt.shape
    bs, ts, _ = x_sample.shape
    assert ts & (ts - 1) == 0
    depth = w_in.shape[0]
    wb = cache_k_win.shape[2]
    assert sp == WINDOW_MAX and ts <= 8 and wb == WINDOW_MAX
    mp, ms = bp * sp, bs * ts

    xp = x_prompt.reshape(mp, D_MODEL)
    xs = x_sample.reshape(ms, D_MODEL)
    cache_k = cache_k_win.reshape(depth, bs, wb * N_HEADS, HEAD_DIM)
    cache_v = cache_v_win.reshape(depth, bs, wb * N_HEADS, HEAD_DIM)
    slopes = jnp.exp2(-8.0 * jnp.arange(1, N_HEADS + 1, dtype=F32) / N_HEADS)
    b_t = jnp.swapaxes(b_spatial, 1, 2)

    k_stack = v_stack = None
    ks_rows, vs_rows, gs_rows = [], [], []
    for l in range(depth):
        qg = jnp.tile(q_gain[l], N_HEADS)
        kg = jnp.tile(k_gain[l], N_HEADS)
        sg = sgu_gain[l].reshape(D_SGU)

        hp = rms_rows(xp, norm1[l])
        hs = rms_rows(xs, norm1[l])
        sec = functools.partial(in_proj_section, hp, hs, w_in, l, seq=sp, depth=depth)
        q_s, *q_p = sec(0 * D_ATT, qg, want_f32=False, want_bf16=False, regroup=True)
        k_s, k_stack, *k_p = sec(1 * D_ATT, kg, want_f32=True, want_bf16=False, regroup=True,
                                 stack=k_stack)
        v_s, v_stack, *v_p = sec(2 * D_ATT, None, want_f32=True, want_bf16=False, regroup=True,
                                 stack=v_stack)
        u_s, u_p = sec(3 * D_ATT, None, want_f32=True, want_bf16=False, stack="unstacked")
        g_s, g_p = sec(3 * D_ATT + D_SGU, sg, want_f32=False, want_bf16=True)

        att_p = prompt_attention(slopes, q_p, k_p, v_p, bp, sp).reshape(mp, D_ATT)
        ug_p = prompt_sgu(g_p, u_p, w_spatial, b_t, l)

        r3 = lambda a: a.reshape(bs, ts, D_ATT)
        att_s = sample_attention(slopes, r3(q_s), r3(k_s), r3(v_s), cache_k, cache_v, l)
        w_tiled = jnp.tile(w_spatial[l][:, :ts, :ts], (1, bs, bs))
        b_rows = jnp.tile(b_spatial[l][:, :ts].T, (bs, 1))
        att_sb, ug_s = sample_sgu(att_s.reshape(ms, D_ATT), g_s, u_s, w_tiled, b_rows, ts)

        xp, xs = out_proj(att_p, ug_p, att_sb, ug_s, w_out, xp, xs, l)
        a_p, a_s = ffn_gate_up(rms_rows(xp, norm2[l]), rms_rows(xs, norm2[l]), w_gate, w_up, l)
        xp, xs = ffn_down(a_p, a_s, w_down, xp, xs, l)

        ks_rows.append(k_s)
        vs_rows.append(v_s)
        gs_rows.append(g_s)

    heads = lambda a, b, t: a.reshape(depth, b, t, N_HEADS, HEAD_DIM)
    return (xp.reshape(bp, sp, D_MODEL), xs.reshape(bs, ts, D_MODEL),
            heads(k_stack, bp, sp), heads(v_stack, bp, sp),
            heads(jnp.stack(ks_rows), bs, ts), heads(jnp.stack(vs_rows), bs, ts),
            heads(jnp.stack(gs_rows), bs, ts))
```

```python
import functools

import jax
import jax.numpy as jnp
from jax import lax
from jax.experimental import pallas as pl
from jax.experimental.pallas import tpu as pltpu

F32 = jnp.float32
BF16 = jnp.bfloat16

D_MODEL = 4096
HEAD_DIM = 128
N_HEADS = 16
D_ATT = N_HEADS * HEAD_DIM
D_SGU = D_MODEL - D_ATT
N_GROUPS = D_SGU // HEAD_DIM
CHUNK = 128
WINDOW_MAX = 2048
RMS_EPS = 1e-6
SCALE = HEAD_DIM ** -0.5
MASKED = 1e30
LANES = 128
VMEM_LIMIT = 60 * 1024 * 1024

HEADS_PER_STEP = 4
QB = 128

TM_PROJ = 1024
TN_PROJ = 512
TM_IN = 512
TN_IN = 1024
TN_FFN = 256
TM_DOWN = 512
TN_DOWN = 512


def _params(n_axes):
    return pltpu.CompilerParams(dimension_semantics=("arbitrary",) * n_axes,
                                vmem_limit_bytes=VMEM_LIMIT)


def _cast_weight(w_ref, wb_ref):
    k = w_ref.shape[0]
    ck = 256
    def body(c, carry):
        r = pl.multiple_of(c * ck, ck)
        wb_ref[pl.ds(r, ck), :] = w_ref[pl.ds(r, ck), :].astype(BF16)
        return carry
    lax.fori_loop(0, k // ck, body, 0)


def _is_first_row_tile():
    return pl.program_id(1) == 0


def _is_last_row_tile():
    return pl.program_id(1) == pl.num_programs(1) - 1


def _rms_rows_kernel(x_ref, g_ref, o_ref):
    x = x_ref[...]
    ms = jnp.mean(x * x, axis=-1, keepdims=True)
    o_ref[...] = (x * lax.rsqrt(ms + RMS_EPS) * g_ref[...]).astype(o_ref.dtype)


def rms_rows(x, gain):
    m, d = x.shape
    tm = min(m, 256)
    return pl.pallas_call(
        _rms_rows_kernel,
        out_shape=jax.ShapeDtypeStruct((m, d), BF16),
        grid=(m // tm,),
        in_specs=[pl.BlockSpec((tm, d), lambda i: (i, 0)),
                  pl.BlockSpec((1, d), lambda i: (0, 0))],
        out_specs=pl.BlockSpec((tm, d), lambda i: (i, 0)),
        compiler_params=_params(1),
        name="rms_rows",
    )(x, gain.reshape(1, d))


def _head_norm(z, gain):
    parts = []
    for hh in range(z.shape[1] // HEAD_DIM):
        cols = slice(hh * HEAD_DIM, (hh + 1) * HEAD_DIM)
        zh = z[:, cols]
        ms = jnp.mean(zh * zh, axis=-1, keepdims=True)
        parts.append(zh * lax.rsqrt(ms + RMS_EPS) * gain[:, cols])
    return jnp.concatenate(parts, axis=1)


def _inproj_kernel(*refs, norm, want_f32, want_bf16, regroup, has_alias, ni, n_tiles):
    it = iter(refs)
    h_ref = next(it)
    hs_ref = next(it)
    w_ref = next(it)
    gain_ref = next(it) if norm else None
    if has_alias:
        next(it)
    s_ref = next(it)
    f32_ref = next(it) if want_f32 else None
    bf_ref = next(it) if want_bf16 else None
    o4_ref = next(it) if regroup else None
    o16_ref = next(it) if regroup else None
    wb_ref = next(it)
    z_refs = (next(it), next(it))
    z4_ref = next(it) if regroup else None

    s = pl.program_id(0)
    tm = h_ref.shape[0]
    n_slabs = wb_ref.shape[1] // LANES
    active = s < n_tiles

    def finish(z_ref):
        for c in range(n_slabs):
            cols = slice(c * LANES, (c + 1) * LANES)
            z = z_ref[c]
            if norm:
                z = _head_norm(z, gain_ref[:, cols])
            if want_f32:
                f32_ref[:, cols] = z
            if want_bf16:
                bf_ref[:, cols] = z.astype(BF16)
            if regroup:
                if norm:
                    z_ref[c] = z
                for r in range(4):
                    z4 = z_ref[c, pl.ds(r, tm // 4, stride=4), :]
                    o4_ref[r, :, cols] = z4.astype(BF16)
                    z4_ref[c, r] = z4
                for r in range(4):
                    for c4 in range(4):
                        o16_ref[r + 4 * c4, :, cols] = (
                            z4_ref[c, r, pl.ds(c4, tm // 16, stride=4), :].astype(BF16))

    def multiply(z_ref):
        wide = 2 * LANES
        for c in range(wb_ref.shape[1] // wide):
            z = jnp.dot(h_ref[...], wb_ref[:, c * wide:(c + 1) * wide], preferred_element_type=F32)
            z_ref[2 * c] = z[:, :LANES]
            z_ref[2 * c + 1] = z[:, LANES:]

    @pl.when(s == 0)
    def _():
        z_refs[1][...] = jnp.zeros(z_refs[1].shape, F32)

    @pl.when(active & (s % ni == 0))
    def _():
        _cast_weight(w_ref, wb_ref)

    for par in range(2):
        @pl.when(active & (s % 2 == par))
        def _():
            finish(z_refs[1 - par])
            multiply(z_refs[par])

    @pl.when(s == n_tiles)
    def _():
        finish(z_refs[(n_tiles - 1) % 2])

    @pl.when(active & (s % ni == ni - 1))
    def _():
        z = jnp.dot(hs_ref[...], wb_ref[...], preferred_element_type=F32)
        if norm:
            z = _head_norm(z, gain_ref[...])
        s_ref[...] = z


def in_proj_section(h, hs, w_in, layer, col0, gain, *, want_f32, want_bf16, regroup=False,
                    seq=None, stack=None, depth=1):
    m, k = h.shape
    ms = hs.shape[0]
    n_sec = D_ATT
    tn, tm = TN_IN, TM_IN
    ni, nj = m // tm, n_sec // tn
    n_tiles = ni * nj
    assert ni >= 2
    jb0 = col0 // tn
    norm = gain is not None
    stacked = want_f32 and stack != "unstacked"
    has_alias = stacked and layer > 0

    def cur(s):
        t = jnp.minimum(s, n_tiles - 1)
        return t // ni, t % ni
    def prev(s):
        t = jnp.maximum(s - 1, 0)
        return t // ni, t % ni

    in_specs = [pl.BlockSpec((tm, k), lambda s: (cur(s)[1], 0)),
                pl.BlockSpec((ms, k), lambda s: (0, 0)),
                pl.BlockSpec((None, k, tn), lambda s: (layer, 0, jb0 + cur(s)[0]),
                             pipeline_mode=pl.Buffered(1))]
    args = [h, hs, w_in]
    if norm:
        in_specs.append(pl.BlockSpec((1, tn), lambda s: (0, prev(s)[0])))
        args.append(gain.reshape(1, n_sec))
    if has_alias:
        in_specs.append(pl.BlockSpec(memory_space=pl.ANY))
        args.append(stack)

    out_shape = [jax.ShapeDtypeStruct((ms, n_sec), F32)]
    out_specs = [pl.BlockSpec((ms, tn), lambda s: (0, cur(s)[0]))]
    if want_f32:
        row0 = layer * ni if stacked else 0
        out_shape.append(jax.ShapeDtypeStruct(((depth if stacked else 1) * m, n_sec), F32))
        out_specs.append(pl.BlockSpec((tm, tn), lambda s: (row0 + prev(s)[1], prev(s)[0])))
    if want_bf16:
        out_shape.append(jax.ShapeDtypeStruct((m, n_sec), BF16))
        out_specs.append(pl.BlockSpec((tm, tn), lambda s: (prev(s)[1], prev(s)[0])))
    if regroup:
        b = m // seq
        tpb = seq // tm
        grouped = lambda s: (prev(s)[1] // tpb, 0, prev(s)[1] % tpb, prev(s)[0])
        out_shape.append(jax.ShapeDtypeStruct((b, 4, seq // 4, n_sec), BF16))
        out_specs.append(pl.BlockSpec((None, 4, tm // 4, tn), grouped))
        out_shape.append(jax.ShapeDtypeStruct((b, 16, seq // 16, n_sec), BF16))
        out_specs.append(pl.BlockSpec((None, 16, tm // 16, tn), grouped))
    scratch = [pltpu.VMEM((k, tn), BF16)] + [pltpu.VMEM((tn // LANES, tm, LANES), F32)] * 2
    if regroup:
        scratch.append(pltpu.VMEM((tn // LANES, 4, tm // 4, LANES), F32))

    kern = functools.partial(_inproj_kernel, norm=norm, want_f32=want_f32, want_bf16=want_bf16,
                             regroup=regroup, has_alias=has_alias, ni=ni, n_tiles=n_tiles)
    return pl.pallas_call(
        kern, out_shape=out_shape, grid=(n_tiles + 1,), in_specs=in_specs, out_specs=out_specs,
        scratch_shapes=scratch,
        input_output_aliases=({len(args) - 1: 1} if has_alias else {}),
        compiler_params=_params(1), name="in_proj",
    )(*args)


def _attn_tiles(qkv, dist, slopes, prev):
    scores = [lax.dot_general(q, k, (((1,), (1,)), ((), ())), preferred_element_type=F32)
              for q, k, _ in qkv]
    probs = []
    for hh, s in enumerate(scores):
        s = s * SCALE - slopes[hh] * dist
        m_cur = jnp.max(s, axis=-1, keepdims=True)
        if prev is None:
            probs.append((jnp.exp(s - m_cur).astype(BF16), jnp.broadcast_to(m_cur, (QB, LANES)), None))
        else:
            m_prev = prev[hh][1]
            m_new = jnp.maximum(m_prev, m_cur)
            p = jnp.exp(s - jnp.concatenate([m_new] * (s.shape[1] // LANES), axis=1))
            probs.append((p.astype(BF16), m_new, jnp.exp(m_prev - m_new)))
    new = []
    for hh, (p, m_new, alpha) in enumerate(probs):
        v = qkv[hh][2]
        v_ones = jnp.concatenate([v, jnp.ones(v.shape, BF16)], axis=1)
        pv = jnp.dot(p, v_ones, preferred_element_type=F32)
        if prev is None:
            new.append((pv[:, :LANES], m_new, pv[:, LANES:]))
        else:
            new.append((alpha * prev[hh][0] + pv[:, :LANES], m_new, alpha * prev[hh][2] + pv[:, LANES:]))
    return new


def _window_dist(d, dil):
    return jnp.where((d >= 0) & (d <= 128), (d * dil).astype(F32), MASKED)


def _attn_kernel(sl_ref, q4, k4, v4, q16, k16, v16, o_ref, acc, m_s, l_s, dist_s, nat_s, *, seq):
    g = pl.program_id(1)
    s4 = seq // 4
    state = (acc, m_s, l_s)

    o_i = lax.broadcasted_iota(jnp.int32, (QB, 2 * QB), 0)
    c_i = lax.broadcasted_iota(jnp.int32, (QB, 2 * QB), 1)
    d1 = 4 * ((o_i & 31) - (c_i & 63)) + ((o_i >> 5) - (c_i >> 6))
    dist_s[0] = _window_dist(d1, 1)
    dist_s[1] = _window_dist(d1 + 128, 1)
    dist_s[2] = _window_dist(o_i - c_i, 4)
    dist_s[3] = _window_dist(o_i - c_i + 128, 4)
    dist_s[4] = _window_dist(o_i - c_i, 16)

    heads = range(HEADS_PER_STEP)

    def load_state(rows):
        return [tuple(ref[hh, rows, :] for ref in state) for hh in heads]

    def store_state(rows_list, new):
        for hh in heads:
            for ref, val in zip(state, new[hh]):
                n = val.shape[0] // len(rows_list)
                for idx, rows in enumerate(rows_list):
                    ref[hh, rows, :] = val[idx * n:(idx + 1) * n, :]

    def head_cols(hh):
        return slice(hh * HEAD_DIM, (hh + 1) * HEAD_DIM)

    slopes = [sl_ref[g * HEADS_PER_STEP + hh] for hh in heads]

    def body1(qb, carry):
        t0 = pl.multiple_of(qb * 32, 32)
        kt0 = pl.multiple_of(jnp.maximum(qb - 1, 0) * 32, 32)
        dist = dist_s[jnp.minimum(qb, 1)]
        def slabs(ref, start, n, hh):
            return jnp.concatenate([ref[r, pl.ds(start, n), head_cols(hh)] for r in range(4)], axis=0)
        qkv = [(slabs(q4, t0, 32, hh), slabs(k4, kt0, 64, hh), slabs(v4, kt0, 64, hh)) for hh in heads]
        new = _attn_tiles(qkv, dist, slopes, None)
        store_state([pl.ds(r * s4 + t0, 32) for r in range(4)], new)
        return carry
    lax.fori_loop(0, seq // QB, body1, 0, unroll=2)

    n_qb4 = s4 // QB
    assert n_qb4 == 4
    def body4(idx, carry):
        r = idx >> 2
        qb = idx & 3
        q0 = pl.multiple_of(qb * QB, QB)
        k0 = pl.multiple_of(jnp.maximum(qb - 1, 0) * QB, QB)
        dist = dist_s[2 + jnp.minimum(qb, 1)]
        rows = pl.ds(r * s4 + q0, QB)
        qkv = [(q4[r, pl.ds(q0, QB), head_cols(hh)], k4[r, pl.ds(k0, 2 * QB), head_cols(hh)],
                v4[r, pl.ds(k0, 2 * QB), head_cols(hh)]) for hh in heads]
        new = _attn_tiles(qkv, dist, slopes, load_state(rows))
        store_state([rows], new)
        return carry
    lax.fori_loop(0, 4 * n_qb4, body4, 0, unroll=2)

    def body16(r16, carry):
        dist = dist_s[4][:, :QB]
        rows = pl.ds((r16 & 3) * s4 + (r16 >> 2), QB, stride=4)
        qkv = [(q16[r16, :, head_cols(hh)], k16[r16, :, head_cols(hh)], v16[r16, :, head_cols(hh)])
               for hh in heads]
        new = _attn_tiles(qkv, dist, slopes, load_state(rows))
        store_state([rows], new)
        return carry
    lax.fori_loop(0, 16, body16, 0, unroll=2)

    tn = nat_s.shape[1] // 4
    def body_out(c, carry):
        t0 = pl.multiple_of(c * tn, tn)
        for hh in range(HEADS_PER_STEP):
            for r in range(4):
                rows = pl.ds(r * s4 + t0, tn)
                nat_s[hh, pl.ds(r, tn, stride=4), :] = acc[hh, rows, :] / l_s[hh, rows, :]
            o_ref[pl.ds(pl.multiple_of(4 * t0, 4 * tn), 4 * tn), head_cols(hh)] = (
                nat_s[hh].astype(o_ref.dtype))
        return carry
    lax.fori_loop(0, s4 // tn, body_out, 0)


def prompt_attention(slopes, qs, ks, vs, batch, seq):
    assert seq // 16 == QB and seq % (4 * QB) == 0
    tw = HEADS_PER_STEP * HEAD_DIM
    r4 = pl.BlockSpec((None, 4, seq // 4, tw), lambda b, g: (b, 0, 0, g))
    r16 = pl.BlockSpec((None, 16, seq // 16, tw), lambda b, g: (b, 0, 0, g))
    state = pltpu.VMEM((HEADS_PER_STEP, seq, LANES), F32)
    return pl.pallas_call(
        functools.partial(_attn_kernel, seq=seq),
        out_shape=jax.ShapeDtypeStruct((batch, seq, D_ATT), BF16),
        grid=(batch, N_HEADS // HEADS_PER_STEP),
        in_specs=[pl.BlockSpec(memory_space=pltpu.SMEM), r4, r4, r4, r16, r16, r16],
        out_specs=pl.BlockSpec((None, seq, tw), lambda b, g: (b, 0, g)),
        scratch_shapes=[state, state, state,
                        pltpu.VMEM((5, QB, 2 * QB), F32),
                        pltpu.VMEM((HEADS_PER_STEP, 256, LANES), F32)],
        compiler_params=_params(2), name="prompt_attention",
    )(slopes, qs[0], ks[0], vs[0], qs[1], ks[1], vs[1])


def _sgu_kernel(g_ref, u_ref, w_ref, bt_ref, o_ref, wt_ref):
    @pl.when(pl.program_id(0) == 0)
    def _():
        r_i = lax.broadcasted_iota(jnp.int32, (CHUNK, CHUNK), 0)
        c_i = lax.broadcasted_iota(jnp.int32, (CHUNK, CHUNK), 1)
        for gi in range(N_GROUPS):
            wt_ref[gi] = jnp.where(r_i >= c_i, w_ref[gi], 0.0).astype(BF16)

    for gi in range(N_GROUPS):
        cols = slice(gi * HEAD_DIM, (gi + 1) * HEAD_DIM)
        gate = jnp.dot(wt_ref[gi], g_ref[:, cols], preferred_element_type=F32) + bt_ref[:, gi:gi + 1]
        o_ref[:, cols] = (u_ref[:, cols] * gate).astype(BF16)


def prompt_sgu(g, u, w_spatial, b_t, layer):
    m = g.shape[0]
    row = pl.BlockSpec((CHUNK, D_SGU), lambda c: (c, 0))
    return pl.pallas_call(
        _sgu_kernel,
        out_shape=jax.ShapeDtypeStruct((m, D_SGU), BF16),
        grid=(m // CHUNK,),
        in_specs=[row, row,
                  pl.BlockSpec((None, N_GROUPS, CHUNK, CHUNK), lambda c: (layer, 0, 0, 0)),
                  pl.BlockSpec((None, CHUNK, N_GROUPS), lambda c: (layer, 0, 0))],
        out_specs=row,
        scratch_shapes=[pltpu.VMEM((N_GROUPS, CHUNK, CHUNK), BF16)],
        compiler_params=_params(1), name="prompt_sgu",
    )(g, u, w_spatial, b_t)


def _outproj_kernel(a_ref, ug_ref, as_ref, ugs_ref, w_ref, x_ref, xs_ref, o_ref, os_ref, wb_ref):
    @pl.when(_is_first_row_tile())
    def _():
        _cast_weight(w_ref, wb_ref)

    def project(a, ug):
        z = jnp.dot(a, wb_ref[:D_ATT, :], preferred_element_type=F32)
        return z + jnp.dot(ug, wb_ref[D_ATT:, :], preferred_element_type=F32)

    o_ref[...] = x_ref[...] + project(a_ref[...], ug_ref[...])

    @pl.when(_is_last_row_tile())
    def _():
        os_ref[...] = xs_ref[...] + project(as_ref[...], ugs_ref[...])


def out_proj(att, ug, att_s, ug_s, w_out, x, xs, layer):
    m, ms = x.shape[0], xs.shape[0]
    tn, tm = TN_IN, TM_IN
    lhs = pl.BlockSpec((tm, D_ATT), lambda j, i: (i, 0))
    lhs_s = pl.BlockSpec((ms, D_ATT), lambda j, i: (0, 0))
    res = pl.BlockSpec((tm, tn), lambda j, i: (i, j))
    res_s = pl.BlockSpec((ms, tn), lambda j, i: (0, j))
    return pl.pallas_call(
        _outproj_kernel,
        out_shape=[jax.ShapeDtypeStruct((m, D_MODEL), F32), jax.ShapeDtypeStruct((ms, D_MODEL), F32)],
        grid=(D_MODEL // tn, m // tm),
        in_specs=[lhs, lhs, lhs_s, lhs_s,
                  pl.BlockSpec((None, D_MODEL, tn), lambda j, i: (layer, 0, j),
                               pipeline_mode=pl.Buffered(1)),
                  res, res_s],
        out_specs=[res, res_s],
        scratch_shapes=[pltpu.VMEM((D_MODEL, tn), BF16)],
        compiler_params=_params(2), name="out_proj",
    )(att, ug, att_s, ug_s, w_out, x, xs)


def _ffn1_kernel(h_ref, hs_ref, wg_ref, wu_ref, wd_ref, a_ref, as_ref, wdb_ref, wgb_ref, wub_ref):
    wdb_ref[...] = wd_ref[...].astype(BF16)

    @pl.when(_is_first_row_tile())
    def _():
        _cast_weight(wg_ref, wgb_ref)
        _cast_weight(wu_ref, wub_ref)

    def gated(h):
        g = jnp.dot(h, wgb_ref[...], preferred_element_type=F32)
        u = jnp.dot(h, wub_ref[...], preferred_element_type=F32)
        return (g / (1.0 + jnp.exp(-g)) * u).astype(BF16)

    a_ref[...] = gated(h_ref[...])

    @pl.when(_is_last_row_tile())
    def _():
        as_ref[...] = gated(hs_ref[...])


def ffn_gate_up(h, hs, w_gate, w_up, w_down, layer):
    m, k = h.shape
    ms = hs.shape[0]
    d_ff = w_gate.shape[2]
    tn, tm = TN_FFN, TM_PROJ
    nj, ni = d_ff // tn, m // tm
    rb = d_ff // (nj * ni)
    assert rb * nj * ni == d_ff and rb % 16 == 0
    wspec = pl.BlockSpec((None, k, tn), lambda j, i: (layer, 0, j))
    return pl.pallas_call(
        _ffn1_kernel,
        out_shape=[jax.ShapeDtypeStruct((m, d_ff), BF16), jax.ShapeDtypeStruct((ms, d_ff), BF16),
                   jax.ShapeDtypeStruct((d_ff, D_MODEL), BF16)],
        grid=(nj, ni),
        in_specs=[pl.BlockSpec((tm, k), lambda j, i: (i, 0)),
                  pl.BlockSpec((ms, k), lambda j, i: (0, 0)), wspec, wspec,
                  pl.BlockSpec((None, rb, D_MODEL), lambda j, i: (layer, j * ni + i, 0))],
        out_specs=[pl.BlockSpec((tm, tn), lambda j, i: (i, j)),
                   pl.BlockSpec((ms, tn), lambda j, i: (0, j)),
                   pl.BlockSpec((rb, D_MODEL), lambda j, i: (j * ni + i, 0))],
        scratch_shapes=[pltpu.VMEM((k, tn), BF16)] * 2,
        compiler_params=_params(2), name="ffn_gate_up",
    )(h, hs, w_gate, w_up, w_down)


def _ffn2_kernel(a_ref, as_ref, w_ref, x_ref, xs_ref, o_ref, os_ref):
    o_ref[...] = x_ref[...] + jnp.dot(a_ref[...], w_ref[...], preferred_element_type=F32)

    @pl.when(_is_last_row_tile())
    def _():
        os_ref[...] = xs_ref[...] + jnp.dot(as_ref[...], w_ref[...], preferred_element_type=F32)


def ffn_down(a, a_s, w_down_bf16, x, xs):
    m, k = a.shape
    ms = a_s.shape[0]
    tn, tm = TN_DOWN, TM_DOWN
    res = pl.BlockSpec((tm, tn), lambda j, i: (i, j))
    res_s = pl.BlockSpec((ms, tn), lambda j, i: (0, j))
    return pl.pallas_call(
        _ffn2_kernel,
        out_shape=[jax.ShapeDtypeStruct((m, D_MODEL), F32), jax.ShapeDtypeStruct((ms, D_MODEL), F32)],
        grid=(D_MODEL // tn, m // tm),
        in_specs=[pl.BlockSpec((tm, k), lambda j, i: (i, 0)),
                  pl.BlockSpec((ms, k), lambda j, i: (0, 0)),
                  pl.BlockSpec((k, tn), lambda j, i: (0, j)),
                  res, res_s],
        out_specs=[res, res_s],
        compiler_params=_params(2), name="ffn_down",
    )(a, a_s, w_down_bf16, x, xs)


SQ = 16


def _multiplicity(d):
    nonneg = d >= 0
    n = (nonneg & (d <= 128)).astype(F32)
    n = n + (nonneg & (d <= 512) & ((d & 3) == 0)).astype(F32)
    n = n + (nonneg & (d <= 2048) & ((d & 15) == 0)).astype(F32)
    return n


def _sattn_kernel(sl_ref, q_ref, kn_ref, vn_ref, kc_ref, vc_ref, o_ref, acc, m_s, l_s, xk_ref, xv_ref,
                  *, tc, wb, ts):
    c = pl.program_id(1)
    last = pl.num_programs(1) - 1

    @pl.when(c == 0)
    def _():
        m_s[...] = jnp.full(m_s.shape, -MASKED, F32)
        l_s[...] = jnp.zeros(l_s.shape, F32)
        acc[...] = jnp.zeros(acc.shape, F32)

    def pad_rows(x, n):
        return jnp.concatenate([x, jnp.zeros((n - x.shape[0], x.shape[1]), F32)], axis=0)

    def head_cols(h):
        return slice(h * HEAD_DIM, (h + 1) * HEAD_DIM)

    def query(h):
        return pad_rows(q_ref[:, head_cols(h)], SQ).astype(BF16)

    def update(key, value, d):
        mult = _multiplicity(d)
        dist = jnp.where(mult > 0.0, d.astype(F32), MASKED)
        n_rep = d.shape[1] // LANES
        scores = [lax.dot_general(query(h), key(h), (((1,), (1,)), ((), ())),
                                  preferred_element_type=F32) for h in range(N_HEADS)]
        staged = []
        for h, s in enumerate(scores):
            s = s * SCALE - sl_ref[h] * dist
            m_prev = m_s[h]
            m_new = jnp.maximum(m_prev, jnp.max(s, axis=-1, keepdims=True))
            p = mult * jnp.exp(s - jnp.concatenate([m_new] * n_rep, axis=1))
            staged.append((p, m_new, jnp.exp(m_prev - m_new)))
        for h, (p, m_new, alpha) in enumerate(staged):
            l_s[h] = alpha * l_s[h] + jnp.sum(p, axis=-1, keepdims=True)
            acc[h] = alpha * acc[h] + jnp.dot(p.astype(BF16), value(h), preferred_element_type=F32)
            m_s[h] = m_new

    t_c = lax.broadcasted_iota(jnp.int32, (SQ, tc), 0)
    c_c = lax.broadcasted_iota(jnp.int32, (SQ, tc), 1)
    d_cache = (wb + t_c) - (c * tc + c_c)
    for r in range(4):
        xk_ref[r] = kc_ref[pl.ds(r, 4 * tc, stride=4), :]
        xv_ref[r] = vc_ref[pl.ds(r, 4 * tc, stride=4), :]
    update(lambda h: xk_ref[h % 4, pl.ds(h // 4, tc, stride=4), :].astype(BF16),
           lambda h: xv_ref[h % 4, pl.ds(h // 4, tc, stride=4), :].astype(BF16), d_cache)

    @pl.when(c == last)
    def _():
        t_n = lax.broadcasted_iota(jnp.int32, (SQ, LANES), 0)
        c_n = lax.broadcasted_iota(jnp.int32, (SQ, LANES), 1)
        d_new = jnp.where(c_n < ts, t_n - c_n, -1)
        update(lambda h: pad_rows(kn_ref[:, head_cols(h)], LANES).astype(BF16),
               lambda h: pad_rows(vn_ref[:, head_cols(h)], LANES).astype(BF16), d_new)
        for h in range(N_HEADS):
            o_ref[:, head_cols(h)] = (acc[h] / l_s[h])[:ts, :]


def sample_attention(slopes, q, k_new, v_new, cache_k, cache_v, layer):
    bs, ts, _ = q.shape
    wb = cache_k.shape[2] // N_HEADS
    tc = 512
    assert wb % tc == 0 and ts <= SQ and wb == WINDOW_MAX
    new = pl.BlockSpec((None, ts, D_ATT), lambda b, c: (b, 0, 0))
    cache = pl.BlockSpec((None, None, tc * N_HEADS, HEAD_DIM), lambda b, c: (layer, b, c, 0))
    return pl.pallas_call(
        functools.partial(_sattn_kernel, tc=tc, wb=wb, ts=ts),
        out_shape=jax.ShapeDtypeStruct((bs, ts, D_ATT), F32),
        grid=(bs, wb // tc),
        in_specs=[pl.BlockSpec(memory_space=pltpu.SMEM), new, new, new, cache, cache],
        out_specs=new,
        scratch_shapes=[pltpu.VMEM((N_HEADS, SQ, LANES), F32)] * 3
                       + [pltpu.VMEM((4, 4 * tc, HEAD_DIM), F32)] * 2,
        compiler_params=_params(2), name="sample_attention",
    )(slopes, q, k_new, v_new, cache_k, cache_v)


def _ssgu_kernel(att_ref, g_ref, u_ref, wt_ref, b_ref, att_o, ug_o, *, ts):
    att_o[...] = att_ref[...].astype(BF16)
    n = att_ref.shape[0]
    r_i = lax.broadcasted_iota(jnp.int32, (n, n), 0)
    c_i = lax.broadcasted_iota(jnp.int32, (n, n), 1)
    keep = ((r_i & -ts) == (c_i & -ts)) & ((c_i & (ts - 1)) <= (r_i & (ts - 1)))
    for gi in range(N_GROUPS):
        cols = slice(gi * HEAD_DIM, (gi + 1) * HEAD_DIM)
        w = jnp.where(keep, wt_ref[gi], 0.0).astype(BF16)
        gate = jnp.dot(w, g_ref[:, cols].astype(BF16), preferred_element_type=F32) + b_ref[:, gi:gi + 1]
        ug_o[:, cols] = (u_ref[:, cols] * gate).astype(BF16)


def sample_sgu(att, g, u, w_tiled, b_rows, ts):
    n = att.shape[0]
    return pl.pallas_call(
        functools.partial(_ssgu_kernel, ts=ts),
        out_shape=[jax.ShapeDtypeStruct((n, D_ATT), BF16), jax.ShapeDtypeStruct((n, D_SGU), BF16)],
        compiler_params=pltpu.CompilerParams(vmem_limit_bytes=VMEM_LIMIT),
        name="sample_sgu",
    )(att, g, u, w_tiled, b_rows)


def kernel(x_prompt, x_sample, cache_k_win, cache_v_win, norm1, w_in, q_gain, k_gain, sgu_gain,
           w_spatial, b_spatial, w_out, norm2, w_gate, w_up, w_down):
    bp, sp, _ = x_prompt.shape
    bs, ts, _ = x_sample.shape
    assert ts & (ts - 1) == 0
    depth = w_in.shape[0]
    wb = cache_k_win.shape[2]
    assert sp == WINDOW_MAX and ts <= 8 and wb == WINDOW_MAX
    mp, ms = bp * sp, bs * ts

    xp = x_prompt.reshape(mp, D_MODEL)
    xs = x_sample.reshape(ms, D_MODEL)
    cache_k = cache_k_win.reshape(depth, bs, wb * N_HEADS, HEAD_DIM)
    cache_v = cache_v_win.reshape(depth, bs, wb * N_HEADS, HEAD_DIM)
    slopes = jnp.exp2(-8.0 * jnp.arange(1, N_HEADS + 1, dtype=F32) / N_HEADS)
    b_t = jnp.swapaxes(b_spatial, 1, 2)

    k_stack = v_stack = None
    ks_rows, vs_rows, gs_rows = [], [], []
    for l in range(depth):
        qg = jnp.tile(q_gain[l], N_HEADS)
        kg = jnp.tile(k_gain[l], N_HEADS)
        sg = sgu_gain[l].reshape(D_SGU)

        hp = rms_rows(xp, norm1[l])
        hs = rms_rows(xs, norm1[l])
        sec = functools.partial(in_proj_section, hp, hs, w_in, l, seq=sp, depth=depth)
        q_s, *q_p = sec(0 * D_ATT, qg, want_f32=False, want_bf16=False, regroup=True)
        k_s, k_stack, *k_p = sec(1 * D_ATT, kg, want_f32=True, want_bf16=False, regroup=True,
                                 stack=k_stack)
        v_s, v_stack, *v_p = sec(2 * D_ATT, None, want_f32=True, want_bf16=False, regroup=True,
                                 stack=v_stack)
        u_s, u_p = sec(3 * D_ATT, None, want_f32=True, want_bf16=False, stack="unstacked")
        g_s, g_p = sec(3 * D_ATT + D_SGU, sg, want_f32=False, want_bf16=True)

        att_p = prompt_attention(slopes, q_p, k_p, v_p, bp, sp).reshape(mp, D_ATT)
        ug_p = prompt_sgu(g_p, u_p, w_spatial, b_t, l)

        r3 = lambda a: a.reshape(bs, ts, D_ATT)
        att_s = sample_attention(slopes, r3(q_s), r3(k_s), r3(v_s), cache_k, cache_v, l)
        w_tiled = jnp.tile(w_spatial[l][:, :ts, :ts], (1, bs, bs))
        b_rows = jnp.tile(b_spatial[l][:, :ts].T, (bs, 1))
        att_sb, ug_s = sample_sgu(att_s.reshape(ms, D_ATT), g_s, u_s, w_tiled, b_rows, ts)

        xp, xs = out_proj(att_p, ug_p, att_sb, ug_s, w_out, xp, xs, l)
        a_p, a_s, w_down_bf16 = ffn_gate_up(rms_rows(xp, norm2[l]), rms_rows(xs, norm2[l]),
                                            w_gate, w_up, w_down, l)
        xp, xs = ffn_down(a_p, a_s, w_down_bf16, xp, xs)

        ks_rows.append(k_s)
        vs_rows.append(v_s)
        gs_rows.append(g_s)

    heads = lambda a, b, t: a.reshape(depth, b, t, N_HEADS, HEAD_DIM)
    return (xp.reshape(bp, sp, D_MODEL), xs.reshape(bs, ts, D_MODEL),
            heads(k_stack, bp, sp), heads(v_stack, bp, sp),
            heads(jnp.stack(ks_rows), bs, ts), heads(jnp.stack(vs_rows), bs, ts),
            heads(jnp.stack(gs_rows), bs, ts))
```

```python
import functools

import jax
import jax.numpy as jnp
from jax import lax
from jax.experimental import pallas as pl
from jax.experimental.pallas import tpu as pltpu

F32 = jnp.float32
BF16 = jnp.bfloat16

D_MODEL = 4096
HEAD_DIM = 128
N_HEADS = 16
D_ATT = N_HEADS * HEAD_DIM
D_SGU = D_MODEL - D_ATT
N_GROUPS = D_SGU // HEAD_DIM
CHUNK = 128
WINDOW_MAX = 2048
RMS_EPS = 1e-6
SCALE = HEAD_DIM ** -0.5
MASKED = 1e30
LANES = 128
VMEM_LIMIT = 60 * 1024 * 1024

HEADS_PER_STEP = 4
QB = 128

TM_PROJ = 1024
TN_PROJ = 512
TM_IN = 512
TN_IN = 1024
TN_FFN = 256
TM_FFN = 2048
TM_DOWN = 512
TN_DOWN = 512


def _params(n_axes):
    return pltpu.CompilerParams(dimension_semantics=("arbitrary",) * n_axes,
                                vmem_limit_bytes=VMEM_LIMIT)


def _cast_weight(w_ref, wb_ref):
    k = w_ref.shape[0]
    ck = 256
    def body(c, carry):
        r = pl.multiple_of(c * ck, ck)
        wb_ref[pl.ds(r, ck), :] = w_ref[pl.ds(r, ck), :].astype(BF16)
        return carry
    lax.fori_loop(0, k // ck, body, 0)


def _weight_tile_copy(w_hbm, layer, col, stage_ref, sem):
    col = pl.multiple_of(col, LANES)
    return pltpu.make_async_copy(w_hbm.at[layer, :, pl.ds(col, stage_ref.shape[1])], stage_ref, sem)


def _take_weight_tile(w_hbm, layer, col_of, j, nj, stage_ref, sem, wb_ref):
    @pl.when(j == 0)
    def _():
        _weight_tile_copy(w_hbm, layer, col_of(0), stage_ref, sem).start()
    _weight_tile_copy(w_hbm, layer, col_of(j), stage_ref, sem).wait()
    _cast_weight(stage_ref, wb_ref)

    @pl.when(j + 1 < nj)
    def _():
        _weight_tile_copy(w_hbm, layer, col_of(j + 1), stage_ref, sem).start()


def _is_first_row_tile():
    return pl.program_id(1) == 0


def _is_last_row_tile():
    return pl.program_id(1) == pl.num_programs(1) - 1


def _rms_rows_kernel(x_ref, g_ref, o_ref):
    x = x_ref[...]
    ms = jnp.mean(x * x, axis=-1, keepdims=True)
    o_ref[...] = (x * lax.rsqrt(ms + RMS_EPS) * g_ref[...]).astype(o_ref.dtype)


def rms_rows(x, gain):
    m, d = x.shape
    tm = min(m, 256)
    return pl.pallas_call(
        _rms_rows_kernel,
        out_shape=jax.ShapeDtypeStruct((m, d), BF16),
        grid=(m // tm,),
        in_specs=[pl.BlockSpec((tm, d), lambda i: (i, 0)),
                  pl.BlockSpec((1, d), lambda i: (0, 0))],
        out_specs=pl.BlockSpec((tm, d), lambda i: (i, 0)),
        compiler_params=_params(1),
        name="rms_rows",
    )(x, gain.reshape(1, d))


def _head_norm(z, gain):
    parts = []
    for hh in range(z.shape[1] // HEAD_DIM):
        cols = slice(hh * HEAD_DIM, (hh + 1) * HEAD_DIM)
        zh = z[:, cols]
        ms = jnp.mean(zh * zh, axis=-1, keepdims=True)
        parts.append(zh * lax.rsqrt(ms + RMS_EPS) * gain[:, cols])
    return jnp.concatenate(parts, axis=1)


def _inproj_kernel(*refs, norm, want_f32, want_bf16, regroup, has_alias, ni, n_tiles, layer, col0):
    it = iter(refs)
    h_ref = next(it)
    hs_ref = next(it)
    w_ref = next(it)
    gain_ref = next(it) if norm else None
    if has_alias:
        next(it)
    s_ref = next(it)
    f32_ref = next(it) if want_f32 else None
    bf_ref = next(it) if want_bf16 else None
    o4_ref = next(it) if regroup else None
    o16_ref = next(it) if regroup else None
    wb_ref = next(it)
    stage_ref = next(it)
    sem = next(it)
    z_refs = (next(it), next(it))
    z4_ref = next(it) if regroup else None

    s = pl.program_id(0)
    tm = h_ref.shape[0]
    n_slabs = wb_ref.shape[1] // LANES
    active = s < n_tiles

    def finish(z_ref):
        for c in range(n_slabs):
            cols = slice(c * LANES, (c + 1) * LANES)
            z = z_ref[c]
            if norm:
                z = _head_norm(z, gain_ref[:, cols])
            if want_f32:
                f32_ref[:, cols] = z
            if want_bf16:
                bf_ref[:, cols] = z.astype(BF16)
            if regroup:
                if norm:
                    z_ref[c] = z
                for r in range(4):
                    z4 = z_ref[c, pl.ds(r, tm // 4, stride=4), :]
                    o4_ref[r, :, cols] = z4.astype(BF16)
                    z4_ref[c, r] = z4
                for r in range(4):
                    for c4 in range(4):
                        o16_ref[r + 4 * c4, :, cols] = (
                            z4_ref[c, r, pl.ds(c4, tm // 16, stride=4), :].astype(BF16))

    def multiply(z_ref):
        wide = 2 * LANES
        for c in range(wb_ref.shape[1] // wide):
            z = jnp.dot(h_ref[...], wb_ref[:, c * wide:(c + 1) * wide], preferred_element_type=F32)
            z_ref[2 * c] = z[:, :LANES]
            z_ref[2 * c + 1] = z[:, LANES:]

    @pl.when(s == 0)
    def _():
        z_refs[1][...] = jnp.zeros(z_refs[1].shape, F32)

    @pl.when(active & (s % ni == 0))
    def _():
        tn = wb_ref.shape[1]
        _take_weight_tile(w_ref, layer, lambda j: col0 + j * tn, s // ni, n_tiles // ni,
                          stage_ref, sem, wb_ref)

    for par in range(2):
        @pl.when(active & (s % 2 == par))
        def _():
            finish(z_refs[1 - par])
            multiply(z_refs[par])

    @pl.when(s == n_tiles)
    def _():
        finish(z_refs[(n_tiles - 1) % 2])

    @pl.when(active & (s % ni == ni - 1))
    def _():
        z = jnp.dot(hs_ref[...], wb_ref[...], preferred_element_type=F32)
        if norm:
            z = _head_norm(z, gain_ref[...])
        s_ref[...] = z


def in_proj_section(h, hs, w_in, layer, col0, gain, *, want_f32, want_bf16, regroup=False,
                    seq=None, stack=None, depth=1):
    m, k = h.shape
    ms = hs.shape[0]
    n_sec = D_ATT
    tn, tm = TN_IN, TM_IN
    ni, nj = m // tm, n_sec // tn
    n_tiles = ni * nj
    assert ni >= 2
    norm = gain is not None
    stacked = want_f32 and stack != "unstacked"
    has_alias = stacked and layer > 0

    def cur(s):
        t = jnp.minimum(s, n_tiles - 1)
        return t // ni, t % ni
    def prev(s):
        t = jnp.maximum(s - 1, 0)
        return t // ni, t % ni

    in_specs = [pl.BlockSpec((tm, k), lambda s: (cur(s)[1], 0)),
                pl.BlockSpec((ms, k), lambda s: (0, 0)),
                pl.BlockSpec(memory_space=pl.ANY)]
    args = [h, hs, w_in]
    if norm:
        in_specs.append(pl.BlockSpec((1, tn), lambda s: (0, prev(s)[0])))
        args.append(gain.reshape(1, n_sec))
    if has_alias:
        in_specs.append(pl.BlockSpec(memory_space=pl.ANY))
        args.append(stack)

    out_shape = [jax.ShapeDtypeStruct((ms, n_sec), F32)]
    out_specs = [pl.BlockSpec((ms, tn), lambda s: (0, cur(s)[0]))]
    if want_f32:
        row0 = layer * ni if stacked else 0
        out_shape.append(jax.ShapeDtypeStruct(((depth if stacked else 1) * m, n_sec), F32))
        out_specs.append(pl.BlockSpec((tm, tn), lambda s: (row0 + prev(s)[1], prev(s)[0])))
    if want_bf16:
        out_shape.append(jax.ShapeDtypeStruct((m, n_sec), BF16))
        out_specs.append(pl.BlockSpec((tm, tn), lambda s: (prev(s)[1], prev(s)[0])))
    if regroup:
        b = m // seq
        tpb = seq // tm
        grouped = lambda s: (prev(s)[1] // tpb, 0, prev(s)[1] % tpb, prev(s)[0])
        out_shape.append(jax.ShapeDtypeStruct((b, 4, seq // 4, n_sec), BF16))
        out_specs.append(pl.BlockSpec((None, 4, tm // 4, tn), grouped))
        out_shape.append(jax.ShapeDtypeStruct((b, 16, seq // 16, n_sec), BF16))
        out_specs.append(pl.BlockSpec((None, 16, tm // 16, tn), grouped))
    scratch = ([pltpu.VMEM((k, tn), BF16), pltpu.VMEM((k, tn), F32), pltpu.SemaphoreType.DMA]
               + [pltpu.VMEM((tn // LANES, tm, LANES), F32)] * 2)
    if regroup:
        scratch.append(pltpu.VMEM((tn // LANES, 4, tm // 4, LANES), F32))

    kern = functools.partial(_inproj_kernel, norm=norm, want_f32=want_f32, want_bf16=want_bf16,
                             regroup=regroup, has_alias=has_alias, ni=ni, n_tiles=n_tiles,
                             layer=layer, col0=col0)
    return pl.pallas_call(
        kern, out_shape=out_shape, grid=(n_tiles + 1,), in_specs=in_specs, out_specs=out_specs,
        scratch_shapes=scratch,
        input_output_aliases=({len(args) - 1: 1} if has_alias else {}),
        compiler_params=_params(1), name="in_proj",
    )(*args)


def _attn_tiles(qkv, dist, slopes, prev):
    scores = [lax.dot_general(q, k, (((1,), (1,)), ((), ())), preferred_element_type=F32)
              for q, k, _ in qkv]
    probs = []
    for hh, s in enumerate(scores):
        s = s * SCALE - slopes[hh] * dist
        m_cur = jnp.max(s, axis=-1, keepdims=True)
        if prev is None:
            probs.append((jnp.exp(s - m_cur).astype(BF16), jnp.broadcast_to(m_cur, (QB, LANES)), None))
        else:
            m_prev = prev[hh][1]
            m_new = jnp.maximum(m_prev, m_cur)
            p = jnp.exp(s - jnp.concatenate([m_new] * (s.shape[1] // LANES), axis=1))
            probs.append((p.astype(BF16), m_new, jnp.exp(m_prev - m_new)))
    new = []
    for hh, (p, m_new, alpha) in enumerate(probs):
        v = qkv[hh][2]
        v_ones = jnp.concatenate([v, jnp.ones(v.shape, BF16)], axis=1)
        pv = jnp.dot(p, v_ones, preferred_element_type=F32)
        if prev is None:
            new.append((pv[:, :LANES], m_new, pv[:, LANES:]))
        else:
            new.append((alpha * prev[hh][0] + pv[:, :LANES], m_new, alpha * prev[hh][2] + pv[:, LANES:]))
    return new


def _window_dist(d, dil):
    return jnp.where((d >= 0) & (d <= 128), (d * dil).astype(F32), MASKED)


def _attn_kernel(sl_ref, q4, k4, v4, q16, k16, v16, o_ref, acc, m_s, l_s, dist_s, nat_s, *, seq):
    g = pl.program_id(1)
    s4 = seq // 4
    state = (acc, m_s, l_s)

    o_i = lax.broadcasted_iota(jnp.int32, (QB, 2 * QB), 0)
    c_i = lax.broadcasted_iota(jnp.int32, (QB, 2 * QB), 1)
    d1 = 4 * ((o_i & 31) - (c_i & 63)) + ((o_i >> 5) - (c_i >> 6))
    dist_s[0] = _window_dist(d1, 1)
    dist_s[1] = _window_dist(d1 + 128, 1)
    dist_s[2] = _window_dist(o_i - c_i, 4)
    dist_s[3] = _window_dist(o_i - c_i + 128, 4)
    dist_s[4] = _window_dist(o_i - c_i, 16)

    heads = range(HEADS_PER_STEP)

    def load_state(rows):
        return [tuple(ref[hh, rows, :] for ref in state) for hh in heads]

    def store_state(rows_list, new):
        for hh in heads:
            for ref, val in zip(state, new[hh]):
                n = val.shape[0] // len(rows_list)
                for idx, rows in enumerate(rows_list):
                    ref[hh, rows, :] = val[idx * n:(idx + 1) * n, :]

    def head_cols(hh):
        return slice(hh * HEAD_DIM, (hh + 1) * HEAD_DIM)

    slopes = [sl_ref[g * HEADS_PER_STEP + hh] for hh in heads]

    def body1(qb, carry):
        t0 = pl.multiple_of(qb * 32, 32)
        kt0 = pl.multiple_of(jnp.maximum(qb - 1, 0) * 32, 32)
        dist = dist_s[jnp.minimum(qb, 1)]
        def slabs(ref, start, n, hh):
            return jnp.concatenate([ref[r, pl.ds(start, n), head_cols(hh)] for r in range(4)], axis=0)
        qkv = [(slabs(q4, t0, 32, hh), slabs(k4, kt0, 64, hh), slabs(v4, kt0, 64, hh)) for hh in heads]
        new = _attn_tiles(qkv, dist, slopes, None)
        store_state([pl.ds(r * s4 + t0, 32) for r in range(4)], new)
        return carry
    lax.fori_loop(0, seq // QB, body1, 0, unroll=2)

    n_qb4 = s4 // QB
    assert n_qb4 == 4
    def body4(idx, carry):
        r = idx >> 2
        qb = idx & 3
        q0 = pl.multiple_of(qb * QB, QB)
        k0 = pl.multiple_of(jnp.maximum(qb - 1, 0) * QB, QB)
        dist = dist_s[2 + jnp.minimum(qb, 1)]
        rows = pl.ds(r * s4 + q0, QB)
        qkv = [(q4[r, pl.ds(q0, QB), head_cols(hh)], k4[r, pl.ds(k0, 2 * QB), head_cols(hh)],
                v4[r, pl.ds(k0, 2 * QB), head_cols(hh)]) for hh in heads]
        new = _attn_tiles(qkv, dist, slopes, load_state(rows))
        store_state([rows], new)
        return carry
    lax.fori_loop(0, 4 * n_qb4, body4, 0, unroll=2)

    def body16(r16, carry):
        dist = dist_s[4][:, :QB]
        rows = pl.ds((r16 & 3) * s4 + (r16 >> 2), QB, stride=4)
        qkv = [(q16[r16, :, head_cols(hh)], k16[r16, :, head_cols(hh)], v16[r16, :, head_cols(hh)])
               for hh in heads]
        new = _attn_tiles(qkv, dist, slopes, load_state(rows))
        store_state([rows], new)
        return carry
    lax.fori_loop(0, 16, body16, 0, unroll=2)

    tn = nat_s.shape[1] // 4
    def body_out(c, carry):
        t0 = pl.multiple_of(c * tn, tn)
        for hh in range(HEADS_PER_STEP):
            for r in range(4):
                rows = pl.ds(r * s4 + t0, tn)
                nat_s[hh, pl.ds(r, tn, stride=4), :] = acc[hh, rows, :] / l_s[hh, rows, :]
            o_ref[pl.ds(pl.multiple_of(4 * t0, 4 * tn), 4 * tn), head_cols(hh)] = (
                nat_s[hh].astype(o_ref.dtype))
        return carry
    lax.fori_loop(0, s4 // tn, body_out, 0)


def prompt_attention(slopes, qs, ks, vs, batch, seq):
    assert seq // 16 == QB and seq % (4 * QB) == 0
    tw = HEADS_PER_STEP * HEAD_DIM
    r4 = pl.BlockSpec((None, 4, seq // 4, tw), lambda b, g: (b, 0, 0, g))
    r16 = pl.BlockSpec((None, 16, seq // 16, tw), lambda b, g: (b, 0, 0, g))
    state = pltpu.VMEM((HEADS_PER_STEP, seq, LANES), F32)
    return pl.pallas_call(
        functools.partial(_attn_kernel, seq=seq),
        out_shape=jax.ShapeDtypeStruct((batch, seq, D_ATT), BF16),
        grid=(batch, N_HEADS // HEADS_PER_STEP),
        in_specs=[pl.BlockSpec(memory_space=pltpu.SMEM), r4, r4, r4, r16, r16, r16],
        out_specs=pl.BlockSpec((None, seq, tw), lambda b, g: (b, 0, g)),
        scratch_shapes=[state, state, state,
                        pltpu.VMEM((5, QB, 2 * QB), F32),
                        pltpu.VMEM((HEADS_PER_STEP, 256, LANES), F32)],
        compiler_params=_params(2), name="prompt_attention",
    )(slopes, qs[0], ks[0], vs[0], qs[1], ks[1], vs[1])


def _sgu_kernel(g_ref, u_ref, w_ref, bt_ref, o_ref, wt_ref):
    @pl.when(pl.program_id(0) == 0)
    def _():
        r_i = lax.broadcasted_iota(jnp.int32, (CHUNK, CHUNK), 0)
        c_i = lax.broadcasted_iota(jnp.int32, (CHUNK, CHUNK), 1)
        for gi in range(N_GROUPS):
            wt_ref[gi] = jnp.where(r_i >= c_i, w_ref[gi], 0.0).astype(BF16)

    for gi in range(N_GROUPS):
        cols = slice(gi * HEAD_DIM, (gi + 1) * HEAD_DIM)
        gate = jnp.dot(wt_ref[gi], g_ref[:, cols], preferred_element_type=F32) + bt_ref[:, gi:gi + 1]
        o_ref[:, cols] = (u_ref[:, cols] * gate).astype(BF16)


def prompt_sgu(g, u, w_spatial, b_t, layer):
    m = g.shape[0]
    row = pl.BlockSpec((CHUNK, D_SGU), lambda c: (c, 0))
    return pl.pallas_call(
        _sgu_kernel,
        out_shape=jax.ShapeDtypeStruct((m, D_SGU), BF16),
        grid=(m // CHUNK,),
        in_specs=[row, row,
                  pl.BlockSpec((None, N_GROUPS, CHUNK, CHUNK), lambda c: (layer, 0, 0, 0)),
                  pl.BlockSpec((None, CHUNK, N_GROUPS), lambda c: (layer, 0, 0))],
        out_specs=row,
        scratch_shapes=[pltpu.VMEM((N_GROUPS, CHUNK, CHUNK), BF16)],
        compiler_params=_params(1), name="prompt_sgu",
    )(g, u, w_spatial, b_t)


def _outproj_kernel(a_ref, ug_ref, as_ref, ugs_ref, w_ref, x_ref, xs_ref, o_ref, os_ref,
                    wb_ref, stage_ref, sem, *, layer):
    @pl.when(_is_first_row_tile())
    def _():
        tn = wb_ref.shape[1]
        _take_weight_tile(w_ref, layer, lambda j: j * tn, pl.program_id(0), pl.num_programs(0),
                          stage_ref, sem, wb_ref)

    def project(a, ug):
        z = jnp.dot(a, wb_ref[:D_ATT, :], preferred_element_type=F32)
        return z + jnp.dot(ug, wb_ref[D_ATT:, :], preferred_element_type=F32)

    o_ref[...] = x_ref[...] + project(a_ref[...], ug_ref[...])

    @pl.when(_is_last_row_tile())
    def _():
        os_ref[...] = xs_ref[...] + project(as_ref[...], ugs_ref[...])


def out_proj(att, ug, att_s, ug_s, w_out, x, xs, layer):
    m, ms = x.shape[0], xs.shape[0]
    tn, tm = TN_IN, TM_IN
    lhs = pl.BlockSpec((tm, D_ATT), lambda j, i: (i, 0))
    lhs_s = pl.BlockSpec((ms, D_ATT), lambda j, i: (0, 0))
    res = pl.BlockSpec((tm, tn), lambda j, i: (i, j))
    res_s = pl.BlockSpec((ms, tn), lambda j, i: (0, j))
    return pl.pallas_call(
        functools.partial(_outproj_kernel, layer=layer),
        out_shape=[jax.ShapeDtypeStruct((m, D_MODEL), F32), jax.ShapeDtypeStruct((ms, D_MODEL), F32)],
        grid=(D_MODEL // tn, m // tm),
        in_specs=[lhs, lhs, lhs_s, lhs_s, pl.BlockSpec(memory_space=pl.ANY), res, res_s],
        out_specs=[res, res_s],
        scratch_shapes=[pltpu.VMEM((D_MODEL, tn), BF16), pltpu.VMEM((D_MODEL, tn), F32),
                        pltpu.SemaphoreType.DMA],
        compiler_params=_params(2), name="out_proj",
    )(att, ug, att_s, ug_s, w_out, x, xs)


def _ffn1_kernel(h_ref, hs_ref, wg_ref, wu_ref, wd_ref, a_ref, as_ref, wdb_ref,
                 wgb_ref, wub_ref, gstage_ref, ustage_ref, sems, *, layer):
    wdb_ref[...] = wd_ref[...].astype(BF16)

    @pl.when(_is_first_row_tile())
    def _():
        tn = wgb_ref.shape[1]
        j, nj = pl.program_id(0), pl.num_programs(0)
        _take_weight_tile(wg_ref, layer, lambda jj: jj * tn, j, nj, gstage_ref, sems.at[0], wgb_ref)
        _take_weight_tile(wu_ref, layer, lambda jj: jj * tn, j, nj, ustage_ref, sems.at[1], wub_ref)

    def gated(h):
        g = jnp.dot(h, wgb_ref[...], preferred_element_type=F32)
        u = jnp.dot(h, wub_ref[...], preferred_element_type=F32)
        return (g / (1.0 + jnp.exp(-g)) * u).astype(BF16)

    a_ref[...] = gated(h_ref[...])

    @pl.when(_is_last_row_tile())
    def _():
        as_ref[...] = gated(hs_ref[...])


def ffn_gate_up(h, hs, w_gate, w_up, w_down, layer):
    m, k = h.shape
    ms = hs.shape[0]
    d_ff = w_gate.shape[2]
    tn, tm = TN_FFN, TM_FFN
    nj, ni = d_ff // tn, m // tm
    rb = d_ff // (nj * ni)
    assert rb * nj * ni == d_ff and rb % 16 == 0
    wspec = pl.BlockSpec(memory_space=pl.ANY)
    return pl.pallas_call(
        functools.partial(_ffn1_kernel, layer=layer),
        out_shape=[jax.ShapeDtypeStruct((m, d_ff), BF16), jax.ShapeDtypeStruct((ms, d_ff), BF16),
                   jax.ShapeDtypeStruct((d_ff, D_MODEL), BF16)],
        grid=(nj, ni),
        in_specs=[pl.BlockSpec((tm, k), lambda j, i: (i, 0)),
                  pl.BlockSpec((ms, k), lambda j, i: (0, 0)), wspec, wspec,
                  pl.BlockSpec((None, rb, D_MODEL), lambda j, i: (layer, j * ni + i, 0))],
        out_specs=[pl.BlockSpec((tm, tn), lambda j, i: (i, j)),
                   pl.BlockSpec((ms, tn), lambda j, i: (0, j)),
                   pl.BlockSpec((rb, D_MODEL), lambda j, i: (j * ni + i, 0))],
        scratch_shapes=[pltpu.VMEM((k, tn), BF16)] * 2 + [pltpu.VMEM((k, tn), F32)] * 2
                       + [pltpu.SemaphoreType.DMA((2,))],
        compiler_params=_params(2), name="ffn_gate_up",
    )(h, hs, w_gate, w_up, w_down)


def _ffn2_kernel(a_ref, as_ref, w_ref, x_ref, xs_ref, o_ref, os_ref):
    o_ref[...] = x_ref[...] + jnp.dot(a_ref[...], w_ref[...], preferred_element_type=F32)

    @pl.when(_is_last_row_tile())
    def _():
        os_ref[...] = xs_ref[...] + jnp.dot(as_ref[...], w_ref[...], preferred_element_type=F32)


def ffn_down(a, a_s, w_down_bf16, x, xs):
    m, k = a.shape
    ms = a_s.shape[0]
    tn, tm = TN_DOWN, TM_DOWN
    res = pl.BlockSpec((tm, tn), lambda j, i: (i, j))
    res_s = pl.BlockSpec((ms, tn), lambda j, i: (0, j))
    return pl.pallas_call(
        _ffn2_kernel,
        out_shape=[jax.ShapeDtypeStruct((m, D_MODEL), F32), jax.ShapeDtypeStruct((ms, D_MODEL), F32)],
        grid=(D_MODEL // tn, m // tm),
        in_specs=[pl.BlockSpec((tm, k), lambda j, i: (i, 0)),
                  pl.BlockSpec((ms, k), lambda j, i: (0, 0)),
                  pl.BlockSpec((k, tn), lambda j, i: (0, j)),
                  res, res_s],
        out_specs=[res, res_s],
        compiler_params=_params(2), name="ffn_down",
    )(a, a_s, w_down_bf16, x, xs)


SQ = 16


def _multiplicity(d):
    nonneg = d >= 0
    n = (nonneg & (d <= 128)).astype(F32)
    n = n + (nonneg & (d <= 512) & ((d & 3) == 0)).astype(F32)
    n = n + (nonneg & (d <= 2048) & ((d & 15) == 0)).astype(F32)
    return n


def _sattn_kernel(sl_ref, q_ref, kn_ref, vn_ref, kc_ref, vc_ref, o_ref, acc, m_s, l_s, xk_ref, xv_ref,
                  *, tc, wb, ts):
    c = pl.program_id(1)
    last = pl.num_programs(1) - 1

    @pl.when(c == 0)
    def _():
        m_s[...] = jnp.full(m_s.shape, -MASKED, F32)
        l_s[...] = jnp.zeros(l_s.shape, F32)
        acc[...] = jnp.zeros(acc.shape, F32)

    def pad_rows(x, n):
        return jnp.concatenate([x, jnp.zeros((n - x.shape[0], x.shape[1]), F32)], axis=0)

    def head_cols(h):
        return slice(h * HEAD_DIM, (h + 1) * HEAD_DIM)

    def query(h):
        return pad_rows(q_ref[:, head_cols(h)], SQ).astype(BF16)

    def update(key, value, d):
        mult = _multiplicity(d)
        dist = jnp.where(mult > 0.0, d.astype(F32), MASKED)
        n_rep = d.shape[1] // LANES
        scores = [lax.dot_general(query(h), key(h), (((1,), (1,)), ((), ())),
                                  preferred_element_type=F32) for h in range(N_HEADS)]
        staged = []
        for h, s in enumerate(scores):
            s = s * SCALE - sl_ref[h] * dist
            m_prev = m_s[h]
            m_new = jnp.maximum(m_prev, jnp.max(s, axis=-1, keepdims=True))
            p = mult * jnp.exp(s - jnp.concatenate([m_new] * n_rep, axis=1))
            staged.append((p, m_new, jnp.exp(m_prev - m_new)))
        for h, (p, m_new, alpha) in enumerate(staged):
            l_s[h] = alpha * l_s[h] + jnp.sum(p, axis=-1, keepdims=True)
            acc[h] = alpha * acc[h] + jnp.dot(p.astype(BF16), value(h), preferred_element_type=F32)
            m_s[h] = m_new

    t_c = lax.broadcasted_iota(jnp.int32, (SQ, tc), 0)
    c_c = lax.broadcasted_iota(jnp.int32, (SQ, tc), 1)
    d_cache = (wb + t_c) - (c * tc + c_c)
    for r in range(4):
        xk_ref[r] = kc_ref[pl.ds(r, 4 * tc, stride=4), :]
        xv_ref[r] = vc_ref[pl.ds(r, 4 * tc, stride=4), :]
    update(lambda h: xk_ref[h % 4, pl.ds(h // 4, tc, stride=4), :].astype(BF16),
           lambda h: xv_ref[h % 4, pl.ds(h // 4, tc, stride=4), :].astype(BF16), d_cache)

    @pl.when(c == last)
    def _():
        t_n = lax.broadcasted_iota(jnp.int32, (SQ, LANES), 0)
        c_n = lax.broadcasted_iota(jnp.int32, (SQ, LANES), 1)
        d_new = jnp.where(c_n < ts, t_n - c_n, -1)
        update(lambda h: pad_rows(kn_ref[:, head_cols(h)], LANES).astype(BF16),
               lambda h: pad_rows(vn_ref[:, head_cols(h)], LANES).astype(BF16), d_new)
        for h in range(N_HEADS):
            o_ref[:, head_cols(h)] = (acc[h] / l_s[h])[:ts, :]


def sample_attention(slopes, q, k_new, v_new, cache_k, cache_v, layer):
    bs, ts, _ = q.shape
    wb = cache_k.shape[2] // N_HEADS
    tc = 512
    assert wb % tc == 0 and ts <= SQ and wb == WINDOW_MAX
    new = pl.BlockSpec((None, ts, D_ATT), lambda b, c: (b, 0, 0))
    cache = pl.BlockSpec((None, None, tc * N_HEADS, HEAD_DIM), lambda b, c: (layer, b, c, 0))
    return pl.pallas_call(
        functools.partial(_sattn_kernel, tc=tc, wb=wb, ts=ts),
        out_shape=jax.ShapeDtypeStruct((bs, ts, D_ATT), F32),
        grid=(bs, wb // tc),
        in_specs=[pl.BlockSpec(memory_space=pltpu.SMEM), new, new, new, cache, cache],
        out_specs=new,
        scratch_shapes=[pltpu.VMEM((N_HEADS, SQ, LANES), F32)] * 3
                       + [pltpu.VMEM((4, 4 * tc, HEAD_DIM), F32)] * 2,
        compiler_params=_params(2), name="sample_attention",
    )(slopes, q, k_new, v_new, cache_k, cache_v)


def _ssgu_kernel(att_ref, g_ref, u_ref, wt_ref, b_ref, att_o, ug_o, *, ts):
    att_o[...] = att_ref[...].astype(BF16)
    n = att_ref.shape[0]
    r_i = lax.broadcasted_iota(jnp.int32, (n, n), 0)
    c_i = lax.broadcasted_iota(jnp.int32, (n, n), 1)
    keep = ((r_i & -ts) == (c_i & -ts)) & ((c_i & (ts - 1)) <= (r_i & (ts - 1)))
    for gi in range(N_GROUPS):
        cols = slice(gi * HEAD_DIM, (gi + 1) * HEAD_DIM)
        w = jnp.where(keep, wt_ref[gi], 0.0).astype(BF16)
        gate = jnp.dot(w, g_ref[:, cols].astype(BF16), preferred_element_type=F32) + b_ref[:, gi:gi + 1]
        ug_o[:, cols] = (u_ref[:, cols] * gate).astype(BF16)


def sample_sgu(att, g, u, w_tiled, b_rows, ts):
    n = att.shape[0]
    return pl.pallas_call(
        functools.partial(_ssgu_kernel, ts=ts),
        out_shape=[jax.ShapeDtypeStruct((n, D_ATT), BF16), jax.ShapeDtypeStruct((n, D_SGU), BF16)],
        compiler_params=pltpu.CompilerParams(vmem_limit_bytes=VMEM_LIMIT),
        name="sample_sgu",
    )(att, g, u, w_tiled, b_rows)


def kernel(x_prompt, x_sample, cache_k_win, cache_v_win, norm1, w_in, q_gain, k_gain, sgu_gain,
           w_spatial, b_spatial, w_out, norm2, w_gate, w_up, w_down):
    bp, sp, _ = x_prompt.shape
    bs, ts, _ = x_sample.shape
    assert ts & (ts - 1) == 0
    depth = w_in.shape[0]
    wb = cache_k_win.shape[2]
    assert sp == WINDOW_MAX and ts <= 8 and wb == WINDOW_MAX
    mp, ms = bp * sp, bs * ts

    xp = x_prompt.reshape(mp, D_MODEL)
    xs = x_sample.reshape(ms, D_MODEL)
    cache_k = cache_k_win.reshape(depth, bs, wb * N_HEADS, HEAD_DIM)
    cache_v = cache_v_win.reshape(depth, bs, wb * N_HEADS, HEAD_DIM)
    slopes = jnp.exp2(-8.0 * jnp.arange(1, N_HEADS + 1, dtype=F32) / N_HEADS)
    b_t = jnp.swapaxes(b_spatial, 1, 2)

    k_stack = v_stack = None
    ks_rows, vs_rows, gs_rows = [], [], []
    for l in range(depth):
        qg = jnp.tile(q_gain[l], N_HEADS)
        kg = jnp.tile(k_gain[l], N_HEADS)
        sg = sgu_gain[l].reshape(D_SGU)

        hp = rms_rows(xp, norm1[l])
        hs = rms_rows(xs, norm1[l])
        sec = functools.partial(in_proj_section, hp, hs, w_in, l, seq=sp, depth=depth)
        q_s, *q_p = sec(0 * D_ATT, qg, want_f32=False, want_bf16=False, regroup=True)
        k_s, k_stack, *k_p = sec(1 * D_ATT, kg, want_f32=True, want_bf16=False, regroup=True,
                                 stack=k_stack)
        v_s, v_stack, *v_p = sec(2 * D_ATT, None, want_f32=True, want_bf16=False, regroup=True,
                                 stack=v_stack)
        u_s, u_p = sec(3 * D_ATT, None, want_f32=True, want_bf16=False, stack="unstacked")
        g_s, g_p = sec(3 * D_ATT + D_SGU, sg, want_f32=False, want_bf16=True)

        att_p = prompt_attention(slopes, q_p, k_p, v_p, bp, sp).reshape(mp, D_ATT)
        ug_p = prompt_sgu(g_p, u_p, w_spatial, b_t, l)

        r3 = lambda a: a.reshape(bs, ts, D_ATT)
        att_s = sample_attention(slopes, r3(q_s), r3(k_s), r3(v_s), cache_k, cache_v, l)
        w_tiled = jnp.tile(w_spatial[l][:, :ts, :ts], (1, bs, bs))
        b_rows = jnp.tile(b_spatial[l][:, :ts].T, (bs, 1))
        att_sb, ug_s = sample_sgu(att_s.reshape(ms, D_ATT), g_s, u_s, w_tiled, b_rows, ts)

        xp, xs = out_proj(att_p, ug_p, att_sb, ug_s, w_out, xp, xs, l)
        a_p, a_s, w_down_bf16 = ffn_gate_up(rms_rows(xp, norm2[l]), rms_rows(xs, norm2[l]),
                                            w_gate, w_up, w_down, l)
        xp, xs = ffn_down(a_p, a_s, w_down_bf16, xp, xs)

        ks_rows.append(k_s)
        vs_rows.append(v_s)
        gs_rows.append(g_s)

    heads = lambda a, b, t: a.reshape(depth, b, t, N_HEADS, HEAD_DIM)
    return (xp.reshape(bp, sp, D_MODEL), xs.reshape(bs, ts, D_MODEL),
            heads(k_stack, bp, sp), heads(v_stack, bp, sp),
            heads(jnp.stack(ks_rows), bs, ts), heads(jnp.stack(vs_rows), bs, ts),
            heads(jnp.stack(gs_rows), bs, ts))
```

```python
import functools

import jax
import jax.numpy as jnp
from jax import lax
from jax.experimental import pallas as pl
from jax.experimental.pallas import tpu as pltpu

F32 = jnp.float32
BF16 = jnp.bfloat16

D_MODEL = 4096
HEAD_DIM = 128
N_HEADS = 16
D_ATT = N_HEADS * HEAD_DIM
D_SGU = D_MODEL - D_ATT
N_GROUPS = D_SGU // HEAD_DIM
CHUNK = 128
WINDOW_MAX = 2048
RMS_EPS = 1e-6
SCALE = HEAD_DIM ** -0.5
MASKED = 1e30
LANES = 128
VMEM_LIMIT = 60 * 1024 * 1024

HEADS_PER_STEP = 4
QB = 128

TM_PROJ = 1024
TN_PROJ = 512
TM_IN = 512
TN_IN = 1024
TN_FFN = 256
TM_FFN = 2048
TM_DOWN = 512
TN_DOWN = 512


def _params(n_axes):
    return pltpu.CompilerParams(dimension_semantics=("arbitrary",) * n_axes,
                                vmem_limit_bytes=VMEM_LIMIT)


def _cast_weight(w_ref, wb_ref):
    k = w_ref.shape[0]
    ck = 256
    def body(c, carry):
        r = pl.multiple_of(c * ck, ck)
        wb_ref[pl.ds(r, ck), :] = w_ref[pl.ds(r, ck), :].astype(BF16)
        return carry
    lax.fori_loop(0, k // ck, body, 0)


def _weight_tile_copy(w_hbm, layer, col, stage_ref, sem):
    col = pl.multiple_of(col, LANES)
    return pltpu.make_async_copy(w_hbm.at[layer, :, pl.ds(col, stage_ref.shape[1])], stage_ref, sem)


def _take_weight_tile(w_hbm, layer, col_of, j, nj, stage_ref, sem, wb_ref):
    @pl.when(j == 0)
    def _():
        _weight_tile_copy(w_hbm, layer, col_of(0), stage_ref, sem).start()
    _weight_tile_copy(w_hbm, layer, col_of(j), stage_ref, sem).wait()
    _cast_weight(stage_ref, wb_ref)

    @pl.when(j + 1 < nj)
    def _():
        _weight_tile_copy(w_hbm, layer, col_of(j + 1), stage_ref, sem).start()


def _is_first_row_tile():
    return pl.program_id(1) == 0


def _is_last_row_tile():
    return pl.program_id(1) == pl.num_programs(1) - 1


def _rms_rows_kernel(x_ref, g_ref, o_ref):
    x = x_ref[...]
    ms = jnp.mean(x * x, axis=-1, keepdims=True)
    o_ref[...] = (x * lax.rsqrt(ms + RMS_EPS) * g_ref[...]).astype(o_ref.dtype)


def rms_rows(x, gain):
    m, d = x.shape
    tm = min(m, 512)
    return pl.pallas_call(
        _rms_rows_kernel,
        out_shape=jax.ShapeDtypeStruct((m, d), BF16),
        grid=(m // tm,),
        in_specs=[pl.BlockSpec((tm, d), lambda i: (i, 0)),
                  pl.BlockSpec((1, d), lambda i: (0, 0))],
        out_specs=pl.BlockSpec((tm, d), lambda i: (i, 0)),
        compiler_params=_params(1),
        name="rms_rows",
    )(x, gain.reshape(1, d))


def _head_norm(z, gain):
    parts = []
    for hh in range(z.shape[1] // HEAD_DIM):
        cols = slice(hh * HEAD_DIM, (hh + 1) * HEAD_DIM)
        zh = z[:, cols]
        ms = jnp.mean(zh * zh, axis=-1, keepdims=True)
        parts.append(zh * lax.rsqrt(ms + RMS_EPS) * gain[:, cols])
    return jnp.concatenate(parts, axis=1)


def _inproj_kernel(*refs, norm, want_f32, want_bf16, regroup, has_alias, ni, n_tiles, layer, col0):
    it = iter(refs)
    h_ref = next(it)
    hs_ref = next(it)
    w_ref = next(it)
    gain_ref = next(it) if norm else None
    if has_alias:
        next(it)
    s_ref = next(it)
    f32_ref = next(it) if want_f32 else None
    bf_ref = next(it) if want_bf16 else None
    o4_ref = next(it) if regroup else None
    o16_ref = next(it) if regroup else None
    wb_ref = next(it)
    stage_ref = next(it)
    sem = next(it)
    z_refs = (next(it), next(it))
    z4_ref = next(it) if regroup else None

    s = pl.program_id(0)
    tm = h_ref.shape[0]
    n_slabs = wb_ref.shape[1] // LANES
    active = s < n_tiles

    def finish(z_ref):
        for c in range(n_slabs):
            cols = slice(c * LANES, (c + 1) * LANES)
            z = z_ref[c]
            if norm:
                z = _head_norm(z, gain_ref[:, cols])
            if want_f32:
                f32_ref[:, cols] = z
            if want_bf16:
                bf_ref[:, cols] = z.astype(BF16)
            if regroup:
                if norm:
                    z_ref[c] = z
                for r in range(4):
                    z4 = z_ref[c, pl.ds(r, tm // 4, stride=4), :]
                    o4_ref[r, :, cols] = z4.astype(BF16)
                    z4_ref[c, r] = z4
                for r in range(4):
                    for c4 in range(4):
                        o16_ref[r + 4 * c4, :, cols] = (
                            z4_ref[c, r, pl.ds(c4, tm // 16, stride=4), :].astype(BF16))

    def multiply(z_ref):
        wide = 2 * LANES
        for c in range(wb_ref.shape[1] // wide):
            z = jnp.dot(h_ref[...], wb_ref[:, c * wide:(c + 1) * wide], preferred_element_type=F32)
            z_ref[2 * c] = z[:, :LANES]
            z_ref[2 * c + 1] = z[:, LANES:]

    @pl.when(s == 0)
    def _():
        z_refs[1][...] = jnp.zeros(z_refs[1].shape, F32)

    @pl.when(active & (s % ni == 0))
    def _():
        tn = wb_ref.shape[1]
        _take_weight_tile(w_ref, layer, lambda j: col0 + j * tn, s // ni, n_tiles // ni,
                          stage_ref, sem, wb_ref)

    for par in range(2):
        @pl.when(active & (s % 2 == par))
        def _():
            finish(z_refs[1 - par])
            multiply(z_refs[par])

    @pl.when(s == n_tiles)
    def _():
        finish(z_refs[(n_tiles - 1) % 2])

    @pl.when(active & (s % ni == ni - 1))
    def _():
        z = jnp.dot(hs_ref[...], wb_ref[...], preferred_element_type=F32)
        if norm:
            z = _head_norm(z, gain_ref[...])
        s_ref[...] = z


def in_proj_section(h, hs, w_in, layer, col0, gain, *, want_f32, want_bf16, regroup=False,
                    seq=None, stack=None, depth=1):
    m, k = h.shape
    ms = hs.shape[0]
    n_sec = D_ATT
    tn, tm = TN_IN, TM_IN
    ni, nj = m // tm, n_sec // tn
    n_tiles = ni * nj
    assert ni >= 2
    norm = gain is not None
    stacked = want_f32 and stack != "unstacked"
    has_alias = stacked and layer > 0

    def cur(s):
        t = jnp.minimum(s, n_tiles - 1)
        return t // ni, t % ni
    def prev(s):
        t = jnp.maximum(s - 1, 0)
        return t // ni, t % ni

    in_specs = [pl.BlockSpec((tm, k), lambda s: (cur(s)[1], 0)),
                pl.BlockSpec((ms, k), lambda s: (0, 0)),
                pl.BlockSpec(memory_space=pl.ANY)]
    args = [h, hs, w_in]
    if norm:
        in_specs.append(pl.BlockSpec((1, tn), lambda s: (0, prev(s)[0])))
        args.append(gain.reshape(1, n_sec))
    if has_alias:
        in_specs.append(pl.BlockSpec(memory_space=pl.ANY))
        args.append(stack)

    out_shape = [jax.ShapeDtypeStruct((ms, n_sec), F32)]
    out_specs = [pl.BlockSpec((ms, tn), lambda s: (0, cur(s)[0]))]
    if want_f32:
        row0 = layer * ni if stacked else 0
        out_shape.append(jax.ShapeDtypeStruct(((depth if stacked else 1) * m, n_sec), F32))
        out_specs.append(pl.BlockSpec((tm, tn), lambda s: (row0 + prev(s)[1], prev(s)[0])))
    if want_bf16:
        out_shape.append(jax.ShapeDtypeStruct((m, n_sec), BF16))
        out_specs.append(pl.BlockSpec((tm, tn), lambda s: (prev(s)[1], prev(s)[0])))
    if regroup:
        b = m // seq
        tpb = seq // tm
        grouped = lambda s: (prev(s)[1] // tpb, 0, prev(s)[1] % tpb, prev(s)[0])
        out_shape.append(jax.ShapeDtypeStruct((b, 4, seq // 4, n_sec), BF16))
        out_specs.append(pl.BlockSpec((None, 4, tm // 4, tn), grouped))
        out_shape.append(jax.ShapeDtypeStruct((b, 16, seq // 16, n_sec), BF16))
        out_specs.append(pl.BlockSpec((None, 16, tm // 16, tn), grouped))
    scratch = ([pltpu.VMEM((k, tn), BF16), pltpu.VMEM((k, tn), F32), pltpu.SemaphoreType.DMA]
               + [pltpu.VMEM((tn // LANES, tm, LANES), F32)] * 2)
    if regroup:
        scratch.append(pltpu.VMEM((tn // LANES, 4, tm // 4, LANES), F32))

    kern = functools.partial(_inproj_kernel, norm=norm, want_f32=want_f32, want_bf16=want_bf16,
                             regroup=regroup, has_alias=has_alias, ni=ni, n_tiles=n_tiles,
                             layer=layer, col0=col0)
    return pl.pallas_call(
        kern, out_shape=out_shape, grid=(n_tiles + 1,), in_specs=in_specs, out_specs=out_specs,
        scratch_shapes=scratch,
        input_output_aliases=({len(args) - 1: 1} if has_alias else {}),
        compiler_params=_params(1), name="in_proj",
    )(*args)


def _attn_tiles(qkv, dist, slopes, prev):
    scores = [lax.dot_general(q, k, (((1,), (1,)), ((), ())), preferred_element_type=F32)
              for q, k, _ in qkv]
    probs = []
    for hh, s in enumerate(scores):
        s = s * SCALE - slopes[hh] * dist
        m_cur = jnp.max(s, axis=-1, keepdims=True)
        if prev is None:
            probs.append((jnp.exp(s - m_cur).astype(BF16), jnp.broadcast_to(m_cur, (QB, LANES)), None))
        else:
            m_prev = prev[hh][1]
            m_new = jnp.maximum(m_prev, m_cur)
            p = jnp.exp(s - jnp.concatenate([m_new] * (s.shape[1] // LANES), axis=1))
            probs.append((p.astype(BF16), m_new, jnp.exp(m_prev - m_new)))
    new = []
    for hh, (p, m_new, alpha) in enumerate(probs):
        v = qkv[hh][2]
        v_ones = jnp.concatenate([v, jnp.ones(v.shape, BF16)], axis=1)
        pv = jnp.dot(p, v_ones, preferred_element_type=F32)
        if prev is None:
            new.append((pv[:, :LANES], m_new, pv[:, LANES:]))
        else:
            new.append((alpha * prev[hh][0] + pv[:, :LANES], m_new, alpha * prev[hh][2] + pv[:, LANES:]))
    return new


def _window_dist(d, dil):
    return jnp.where((d >= 0) & (d <= 128), (d * dil).astype(F32), MASKED)


def _attn_kernel(sl_ref, q4, k4, v4, q16, k16, v16, o_ref, acc, m_s, l_s, dist_s, nat_s, *, seq):
    g = pl.program_id(1)
    s4 = seq // 4
    state = (acc, m_s, l_s)

    o_i = lax.broadcasted_iota(jnp.int32, (QB, 2 * QB), 0)
    c_i = lax.broadcasted_iota(jnp.int32, (QB, 2 * QB), 1)
    d1 = 4 * ((o_i & 31) - (c_i & 63)) + ((o_i >> 5) - (c_i >> 6))
    dist_s[0] = _window_dist(d1, 1)
    dist_s[1] = _window_dist(d1 + 128, 1)
    dist_s[2] = _window_dist(o_i - c_i, 4)
    dist_s[3] = _window_dist(o_i - c_i + 128, 4)
    dist_s[4] = _window_dist(o_i - c_i, 16)

    heads = range(HEADS_PER_STEP)

    def load_state(rows):
        return [tuple(ref[hh, rows, :] for ref in state) for hh in heads]

    def store_state(rows_list, new):
        for hh in heads:
            for ref, val in zip(state, new[hh]):
                n = val.shape[0] // len(rows_list)
                for idx, rows in enumerate(rows_list):
                    ref[hh, rows, :] = val[idx * n:(idx + 1) * n, :]

    def head_cols(hh):
        return slice(hh * HEAD_DIM, (hh + 1) * HEAD_DIM)

    slopes = [sl_ref[g * HEADS_PER_STEP + hh] for hh in heads]

    def body1(qb, carry):
        t0 = pl.multiple_of(qb * 32, 32)
        kt0 = pl.multiple_of(jnp.maximum(qb - 1, 0) * 32, 32)
        dist = dist_s[jnp.minimum(qb, 1)]
        def slabs(ref, start, n, hh):
            return jnp.concatenate([ref[r, pl.ds(start, n), head_cols(hh)] for r in range(4)], axis=0)
        qkv = [(slabs(q4, t0, 32, hh), slabs(k4, kt0, 64, hh), slabs(v4, kt0, 64, hh)) for hh in heads]
        new = _attn_tiles(qkv, dist, slopes, None)
        store_state([pl.ds(r * s4 + t0, 32) for r in range(4)], new)
        return carry
    lax.fori_loop(0, seq // QB, body1, 0, unroll=4)

    n_qb4 = s4 // QB
    assert n_qb4 == 4
    def body4(idx, carry):
        r = idx >> 2
        qb = idx & 3
        q0 = pl.multiple_of(qb * QB, QB)
        k0 = pl.multiple_of(jnp.maximum(qb - 1, 0) * QB, QB)
        dist = dist_s[2 + jnp.minimum(qb, 1)]
        rows = pl.ds(r * s4 + q0, QB)
        qkv = [(q4[r, pl.ds(q0, QB), head_cols(hh)], k4[r, pl.ds(k0, 2 * QB), head_cols(hh)],
                v4[r, pl.ds(k0, 2 * QB), head_cols(hh)]) for hh in heads]
        new = _attn_tiles(qkv, dist, slopes, load_state(rows))
        store_state([rows], new)
        return carry
    lax.fori_loop(0, 4 * n_qb4, body4, 0, unroll=4)

    def body16(r16, carry):
        dist = dist_s[4][:, :QB]
        rows = pl.ds((r16 & 3) * s4 + (r16 >> 2), QB, stride=4)
        qkv = [(q16[r16, :, head_cols(hh)], k16[r16, :, head_cols(hh)], v16[r16, :, head_cols(hh)])
               for hh in heads]
        new = _attn_tiles(qkv, dist, slopes, load_state(rows))
        store_state([rows], new)
        return carry
    lax.fori_loop(0, 16, body16, 0, unroll=4)

    tn = nat_s.shape[1] // 4
    def body_out(c, carry):
        t0 = pl.multiple_of(c * tn, tn)
        for hh in range(HEADS_PER_STEP):
            for r in range(4):
                rows = pl.ds(r * s4 + t0, tn)
                nat_s[hh, pl.ds(r, tn, stride=4), :] = acc[hh, rows, :] / l_s[hh, rows, :]
            o_ref[pl.ds(pl.multiple_of(4 * t0, 4 * tn), 4 * tn), head_cols(hh)] = (
                nat_s[hh].astype(o_ref.dtype))
        return carry
    lax.fori_loop(0, s4 // tn, body_out, 0)


def prompt_attention(slopes, qs, ks, vs, batch, seq):
    assert seq // 16 == QB and seq % (4 * QB) == 0
    tw = HEADS_PER_STEP * HEAD_DIM
    r4 = pl.BlockSpec((None, 4, seq // 4, tw), lambda b, g: (b, 0, 0, g))
    r16 = pl.BlockSpec((None, 16, seq // 16, tw), lambda b, g: (b, 0, 0, g))
    state = pltpu.VMEM((HEADS_PER_STEP, seq, LANES), F32)
    return pl.pallas_call(
        functools.partial(_attn_kernel, seq=seq),
        out_shape=jax.ShapeDtypeStruct((batch, seq, D_ATT), BF16),
        grid=(batch, N_HEADS // HEADS_PER_STEP),
        in_specs=[pl.BlockSpec(memory_space=pltpu.SMEM), r4, r4, r4, r16, r16, r16],
        out_specs=pl.BlockSpec((None, seq, tw), lambda b, g: (b, 0, g)),
        scratch_shapes=[state, state, state,
                        pltpu.VMEM((5, QB, 2 * QB), F32),
                        pltpu.VMEM((HEADS_PER_STEP, 256, LANES), F32)],
        compiler_params=_params(2), name="prompt_attention",
    )(slopes, qs[0], ks[0], vs[0], qs[1], ks[1], vs[1])


def _sgu_kernel(g_ref, u_ref, w_ref, bt_ref, o_ref, wt_ref):
    @pl.when(pl.program_id(0) == 0)
    def _():
        r_i = lax.broadcasted_iota(jnp.int32, (CHUNK, CHUNK), 0)
        c_i = lax.broadcasted_iota(jnp.int32, (CHUNK, CHUNK), 1)
        for gi in range(N_GROUPS):
            wt_ref[gi] = jnp.where(r_i >= c_i, w_ref[gi], 0.0).astype(BF16)

    n_chunks = g_ref.shape[0] // CHUNK
    for gi in range(N_GROUPS):
        cols = slice(gi * HEAD_DIM, (gi + 1) * HEAD_DIM)
        g_wide = jnp.concatenate([g_ref[c * CHUNK:(c + 1) * CHUNK, cols] for c in range(n_chunks)],
                                 axis=1)
        gate = jnp.dot(wt_ref[gi], g_wide, preferred_element_type=F32)
        for c in range(n_chunks):
            rows = slice(c * CHUNK, (c + 1) * CHUNK)
            gate_c = gate[:, c * HEAD_DIM:(c + 1) * HEAD_DIM] + bt_ref[:, gi:gi + 1]
            o_ref[rows, cols] = (u_ref[rows, cols] * gate_c).astype(BF16)


SGU_CHUNKS_PER_STEP = 4


def prompt_sgu(g, u, w_spatial, b_t, layer):
    m = g.shape[0]
    tm = SGU_CHUNKS_PER_STEP * CHUNK
    row = pl.BlockSpec((tm, D_SGU), lambda c: (c, 0))
    return pl.pallas_call(
        _sgu_kernel,
        out_shape=jax.ShapeDtypeStruct((m, D_SGU), BF16),
        grid=(m // tm,),
        in_specs=[row, row,
                  pl.BlockSpec((None, N_GROUPS, CHUNK, CHUNK), lambda c: (layer, 0, 0, 0)),
                  pl.BlockSpec((None, CHUNK, N_GROUPS), lambda c: (layer, 0, 0))],
        out_specs=row,
        scratch_shapes=[pltpu.VMEM((N_GROUPS, CHUNK, CHUNK), BF16)],
        compiler_params=_params(1), name="prompt_sgu",
    )(g, u, w_spatial, b_t)


def _outproj_kernel(a_ref, ug_ref, as_ref, ugs_ref, w_ref, x_ref, xs_ref, o_ref, os_ref,
                    wb_ref, stage_ref, sem, *, layer):
    @pl.when(_is_first_row_tile())
    def _():
        tn = wb_ref.shape[1]
        _take_weight_tile(w_ref, layer, lambda j: j * tn, pl.program_id(0), pl.num_programs(0),
                          stage_ref, sem, wb_ref)

    def project(a, ug):
        z = jnp.dot(a, wb_ref[:D_ATT, :], preferred_element_type=F32)
        return z + jnp.dot(ug, wb_ref[D_ATT:, :], preferred_element_type=F32)

    o_ref[...] = x_ref[...] + project(a_ref[...], ug_ref[...])

    @pl.when(_is_last_row_tile())
    def _():
        os_ref[...] = xs_ref[...] + project(as_ref[...], ugs_ref[...])


def out_proj(att, ug, att_s, ug_s, w_out, x, xs, layer):
    m, ms = x.shape[0], xs.shape[0]
    tn, tm = TN_IN, TM_IN
    lhs = pl.BlockSpec((tm, D_ATT), lambda j, i: (i, 0))
    lhs_s = pl.BlockSpec((ms, D_ATT), lambda j, i: (0, 0))
    res = pl.BlockSpec((tm, tn), lambda j, i: (i, j))
    res_s = pl.BlockSpec((ms, tn), lambda j, i: (0, j))
    return pl.pallas_call(
        functools.partial(_outproj_kernel, layer=layer),
        out_shape=[jax.ShapeDtypeStruct((m, D_MODEL), F32), jax.ShapeDtypeStruct((ms, D_MODEL), F32)],
        grid=(D_MODEL // tn, m // tm),
        in_specs=[lhs, lhs, lhs_s, lhs_s, pl.BlockSpec(memory_space=pl.ANY), res, res_s],
        out_specs=[res, res_s],
        scratch_shapes=[pltpu.VMEM((D_MODEL, tn), BF16), pltpu.VMEM((D_MODEL, tn), F32),
                        pltpu.SemaphoreType.DMA],
        compiler_params=_params(2), name="out_proj",
    )(att, ug, att_s, ug_s, w_out, x, xs)


def _ffn1_kernel(h_ref, hs_ref, wg_ref, wu_ref, wd_ref, a_ref, as_ref, wdb_ref,
                 wgb_ref, wub_ref, gstage_ref, ustage_ref, sems, *, layer):
    wdb_ref[...] = wd_ref[...].astype(BF16)

    @pl.when(_is_first_row_tile())
    def _():
        tn = wgb_ref.shape[1]
        j, nj = pl.program_id(0), pl.num_programs(0)
        _take_weight_tile(wg_ref, layer, lambda jj: jj * tn, j, nj, gstage_ref, sems.at[0], wgb_ref)
        _take_weight_tile(wu_ref, layer, lambda jj: jj * tn, j, nj, ustage_ref, sems.at[1], wub_ref)

    def gated(h):
        g = jnp.dot(h, wgb_ref[...], preferred_element_type=F32)
        u = jnp.dot(h, wub_ref[...], preferred_element_type=F32)
        return (g / (1.0 + jnp.exp(-g)) * u).astype(BF16)

    a_ref[...] = gated(h_ref[...])

    @pl.when(_is_last_row_tile())
    def _():
        as_ref[...] = gated(hs_ref[...])


def ffn_gate_up(h, hs, w_gate, w_up, w_down, layer):
    m, k = h.shape
    ms = hs.shape[0]
    d_ff = w_gate.shape[2]
    tn, tm = TN_FFN, TM_FFN
    nj, ni = d_ff // tn, m // tm
    rb = d_ff // (nj * ni)
    assert rb * nj * ni == d_ff and rb % 16 == 0
    wspec = pl.BlockSpec(memory_space=pl.ANY)
    return pl.pallas_call(
        functools.partial(_ffn1_kernel, layer=layer),
        out_shape=[jax.ShapeDtypeStruct((m, d_ff), BF16), jax.ShapeDtypeStruct((ms, d_ff), BF16),
                   jax.ShapeDtypeStruct((d_ff, D_MODEL), BF16)],
        grid=(nj, ni),
        in_specs=[pl.BlockSpec((tm, k), lambda j, i: (i, 0)),
                  pl.BlockSpec((ms, k), lambda j, i: (0, 0)), wspec, wspec,
                  pl.BlockSpec((None, rb, D_MODEL), lambda j, i: (layer, j * ni + i, 0))],
        out_specs=[pl.BlockSpec((tm, tn), lambda j, i: (i, j)),
                   pl.BlockSpec((ms, tn), lambda j, i: (0, j)),
                   pl.BlockSpec((rb, D_MODEL), lambda j, i: (j * ni + i, 0))],
        scratch_shapes=[pltpu.VMEM((k, tn), BF16)] * 2 + [pltpu.VMEM((k, tn), F32)] * 2
                       + [pltpu.SemaphoreType.DMA((2,))],
        compiler_params=_params(2), name="ffn_gate_up",
    )(h, hs, w_gate, w_up, w_down)


def _ffn2_kernel(a_ref, as_ref, w_ref, x_ref, xs_ref, o_ref, os_ref):
    o_ref[...] = x_ref[...] + jnp.dot(a_ref[...], w_ref[...], preferred_element_type=F32)

    @pl.when(_is_last_row_tile())
    def _():
        os_ref[...] = xs_ref[...] + jnp.dot(as_ref[...], w_ref[...], preferred_element_type=F32)


def ffn_down(a, a_s, w_down_bf16, x, xs):
    m, k = a.shape
    ms = a_s.shape[0]
    tn, tm = TN_DOWN, TM_DOWN
    res = pl.BlockSpec((tm, tn), lambda j, i: (i, j))
    res_s = pl.BlockSpec((ms, tn), lambda j, i: (0, j))
    return pl.pallas_call(
        _ffn2_kernel,
        out_shape=[jax.ShapeDtypeStruct((m, D_MODEL), F32), jax.ShapeDtypeStruct((ms, D_MODEL), F32)],
        grid=(D_MODEL // tn, m // tm),
        in_specs=[pl.BlockSpec((tm, k), lambda j, i: (i, 0)),
                  pl.BlockSpec((ms, k), lambda j, i: (0, 0)),
                  pl.BlockSpec((k, tn), lambda j, i: (0, j)),
                  res, res_s],
        out_specs=[res, res_s],
        compiler_params=_params(2), name="ffn_down",
    )(a, a_s, w_down_bf16, x, xs)


SQ = 16


def _multiplicity(d):
    nonneg = d >= 0
    n = (nonneg & (d <= 128)).astype(F32)
    n = n + (nonneg & (d <= 512) & ((d & 3) == 0)).astype(F32)
    n = n + (nonneg & (d <= 2048) & ((d & 15) == 0)).astype(F32)
    return n


def _sattn_kernel(sl_ref, q_ref, kn_ref, vn_ref, kc_ref, vc_ref, o_ref, acc, m_s, l_s, xk_ref, xv_ref,
                  *, tc, wb, ts):
    c = pl.program_id(1)
    last = pl.num_programs(1) - 1

    @pl.when(c == 0)
    def _():
        m_s[...] = jnp.full(m_s.shape, -MASKED, F32)
        l_s[...] = jnp.zeros(l_s.shape, F32)
        acc[...] = jnp.zeros(acc.shape, F32)

    def pad_rows(x, n):
        return jnp.concatenate([x, jnp.zeros((n - x.shape[0], x.shape[1]), F32)], axis=0)

    def head_cols(h):
        return slice(h * HEAD_DIM, (h + 1) * HEAD_DIM)

    def query(h):
        return pad_rows(q_ref[:, head_cols(h)], SQ).astype(BF16)

    def update(key, value, d):
        mult = _multiplicity(d)
        dist = jnp.where(mult > 0.0, d.astype(F32), MASKED)
        n_rep = d.shape[1] // LANES
        scores = [lax.dot_general(query(h), key(h), (((1,), (1,)), ((), ())),
                                  preferred_element_type=F32) for h in range(N_HEADS)]
        staged = []
        for h, s in enumerate(scores):
            s = s * SCALE - sl_ref[h] * dist
            m_prev = m_s[h]
            m_new = jnp.maximum(m_prev, jnp.max(s, axis=-1, keepdims=True))
            p = mult * jnp.exp(s - jnp.concatenate([m_new] * n_rep, axis=1))
            staged.append((p, m_new, jnp.exp(m_prev - m_new)))
        for h, (p, m_new, alpha) in enumerate(staged):
            l_s[h] = alpha * l_s[h] + jnp.sum(p, axis=-1, keepdims=True)
            acc[h] = alpha * acc[h] + jnp.dot(p.astype(BF16), value(h), preferred_element_type=F32)
            m_s[h] = m_new

    t_c = lax.broadcasted_iota(jnp.int32, (SQ, tc), 0)
    c_c = lax.broadcasted_iota(jnp.int32, (SQ, tc), 1)
    d_cache = (wb + t_c) - (c * tc + c_c)
    for r in range(4):
        xk_ref[r] = kc_ref[pl.ds(r, 4 * tc, stride=4), :]
        xv_ref[r] = vc_ref[pl.ds(r, 4 * tc, stride=4), :]
    update(lambda h: xk_ref[h % 4, pl.ds(h // 4, tc, stride=4), :].astype(BF16),
           lambda h: xv_ref[h % 4, pl.ds(h // 4, tc, stride=4), :].astype(BF16), d_cache)

    @pl.when(c == last)
    def _():
        t_n = lax.broadcasted_iota(jnp.int32, (SQ, LANES), 0)
        c_n = lax.broadcasted_iota(jnp.int32, (SQ, LANES), 1)
        d_new = jnp.where(c_n < ts, t_n - c_n, -1)
        update(lambda h: pad_rows(kn_ref[:, head_cols(h)], LANES).astype(BF16),
               lambda h: pad_rows(vn_ref[:, head_cols(h)], LANES).astype(BF16), d_new)
        for h in range(N_HEADS):
            o_ref[:, head_cols(h)] = (acc[h] / l_s[h])[:ts, :]


def sample_attention(slopes, q, k_new, v_new, cache_k, cache_v, layer):
    bs, ts, _ = q.shape
    wb = cache_k.shape[2] // N_HEADS
    tc = 512
    assert wb % tc == 0 and ts <= SQ and wb == WINDOW_MAX
    new = pl.BlockSpec((None, ts, D_ATT), lambda b, c: (b, 0, 0))
    cache = pl.BlockSpec((None, None, tc * N_HEADS, HEAD_DIM), lambda b, c: (layer, b, c, 0))
    return pl.pallas_call(
        functools.partial(_sattn_kernel, tc=tc, wb=wb, ts=ts),
        out_shape=jax.ShapeDtypeStruct((bs, ts, D_ATT), F32),
        grid=(bs, wb // tc),
        in_specs=[pl.BlockSpec(memory_space=pltpu.SMEM), new, new, new, cache, cache],
        out_specs=new,
        scratch_shapes=[pltpu.VMEM((N_HEADS, SQ, LANES), F32)] * 3
                       + [pltpu.VMEM((4, 4 * tc, HEAD_DIM), F32)] * 2,
        compiler_params=_params(2), name="sample_attention",
    )(slopes, q, k_new, v_new, cache_k, cache_v)


def _ssgu_kernel(att_ref, g_ref, u_ref, wt_ref, b_ref, att_o, ug_o, *, ts):
    att_o[...] = att_ref[...].astype(BF16)
    n = att_ref.shape[0]
    r_i = lax.broadcasted_iota(jnp.int32, (n, n), 0)
    c_i = lax.broadcasted_iota(jnp.int32, (n, n), 1)
    keep = ((r_i & -ts) == (c_i & -ts)) & ((c_i & (ts - 1)) <= (r_i & (ts - 1)))
    for gi in range(N_GROUPS):
        cols = slice(gi * HEAD_DIM, (gi + 1) * HEAD_DIM)
        w = jnp.where(keep, wt_ref[gi], 0.0).astype(BF16)
        gate = jnp.dot(w, g_ref[:, cols].astype(BF16), preferred_element_type=F32) + b_ref[:, gi:gi + 1]
        ug_o[:, cols] = (u_ref[:, cols] * gate).astype(BF16)


def sample_sgu(att, g, u, w_tiled, b_rows, ts):
    n = att.shape[0]
    return pl.pallas_call(
        functools.partial(_ssgu_kernel, ts=ts),
        out_shape=[jax.ShapeDtypeStruct((n, D_ATT), BF16), jax.ShapeDtypeStruct((n, D_SGU), BF16)],
        compiler_params=pltpu.CompilerParams(vmem_limit_bytes=VMEM_LIMIT),
        name="sample_sgu",
    )(att, g, u, w_tiled, b_rows)


def kernel(x_prompt, x_sample, cache_k_win, cache_v_win, norm1, w_in, q_gain, k_gain, sgu_gain,
           w_spatial, b_spatial, w_out, norm2, w_gate, w_up, w_down):
    bp, sp, _ = x_prompt.shape
    bs, ts, _ = x_sample.shape
    assert ts & (ts - 1) == 0
    depth = w_in.shape[0]
    wb = cache_k_win.shape[2]
    assert sp == WINDOW_MAX and ts <= 8 and wb == WINDOW_MAX
    mp, ms = bp * sp, bs * ts

    xp = x_prompt.reshape(mp, D_MODEL)
    xs = x_sample.reshape(ms, D_MODEL)
    cache_k = cache_k_win.reshape(depth, bs, wb * N_HEADS, HEAD_DIM)
    cache_v = cache_v_win.reshape(depth, bs, wb * N_HEADS, HEAD_DIM)
    slopes = jnp.exp2(-8.0 * jnp.arange(1, N_HEADS + 1, dtype=F32) / N_HEADS)
    b_t = jnp.swapaxes(b_spatial, 1, 2)

    k_stack = v_stack = None
    ks_rows, vs_rows, gs_rows = [], [], []
    for l in range(depth):
        qg = jnp.tile(q_gain[l], N_HEADS)
        kg = jnp.tile(k_gain[l], N_HEADS)
        sg = sgu_gain[l].reshape(D_SGU)

        hp = rms_rows(xp, norm1[l])
        hs = rms_rows(xs, norm1[l])
        sec = functools.partial(in_proj_section, hp, hs, w_in, l, seq=sp, depth=depth)
        q_s, *q_p = sec(0 * D_ATT, qg, want_f32=False, want_bf16=False, regroup=True)
        k_s, k_stack, *k_p = sec(1 * D_ATT, kg, want_f32=True, want_bf16=False, regroup=True,
                                 stack=k_stack)
        v_s, v_stack, *v_p = sec(2 * D_ATT, None, want_f32=True, want_bf16=False, regroup=True,
                                 stack=v_stack)
        u_s, u_p = sec(3 * D_ATT, None, want_f32=True, want_bf16=False, stack="unstacked")
        g_s, g_p = sec(3 * D_ATT + D_SGU, sg, want_f32=False, want_bf16=True)

        att_p = prompt_attention(slopes, q_p, k_p, v_p, bp, sp).reshape(mp, D_ATT)
        ug_p = prompt_sgu(g_p, u_p, w_spatial, b_t, l)

        r3 = lambda a: a.reshape(bs, ts, D_ATT)
        att_s = sample_attention(slopes, r3(q_s), r3(k_s), r3(v_s), cache_k, cache_v, l)
        w_tiled = jnp.tile(w_spatial[l][:, :ts, :ts], (1, bs, bs))
        b_rows = jnp.tile(b_spatial[l][:, :ts].T, (bs, 1))
        att_sb, ug_s = sample_sgu(att_s.reshape(ms, D_ATT), g_s, u_s, w_tiled, b_rows, ts)

        xp, xs = out_proj(att_p, ug_p, att_sb, ug_s, w_out, xp, xs, l)
        a_p, a_s, w_down_bf16 = ffn_gate_up(rms_rows(xp, norm2[l]), rms_rows(xs, norm2[l]),
                                            w_gate, w_up, w_down, l)
        xp, xs = ffn_down(a_p, a_s, w_down_bf16, xp, xs)

        ks_rows.append(k_s)
        vs_rows.append(v_s)
        gs_rows.append(g_s)

    heads = lambda a, b, t: a.reshape(depth, b, t, N_HEADS, HEAD_DIM)
    return (xp.reshape(bp, sp, D_MODEL), xs.reshape(bs, ts, D_MODEL),
            heads(k_stack, bp, sp), heads(v_stack, bp, sp),
            heads(jnp.stack(ks_rows), bs, ts), heads(jnp.stack(vs_rows), bs, ts),
            heads(jnp.stack(gs_rows), bs, ts))
```

```python
import functools

import jax
import jax.numpy as jnp
from jax import lax
from jax.experimental import pallas as pl
from jax.experimental.pallas import tpu as pltpu

F32 = jnp.float32
BF16 = jnp.bfloat16

D_MODEL = 4096
HEAD_DIM = 128
N_HEADS = 16
D_ATT = N_HEADS * HEAD_DIM
D_SGU = D_MODEL - D_ATT
N_GROUPS = D_SGU // HEAD_DIM
CHUNK = 128
WINDOW_MAX = 2048
RMS_EPS = 1e-6
SCALE = HEAD_DIM ** -0.5
LOG2E = 1.4426950408889634
MASKED = 1e30
LANES = 128
VMEM_LIMIT = 60 * 1024 * 1024

HEADS_PER_STEP = 4
QB = 128

TM_PROJ = 1024
TN_PROJ = 512
TM_IN = 512
TN_IN = 1024
TN_FFN = 256
TM_FFN = 2048
TM_DOWN = 512
TN_DOWN = 512


def _params(n_axes):
    return pltpu.CompilerParams(dimension_semantics=("arbitrary",) * n_axes,
                                vmem_limit_bytes=VMEM_LIMIT)


def _cast_weight(w_ref, wb_ref):
    k = w_ref.shape[0]
    ck = 256
    def body(c, carry):
        r = pl.multiple_of(c * ck, ck)
        wb_ref[pl.ds(r, ck), :] = w_ref[pl.ds(r, ck), :].astype(BF16)
        return carry
    lax.fori_loop(0, k // ck, body, 0)


def _weight_tile_copy(w_hbm, layer, col, stage_ref, sem):
    col = pl.multiple_of(col, LANES)
    return pltpu.make_async_copy(w_hbm.at[layer, :, pl.ds(col, stage_ref.shape[1])], stage_ref, sem)


def _take_weight_tile(w_hbm, layer, col_of, j, nj, stage_ref, sem, wb_ref):
    @pl.when(j == 0)
    def _():
        _weight_tile_copy(w_hbm, layer, col_of(0), stage_ref, sem).start()
    _weight_tile_copy(w_hbm, layer, col_of(j), stage_ref, sem).wait()
    _cast_weight(stage_ref, wb_ref)

    @pl.when(j + 1 < nj)
    def _():
        _weight_tile_copy(w_hbm, layer, col_of(j + 1), stage_ref, sem).start()


def _is_first_row_tile():
    return pl.program_id(1) == 0


def _is_last_row_tile():
    return pl.program_id(1) == pl.num_programs(1) - 1


def _rms_rows_kernel(x_ref, g_ref, o_ref):
    x = x_ref[...]
    ms = jnp.mean(x * x, axis=-1, keepdims=True)
    o_ref[...] = (x * lax.rsqrt(ms + RMS_EPS) * g_ref[...]).astype(o_ref.dtype)


def rms_rows(x, gain):
    m, d = x.shape
    tm = min(m, 512)
    return pl.pallas_call(
        _rms_rows_kernel,
        out_shape=jax.ShapeDtypeStruct((m, d), BF16),
        grid=(m // tm,),
        in_specs=[pl.BlockSpec((tm, d), lambda i: (i, 0)),
                  pl.BlockSpec((1, d), lambda i: (0, 0))],
        out_specs=pl.BlockSpec((tm, d), lambda i: (i, 0)),
        compiler_params=_params(1),
        name="rms_rows",
    )(x, gain.reshape(1, d))


def _head_norm(z, gain):
    parts = []
    for hh in range(z.shape[1] // HEAD_DIM):
        cols = slice(hh * HEAD_DIM, (hh + 1) * HEAD_DIM)
        zh = z[:, cols]
        ms = jnp.mean(zh * zh, axis=-1, keepdims=True)
        parts.append(zh * lax.rsqrt(ms + RMS_EPS) * gain[:, cols])
    return jnp.concatenate(parts, axis=1)


def _inproj_kernel(*refs, norm, want_f32, want_bf16, regroup, has_alias, ni, n_tiles, layer, col0):
    it = iter(refs)
    h_ref = next(it)
    hs_ref = next(it)
    w_ref = next(it)
    gain_ref = next(it) if norm else None
    if has_alias:
        next(it)
    s_ref = next(it)
    f32_ref = next(it) if want_f32 else None
    bf_ref = next(it) if want_bf16 else None
    o4_ref = next(it) if regroup else None
    o16_ref = next(it) if regroup else None
    wb_ref = next(it)
    stage_ref = next(it)
    sem = next(it)
    z_refs = (next(it), next(it))
    z4_ref = next(it) if regroup else None

    s = pl.program_id(0)
    tm = h_ref.shape[0]
    n_slabs = wb_ref.shape[1] // LANES
    active = s < n_tiles

    def finish(z_ref):
        for c in range(n_slabs):
            cols = slice(c * LANES, (c + 1) * LANES)
            z = z_ref[c]
            if norm:
                z = _head_norm(z, gain_ref[:, cols])
            if want_f32:
                if len(f32_ref.shape) == 3:
                    nh = f32_ref.shape[1]
                    f32_ref.reshape(tm * nh, LANES)[pl.ds(c, tm, stride=nh), :] = z
                else:
                    f32_ref[:, cols] = z
            if want_bf16:
                bf_ref[:, cols] = z.astype(BF16)
            if regroup:
                if norm:
                    z_ref[c] = z
                for r in range(4):
                    z4 = z_ref[c, pl.ds(r, tm // 4, stride=4), :]
                    o4_ref[r, :, cols] = z4.astype(BF16)
                    z4_ref[c, r] = z4
                for r in range(4):
                    for c4 in range(4):
                        o16_ref[r + 4 * c4, :, cols] = (
                            z4_ref[c, r, pl.ds(c4, tm // 16, stride=4), :].astype(BF16))

    def multiply(z_ref):
        wide = 2 * LANES
        for c in range(wb_ref.shape[1] // wide):
            z = jnp.dot(h_ref[...], wb_ref[:, c * wide:(c + 1) * wide], preferred_element_type=F32)
            z_ref[2 * c] = z[:, :LANES]
            z_ref[2 * c + 1] = z[:, LANES:]

    @pl.when(s == 0)
    def _():
        z_refs[1][...] = jnp.zeros(z_refs[1].shape, F32)

    @pl.when(active & (s % ni == 0))
    def _():
        tn = wb_ref.shape[1]
        _take_weight_tile(w_ref, layer, lambda j: col0 + j * tn, s // ni, n_tiles // ni,
                          stage_ref, sem, wb_ref)

    for par in range(2):
        @pl.when(active & (s % 2 == par))
        def _():
            finish(z_refs[1 - par])
            multiply(z_refs[par])

    @pl.when(s == n_tiles)
    def _():
        finish(z_refs[(n_tiles - 1) % 2])

    @pl.when(active & (s % ni == ni - 1))
    def _():
        z = jnp.dot(hs_ref[...], wb_ref[...], preferred_element_type=F32)
        if norm:
            z = _head_norm(z, gain_ref[...])
        s_ref[...] = z


def in_proj_section(h, hs, w_in, layer, col0, gain, *, want_f32, want_bf16, regroup=False,
                    seq=None, stack=None, depth=1):
    m, k = h.shape
    ms = hs.shape[0]
    n_sec = D_ATT
    tn, tm = TN_IN, TM_IN
    ni, nj = m // tm, n_sec // tn
    n_tiles = ni * nj
    assert ni >= 2
    norm = gain is not None
    stacked = want_f32 and stack != "unstacked"
    has_alias = stacked and layer > 0

    def cur(s):
        t = jnp.minimum(s, n_tiles - 1)
        return t // ni, t % ni
    def prev(s):
        t = jnp.maximum(s - 1, 0)
        return t // ni, t % ni

    in_specs = [pl.BlockSpec((tm, k), lambda s: (cur(s)[1], 0)),
                pl.BlockSpec((ms, k), lambda s: (0, 0)),
                pl.BlockSpec(memory_space=pl.ANY)]
    args = [h, hs, w_in]
    if norm:
        in_specs.append(pl.BlockSpec((1, tn), lambda s: (0, prev(s)[0])))
        args.append(gain.reshape(1, n_sec))
    if has_alias:
        in_specs.append(pl.BlockSpec(memory_space=pl.ANY))
        args.append(stack)

    out_shape = [jax.ShapeDtypeStruct((ms, n_sec), F32)]
    out_specs = [pl.BlockSpec((ms, tn), lambda s: (0, cur(s)[0]))]
    if want_f32:
        row0 = layer * ni if stacked else 0
        if stacked:
            out_shape.append(jax.ShapeDtypeStruct((depth * m, n_sec // HEAD_DIM, HEAD_DIM), F32))
            out_specs.append(pl.BlockSpec((tm, tn // HEAD_DIM, HEAD_DIM),
                                          lambda s: (row0 + prev(s)[1], prev(s)[0], 0)))
        else:
            out_shape.append(jax.ShapeDtypeStruct((m, n_sec), F32))
            out_specs.append(pl.BlockSpec((tm, tn), lambda s: (prev(s)[1], prev(s)[0])))
    if want_bf16:
        out_shape.append(jax.ShapeDtypeStruct((m, n_sec), BF16))
        out_specs.append(pl.BlockSpec((tm, tn), lambda s: (prev(s)[1], prev(s)[0])))
    if regroup:
        b = m // seq
        tpb = seq // tm
        grouped = lambda s: (prev(s)[1] // tpb, 0, prev(s)[1] % tpb, prev(s)[0])
        out_shape.append(jax.ShapeDtypeStruct((b, 4, seq // 4, n_sec), BF16))
        out_specs.append(pl.BlockSpec((None, 4, tm // 4, tn), grouped))
        out_shape.append(jax.ShapeDtypeStruct((b, 16, seq // 16, n_sec), BF16))
        out_specs.append(pl.BlockSpec((None, 16, tm // 16, tn), grouped))
    scratch = ([pltpu.VMEM((k, tn), BF16), pltpu.VMEM((k, tn), F32), pltpu.SemaphoreType.DMA]
               + [pltpu.VMEM((tn // LANES, tm, LANES), F32)] * 2)
    if regroup:
        scratch.append(pltpu.VMEM((tn // LANES, 4, tm // 4, LANES), F32))

    kern = functools.partial(_inproj_kernel, norm=norm, want_f32=want_f32, want_bf16=want_bf16,
                             regroup=regroup, has_alias=has_alias, ni=ni, n_tiles=n_tiles,
                             layer=layer, col0=col0)
    return pl.pallas_call(
        kern, out_shape=out_shape, grid=(n_tiles + 1,), in_specs=in_specs, out_specs=out_specs,
        scratch_shapes=scratch,
        input_output_aliases=({len(args) - 1: 1} if has_alias else {}),
        compiler_params=_params(1), name="in_proj",
    )(*args)


def _attn_tiles(qkv, bias, prev):
    scores = [lax.dot_general(q, k, (((1,), (1,)), ((), ())), preferred_element_type=F32)
              for q, k, _ in qkv]
    probs = []
    for hh, s in enumerate(scores):
        s = s * (SCALE * LOG2E) - bias[hh]
        m_cur = jnp.max(s, axis=-1, keepdims=True)
        if prev is None:
            probs.append((jnp.exp2(s - m_cur).astype(BF16), jnp.broadcast_to(m_cur, (QB, LANES)), None))
        else:
            m_prev = prev[hh][1]
            m_new = jnp.maximum(m_prev, m_cur)
            p = jnp.exp2(s - jnp.concatenate([m_new] * (s.shape[1] // LANES), axis=1))
            probs.append((p.astype(BF16), m_new, jnp.exp2(m_prev - m_new)))
    new = []
    for hh, (p, m_new, alpha) in enumerate(probs):
        v = qkv[hh][2]
        v_ones = jnp.concatenate([v, jnp.ones(v.shape, BF16)], axis=1)
        pv = jnp.dot(p, v_ones, preferred_element_type=F32)
        if prev is None:
            new.append((pv[:, :LANES], m_new, pv[:, LANES:]))
        else:
            new.append((alpha * prev[hh][0] + pv[:, :LANES], m_new, alpha * prev[hh][2] + pv[:, LANES:]))
    return new


def _window_dist(d, dil):
    return jnp.where((d >= 0) & (d <= 128), (d * dil).astype(F32), MASKED)


def _attn_kernel(sl_ref, q4, k4, v4, q16, k16, v16, o_ref, acc, m_s, l_s, bias_s, nat_s, *, seq):
    g = pl.program_id(1)
    s4 = seq // 4
    state = (acc, m_s, l_s)

    o_i = lax.broadcasted_iota(jnp.int32, (QB, 2 * QB), 0)
    c_i = lax.broadcasted_iota(jnp.int32, (QB, 2 * QB), 1)
    d1 = 4 * ((o_i & 31) - (c_i & 63)) + ((o_i >> 5) - (c_i >> 6))
    heads = range(HEADS_PER_STEP)
    tables = (_window_dist(d1, 1),
              _window_dist(d1 + 128, 1),
              _window_dist(o_i - c_i, 4),
              _window_dist(o_i - c_i + 128, 4),
              _window_dist(o_i - c_i, 16))
    for hh in heads:
        slope = sl_ref[g * HEADS_PER_STEP + hh] * LOG2E
        for t, dist in enumerate(tables):
            bias_s[hh, t] = slope * dist

    def load_state(rows):
        return [tuple(ref[hh, rows, :] for ref in state) for hh in heads]

    def store_state(rows_list, new):
        for hh in heads:
            for ref, val in zip(state, new[hh]):
                n = val.shape[0] // len(rows_list)
                for idx, rows in enumerate(rows_list):
                    ref[hh, rows, :] = val[idx * n:(idx + 1) * n, :]

    def head_cols(hh):
        return slice(hh * HEAD_DIM, (hh + 1) * HEAD_DIM)

    def bias(table, n_keys=2 * QB):
        return [bias_s[hh, table, :, :n_keys] for hh in heads]

    def body1(qb, carry):
        t0 = pl.multiple_of(qb * 32, 32)
        kt0 = pl.multiple_of(jnp.maximum(qb - 1, 0) * 32, 32)
        def slabs(ref, start, n, hh):
            return jnp.concatenate([ref[r, pl.ds(start, n), head_cols(hh)] for r in range(4)], axis=0)
        qkv = [(slabs(q4, t0, 32, hh), slabs(k4, kt0, 64, hh), slabs(v4, kt0, 64, hh)) for hh in heads]
        new = _attn_tiles(qkv, bias(jnp.minimum(qb, 1)), None)
        store_state([pl.ds(r * s4 + t0, 32) for r in range(4)], new)
        return carry
    lax.fori_loop(0, seq // QB, body1, 0, unroll=4)

    n_qb4 = s4 // QB
    assert n_qb4 == 4
    def body4(idx, carry):
        r = idx >> 2
        qb = idx & 3
        q0 = pl.multiple_of(qb * QB, QB)
        k0 = pl.multiple_of(jnp.maximum(qb - 1, 0) * QB, QB)
        rows = pl.ds(r * s4 + q0, QB)
        qkv = [(q4[r, pl.ds(q0, QB), head_cols(hh)], k4[r, pl.ds(k0, 2 * QB), head_cols(hh)],
                v4[r, pl.ds(k0, 2 * QB), head_cols(hh)]) for hh in heads]
        new = _attn_tiles(qkv, bias(2 + jnp.minimum(qb, 1)), load_state(rows))
        store_state([rows], new)
        return carry
    lax.fori_loop(0, 4 * n_qb4, body4, 0, unroll=4)

    def body16(r16, carry):
        rows = pl.ds((r16 & 3) * s4 + (r16 >> 2), QB, stride=4)
        qkv = [(q16[r16, :, head_cols(hh)], k16[r16, :, head_cols(hh)], v16[r16, :, head_cols(hh)])
               for hh in heads]
        new = _attn_tiles(qkv, bias(4, QB), load_state(rows))
        store_state([rows], new)
        return carry
    lax.fori_loop(0, 16, body16, 0, unroll=4)

    tn = nat_s.shape[1] // 4
    def body_out(c, carry):
        t0 = pl.multiple_of(c * tn, tn)
        for hh in range(HEADS_PER_STEP):
            for r in range(4):
                rows = pl.ds(r * s4 + t0, tn)
                nat_s[hh, pl.ds(r, tn, stride=4), :] = acc[hh, rows, :] / l_s[hh, rows, :]
            o_ref[pl.ds(pl.multiple_of(4 * t0, 4 * tn), 4 * tn), head_cols(hh)] = (
                nat_s[hh].astype(o_ref.dtype))
        return carry
    lax.fori_loop(0, s4 // tn, body_out, 0)


def prompt_attention(slopes, qs, ks, vs, batch, seq):
    assert seq // 16 == QB and seq % (4 * QB) == 0
    tw = HEADS_PER_STEP * HEAD_DIM
    r4 = pl.BlockSpec((None, 4, seq // 4, tw), lambda b, g: (b, 0, 0, g))
    r16 = pl.BlockSpec((None, 16, seq // 16, tw), lambda b, g: (b, 0, 0, g))
    state = pltpu.VMEM((HEADS_PER_STEP, seq, LANES), F32)
    return pl.pallas_call(
        functools.partial(_attn_kernel, seq=seq),
        out_shape=jax.ShapeDtypeStruct((batch, seq, D_ATT), BF16),
        grid=(batch, N_HEADS // HEADS_PER_STEP),
        in_specs=[pl.BlockSpec(memory_space=pltpu.SMEM), r4, r4, r4, r16, r16, r16],
        out_specs=pl.BlockSpec((None, seq, tw), lambda b, g: (b, 0, g)),
        scratch_shapes=[state, state, state,
                        pltpu.VMEM((HEADS_PER_STEP, 5, QB, 2 * QB), F32),
                        pltpu.VMEM((HEADS_PER_STEP, 256, LANES), F32)],
        compiler_params=_params(2), name="prompt_attention",
    )(slopes, qs[0], ks[0], vs[0], qs[1], ks[1], vs[1])


def _sgu_kernel(g_ref, u_ref, w_ref, bt_ref, o_ref, wt_ref):
    @pl.when(pl.program_id(0) == 0)
    def _():
        r_i = lax.broadcasted_iota(jnp.int32, (CHUNK, CHUNK), 0)
        c_i = lax.broadcasted_iota(jnp.int32, (CHUNK, CHUNK), 1)
        for gi in range(N_GROUPS):
            wt_ref[gi] = jnp.where(r_i >= c_i, w_ref[gi], 0.0).astype(BF16)

    n_chunks = g_ref.shape[0] // CHUNK
    for gi in range(N_GROUPS):
        cols = slice(gi * HEAD_DIM, (gi + 1) * HEAD_DIM)
        g_wide = jnp.concatenate([g_ref[c * CHUNK:(c + 1) * CHUNK, cols] for c in range(n_chunks)],
                                 axis=1)
        gate = jnp.dot(wt_ref[gi], g_wide, preferred_element_type=F32)
        for c in range(n_chunks):
            rows = slice(c * CHUNK, (c + 1) * CHUNK)
            gate_c = gate[:, c * HEAD_DIM:(c + 1) * HEAD_DIM] + bt_ref[:, gi:gi + 1]
            o_ref[rows, cols] = (u_ref[rows, cols] * gate_c).astype(BF16)


SGU_CHUNKS_PER_STEP = 4


def prompt_sgu(g, u, w_spatial, b_t, layer):
    m = g.shape[0]
    tm = SGU_CHUNKS_PER_STEP * CHUNK
    row = pl.BlockSpec((tm, D_SGU), lambda c: (c, 0))
    return pl.pallas_call(
        _sgu_kernel,
        out_shape=jax.ShapeDtypeStruct((m, D_SGU), BF16),
        grid=(m // tm,),
        in_specs=[row, row,
                  pl.BlockSpec((None, N_GROUPS, CHUNK, CHUNK), lambda c: (layer, 0, 0, 0)),
                  pl.BlockSpec((None, CHUNK, N_GROUPS), lambda c: (layer, 0, 0))],
        out_specs=row,
        scratch_shapes=[pltpu.VMEM((N_GROUPS, CHUNK, CHUNK), BF16)],
        compiler_params=_params(1), name="prompt_sgu",
    )(g, u, w_spatial, b_t)


def _outproj_kernel(a_ref, ug_ref, as_ref, ugs_ref, w_ref, x_ref, xs_ref, o_ref, os_ref,
                    wb_ref, stage_ref, sem, *, layer):
    @pl.when(_is_first_row_tile())
    def _():
        tn = wb_ref.shape[1]
        _take_weight_tile(w_ref, layer, lambda j: j * tn, pl.program_id(0), pl.num_programs(0),
                          stage_ref, sem, wb_ref)

    def project(a, ug):
        z = jnp.dot(a, wb_ref[:D_ATT, :], preferred_element_type=F32)
        return z + jnp.dot(ug, wb_ref[D_ATT:, :], preferred_element_type=F32)

    o_ref[...] = x_ref[...] + project(a_ref[...], ug_ref[...])

    @pl.when(_is_last_row_tile())
    def _():
        os_ref[...] = xs_ref[...] + project(as_ref[...], ugs_ref[...])


def out_proj(att, ug, att_s, ug_s, w_out, x, xs, layer):
    m, ms = x.shape[0], xs.shape[0]
    tn, tm = TN_IN, TM_IN
    lhs = pl.BlockSpec((tm, D_ATT), lambda j, i: (i, 0))
    lhs_s = pl.BlockSpec((ms, D_ATT), lambda j, i: (0, 0))
    res = pl.BlockSpec((tm, tn), lambda j, i: (i, j))
    res_s = pl.BlockSpec((ms, tn), lambda j, i: (0, j))
    return pl.pallas_call(
        functools.partial(_outproj_kernel, layer=layer),
        out_shape=[jax.ShapeDtypeStruct((m, D_MODEL), F32), jax.ShapeDtypeStruct((ms, D_MODEL), F32)],
        grid=(D_MODEL // tn, m // tm),
        in_specs=[lhs, lhs, lhs_s, lhs_s, pl.BlockSpec(memory_space=pl.ANY), res, res_s],
        out_specs=[res, res_s],
        scratch_shapes=[pltpu.VMEM((D_MODEL, tn), BF16), pltpu.VMEM((D_MODEL, tn), F32),
                        pltpu.SemaphoreType.DMA],
        compiler_params=_params(2), name="out_proj",
    )(att, ug, att_s, ug_s, w_out, x, xs)


def _ffn1_kernel(h_ref, hs_ref, wg_ref, wu_ref, wd_ref, a_ref, as_ref, wdb_ref,
                 wgb_ref, wub_ref, gstage_ref, ustage_ref, sems, *, layer):
    wdb_ref[...] = wd_ref[...].astype(BF16)

    @pl.when(_is_first_row_tile())
    def _():
        tn = wgb_ref.shape[1]
        j, nj = pl.program_id(0), pl.num_programs(0)
        _take_weight_tile(wg_ref, layer, lambda jj: jj * tn, j, nj, gstage_ref, sems.at[0], wgb_ref)
        _take_weight_tile(wu_ref, layer, lambda jj: jj * tn, j, nj, ustage_ref, sems.at[1], wub_ref)

    def gated(h):
        g = jnp.dot(h, wgb_ref[...], preferred_element_type=F32)
        u = jnp.dot(h, wub_ref[...], preferred_element_type=F32)
        return (g / (1.0 + jnp.exp(-g)) * u).astype(BF16)

    a_ref[...] = gated(h_ref[...])

    @pl.when(_is_last_row_tile())
    def _():
        as_ref[...] = gated(hs_ref[...])


def ffn_gate_up(h, hs, w_gate, w_up, w_down, layer):
    m, k = h.shape
    ms = hs.shape[0]
    d_ff = w_gate.shape[2]
    tn, tm = TN_FFN, TM_FFN
    nj, ni = d_ff // tn, m // tm
    rb = d_ff // (nj * ni)
    assert rb * nj * ni == d_ff and rb % 16 == 0
    wspec = pl.BlockSpec(memory_space=pl.ANY)
    return pl.pallas_call(
        functools.partial(_ffn1_kernel, layer=layer),
        out_shape=[jax.ShapeDtypeStruct((m, d_ff), BF16), jax.ShapeDtypeStruct((ms, d_ff), BF16),
                   jax.ShapeDtypeStruct((d_ff, D_MODEL), BF16)],
        grid=(nj, ni),
        in_specs=[pl.BlockSpec((tm, k), lambda j, i: (i, 0)),
                  pl.BlockSpec((ms, k), lambda j, i: (0, 0)), wspec, wspec,
                  pl.BlockSpec((None, rb, D_MODEL), lambda j, i: (layer, j * ni + i, 0))],
        out_specs=[pl.BlockSpec((tm, tn), lambda j, i: (i, j)),
                   pl.BlockSpec((ms, tn), lambda j, i: (0, j)),
                   pl.BlockSpec((rb, D_MODEL), lambda j, i: (j * ni + i, 0))],
        scratch_shapes=[pltpu.VMEM((k, tn), BF16)] * 2 + [pltpu.VMEM((k, tn), F32)] * 2
                       + [pltpu.SemaphoreType.DMA((2,))],
        compiler_params=_params(2), name="ffn_gate_up",
    )(h, hs, w_gate, w_up, w_down)


def _ffn2_kernel(a_ref, as_ref, w_ref, x_ref, xs_ref, o_ref, os_ref):
    o_ref[...] = x_ref[...] + jnp.dot(a_ref[...], w_ref[...], preferred_element_type=F32)

    @pl.when(_is_last_row_tile())
    def _():
        os_ref[...] = xs_ref[...] + jnp.dot(as_ref[...], w_ref[...], preferred_element_type=F32)


def ffn_down(a, a_s, w_down_bf16, x, xs):
    m, k = a.shape
    ms = a_s.shape[0]
    tn, tm = TN_DOWN, TM_DOWN
    res = pl.BlockSpec((tm, tn), lambda j, i: (i, j))
    res_s = pl.BlockSpec((ms, tn), lambda j, i: (0, j))
    return pl.pallas_call(
        _ffn2_kernel,
        out_shape=[jax.ShapeDtypeStruct((m, D_MODEL), F32), jax.ShapeDtypeStruct((ms, D_MODEL), F32)],
        grid=(D_MODEL // tn, m // tm),
        in_specs=[pl.BlockSpec((tm, k), lambda j, i: (i, 0)),
                  pl.BlockSpec((ms, k), lambda j, i: (0, 0)),
                  pl.BlockSpec((k, tn), lambda j, i: (0, j)),
                  res, res_s],
        out_specs=[res, res_s],
        compiler_params=_params(2), name="ffn_down",
    )(a, a_s, w_down_bf16, x, xs)


SQ = 16


def _multiplicity(d):
    nonneg = d >= 0
    n = (nonneg & (d <= 128)).astype(F32)
    n = n + (nonneg & (d <= 512) & ((d & 3) == 0)).astype(F32)
    n = n + (nonneg & (d <= 2048) & ((d & 15) == 0)).astype(F32)
    return n


def _sattn_kernel(sl_ref, q_ref, kn_ref, vn_ref, kc_ref, vc_ref, o_ref, acc, m_s, l_s, xk_ref, xv_ref,
                  *, tc, wb, ts):
    c = pl.program_id(1)
    last = pl.num_programs(1) - 1

    @pl.when(c == 0)
    def _():
        m_s[...] = jnp.full(m_s.shape, -MASKED, F32)
        l_s[...] = jnp.zeros(l_s.shape, F32)
        acc[...] = jnp.zeros(acc.shape, F32)

    def pad_rows(x, n):
        return jnp.concatenate([x, jnp.zeros((n - x.shape[0], x.shape[1]), F32)], axis=0)

    def head_cols(h):
        return slice(h * HEAD_DIM, (h + 1) * HEAD_DIM)

    def query(h):
        return pad_rows(q_ref[:, head_cols(h)], SQ).astype(BF16)

    def update(key, value, d):
        mult = _multiplicity(d)
        dist = jnp.where(mult > 0.0, d.astype(F32), MASKED)
        n_rep = d.shape[1] // LANES
        scores = [lax.dot_general(query(h), key(h), (((1,), (1,)), ((), ())),
                                  preferred_element_type=F32) for h in range(N_HEADS)]
        staged = []
        for h, s in enumerate(scores):
            s = s * SCALE - sl_ref[h] * dist
            m_prev = m_s[h]
            m_new = jnp.maximum(m_prev, jnp.max(s, axis=-1, keepdims=True))
            p = mult * jnp.exp(s - jnp.concatenate([m_new] * n_rep, axis=1))
            staged.append((p, m_new, jnp.exp(m_prev - m_new)))
        for h, (p, m_new, alpha) in enumerate(staged):
            l_s[h] = alpha * l_s[h] + jnp.sum(p, axis=-1, keepdims=True)
            acc[h] = alpha * acc[h] + jnp.dot(p.astype(BF16), value(h), preferred_element_type=F32)
            m_s[h] = m_new

    t_c = lax.broadcasted_iota(jnp.int32, (SQ, tc), 0)
    c_c = lax.broadcasted_iota(jnp.int32, (SQ, tc), 1)
    d_cache = (wb + t_c) - (c * tc + c_c)
    for r in range(4):
        xk_ref[r] = kc_ref[pl.ds(r, 4 * tc, stride=4), :]
        xv_ref[r] = vc_ref[pl.ds(r, 4 * tc, stride=4), :]
    update(lambda h: xk_ref[h % 4, pl.ds(h // 4, tc, stride=4), :].astype(BF16),
           lambda h: xv_ref[h % 4, pl.ds(h // 4, tc, stride=4), :].astype(BF16), d_cache)

    @pl.when(c == last)
    def _():
        t_n = lax.broadcasted_iota(jnp.int32, (SQ, LANES), 0)
        c_n = lax.broadcasted_iota(jnp.int32, (SQ, LANES), 1)
        d_new = jnp.where(c_n < ts, t_n - c_n, -1)
        update(lambda h: pad_rows(kn_ref[:, head_cols(h)], LANES).astype(BF16),
               lambda h: pad_rows(vn_ref[:, head_cols(h)], LANES).astype(BF16), d_new)
        for h in range(N_HEADS):
            o_ref[:, head_cols(h)] = (acc[h] / l_s[h])[:ts, :]


def sample_attention(slopes, q, k_new, v_new, cache_k, cache_v, layer):
    bs, ts, _ = q.shape
    wb = cache_k.shape[2] // N_HEADS
    tc = 512
    assert wb % tc == 0 and ts <= SQ and wb == WINDOW_MAX
    new = pl.BlockSpec((None, ts, D_ATT), lambda b, c: (b, 0, 0))
    cache = pl.BlockSpec((None, None, tc * N_HEADS, HEAD_DIM), lambda b, c: (layer, b, c, 0))
    return pl.pallas_call(
        functools.partial(_sattn_kernel, tc=tc, wb=wb, ts=ts),
        out_shape=jax.ShapeDtypeStruct((bs, ts, D_ATT), F32),
        grid=(bs, wb // tc),
        in_specs=[pl.BlockSpec(memory_space=pltpu.SMEM), new, new, new, cache, cache],
        out_specs=new,
        scratch_shapes=[pltpu.VMEM((N_HEADS, SQ, LANES), F32)] * 3
                       + [pltpu.VMEM((4, 4 * tc, HEAD_DIM), F32)] * 2,
        compiler_params=_params(2), name="sample_attention",
    )(slopes, q, k_new, v_new, cache_k, cache_v)


def _ssgu_kernel(att_ref, g_ref, u_ref, wt_ref, b_ref, att_o, ug_o, *, ts):
    att_o[...] = att_ref[...].astype(BF16)
    n = att_ref.shape[0]
    r_i = lax.broadcasted_iota(jnp.int32, (n, n), 0)
    c_i = lax.broadcasted_iota(jnp.int32, (n, n), 1)
    keep = ((r_i & -ts) == (c_i & -ts)) & ((c_i & (ts - 1)) <= (r_i & (ts - 1)))
    for gi in range(N_GROUPS):
        cols = slice(gi * HEAD_DIM, (gi + 1) * HEAD_DIM)
        w = jnp.where(keep, wt_ref[gi], 0.0).astype(BF16)
        gate = jnp.dot(w, g_ref[:, cols].astype(BF16), preferred_element_type=F32) + b_ref[:, gi:gi + 1]
        ug_o[:, cols] = (u_ref[:, cols] * gate).astype(BF16)


def sample_sgu(att, g, u, w_tiled, b_rows, ts):
    n = att.shape[0]
    return pl.pallas_call(
        functools.partial(_ssgu_kernel, ts=ts),
        out_shape=[jax.ShapeDtypeStruct((n, D_ATT), BF16), jax.ShapeDtypeStruct((n, D_SGU), BF16)],
        compiler_params=pltpu.CompilerParams(vmem_limit_bytes=VMEM_LIMIT),
        name="sample_sgu",
    )(att, g, u, w_tiled, b_rows)


def kernel(x_prompt, x_sample, cache_k_win, cache_v_win, norm1, w_in, q_gain, k_gain, sgu_gain,
           w_spatial, b_spatial, w_out, norm2, w_gate, w_up, w_down):
    bp, sp, _ = x_prompt.shape
    bs, ts, _ = x_sample.shape
    assert ts & (ts - 1) == 0
    depth = w_in.shape[0]
    wb = cache_k_win.shape[2]
    assert sp == WINDOW_MAX and ts <= 8 and wb == WINDOW_MAX
    mp, ms = bp * sp, bs * ts

    xp = x_prompt.reshape(mp, D_MODEL)
    xs = x_sample.reshape(ms, D_MODEL)
    cache_k = cache_k_win.reshape(depth, bs, wb * N_HEADS, HEAD_DIM)
    cache_v = cache_v_win.reshape(depth, bs, wb * N_HEADS, HEAD_DIM)
    slopes = jnp.exp2(-8.0 * jnp.arange(1, N_HEADS + 1, dtype=F32) / N_HEADS)
    b_t = jnp.swapaxes(b_spatial, 1, 2)

    k_stack = v_stack = None
    ks_rows, vs_rows, gs_rows = [], [], []
    for l in range(depth):
        qg = jnp.tile(q_gain[l], N_HEADS)
        kg = jnp.tile(k_gain[l], N_HEADS)
        sg = sgu_gain[l].reshape(D_SGU)

        hp = rms_rows(xp, norm1[l])
        hs = rms_rows(xs, norm1[l])
        sec = functools.partial(in_proj_section, hp, hs, w_in, l, seq=sp, depth=depth)
        q_s, *q_p = sec(0 * D_ATT, qg, want_f32=False, want_bf16=False, regroup=True)
        k_s, k_stack, *k_p = sec(1 * D_ATT, kg, want_f32=True, want_bf16=False, regroup=True,
                                 stack=k_stack)
        v_s, v_stack, *v_p = sec(2 * D_ATT, None, want_f32=True, want_bf16=False, regroup=True,
                                 stack=v_stack)
        u_s, u_p = sec(3 * D_ATT, None, want_f32=True, want_bf16=False, stack="unstacked")
        g_s, g_p = sec(3 * D_ATT + D_SGU, sg, want_f32=False, want_bf16=True)

        att_p = prompt_attention(slopes, q_p, k_p, v_p, bp, sp).reshape(mp, D_ATT)
        ug_p = prompt_sgu(g_p, u_p, w_spatial, b_t, l)

        r3 = lambda a: a.reshape(bs, ts, D_ATT)
        att_s = sample_attention(slopes, r3(q_s), r3(k_s), r3(v_s), cache_k, cache_v, l)
        w_tiled = jnp.tile(w_spatial[l][:, :ts, :ts], (1, bs, bs))
        b_rows = jnp.tile(b_spatial[l][:, :ts].T, (bs, 1))
        att_sb, ug_s = sample_sgu(att_s.reshape(ms, D_ATT), g_s, u_s, w_tiled, b_rows, ts)

        xp, xs = out_proj(att_p, ug_p, att_sb, ug_s, w_out, xp, xs, l)
        a_p, a_s, w_down_bf16 = ffn_gate_up(rms_rows(xp, norm2[l]), rms_rows(xs, norm2[l]),
                                            w_gate, w_up, w_down, l)
        xp, xs = ffn_down(a_p, a_s, w_down_bf16, xp, xs)

        ks_rows.append(k_s)
        vs_rows.append(v_s)
        gs_rows.append(g_s)

    heads = lambda a, b, t: a.reshape(depth, b, t, N_HEADS, HEAD_DIM)
    return (xp.reshape(bp, sp, D_MODEL), xs.reshape(bs, ts, D_MODEL),
            heads(k_stack, bp, sp), heads(v_stack, bp, sp),
            heads(jnp.stack(ks_rows), bs, ts), heads(jnp.stack(vs_rows), bs, ts),
            heads(jnp.stack(gs_rows), bs, ts))
```

```python
import functools

import jax
import jax.numpy as jnp
from jax import lax
from jax.experimental import pallas as pl
from jax.experimental.pallas import tpu as pltpu

F32 = jnp.float32
BF16 = jnp.bfloat16

D_MODEL = 4096
HEAD_DIM = 128
N_HEADS = 16
D_ATT = N_HEADS * HEAD_DIM
D_SGU = D_MODEL - D_ATT
N_GROUPS = D_SGU // HEAD_DIM
CHUNK = 128
WINDOW_MAX = 2048
RMS_EPS = 1e-6
SCALE = HEAD_DIM ** -0.5
LOG2E = 1.4426950408889634
MASKED = 1e30
LANES = 128
VMEM_LIMIT = 60 * 1024 * 1024

HEADS_PER_STEP = 4
QB = 128

TM_PROJ = 1024
TN_PROJ = 512
TM_IN = 512
TN_IN = 1024
TN_FFN = 256
TM_FFN = 2048
TM_DOWN = 512
TN_DOWN = 512


def _params(n_axes):
    return pltpu.CompilerParams(dimension_semantics=("arbitrary",) * n_axes,
                                vmem_limit_bytes=VMEM_LIMIT)


def _cast_weight(w_ref, wb_ref):
    k = w_ref.shape[0]
    ck = 256
    def body(c, carry):
        r = pl.multiple_of(c * ck, ck)
        wb_ref[pl.ds(r, ck), :] = w_ref[pl.ds(r, ck), :].astype(BF16)
        return carry
    lax.fori_loop(0, k // ck, body, 0)


def _weight_tile_copy(w_hbm, layer, col, stage_ref, sem):
    col = pl.multiple_of(col, LANES)
    return pltpu.make_async_copy(w_hbm.at[layer, :, pl.ds(col, stage_ref.shape[1])], stage_ref, sem)


def _take_weight_tile(w_hbm, layer, col_of, j, nj, stage_ref, sem, wb_ref):
    @pl.when(j == 0)
    def _():
        _weight_tile_copy(w_hbm, layer, col_of(0), stage_ref, sem).start()
    _weight_tile_copy(w_hbm, layer, col_of(j), stage_ref, sem).wait()
    _cast_weight(stage_ref, wb_ref)

    @pl.when(j + 1 < nj)
    def _():
        _weight_tile_copy(w_hbm, layer, col_of(j + 1), stage_ref, sem).start()


def _is_first_row_tile():
    return pl.program_id(1) == 0


def _is_last_row_tile():
    return pl.program_id(1) == pl.num_programs(1) - 1


def _rms_rows_kernel(x_ref, g_ref, o_ref):
    x = x_ref[...]
    ms = jnp.mean(x * x, axis=-1, keepdims=True)
    o_ref[...] = (x * lax.rsqrt(ms + RMS_EPS) * g_ref[...]).astype(o_ref.dtype)


def rms_rows(x, gain):
    m, d = x.shape
    tm = min(m, 512)
    return pl.pallas_call(
        _rms_rows_kernel,
        out_shape=jax.ShapeDtypeStruct((m, d), BF16),
        grid=(m // tm,),
        in_specs=[pl.BlockSpec((tm, d), lambda i: (i, 0)),
                  pl.BlockSpec((1, d), lambda i: (0, 0))],
        out_specs=pl.BlockSpec((tm, d), lambda i: (i, 0)),
        compiler_params=_params(1),
        name="rms_rows",
    )(x, gain.reshape(1, d))


def _head_norm(z, gain):
    parts = []
    for hh in range(z.shape[1] // HEAD_DIM):
        cols = slice(hh * HEAD_DIM, (hh + 1) * HEAD_DIM)
        zh = z[:, cols]
        ms = jnp.mean(zh * zh, axis=-1, keepdims=True)
        parts.append(zh * lax.rsqrt(ms + RMS_EPS) * gain[:, cols])
    return jnp.concatenate(parts, axis=1)


def _inproj_kernel(*refs, norm, want_f32, want_bf16, regroup, has_alias, ni, n_tiles, layer, col0):
    it = iter(refs)
    h_ref = next(it)
    hs_ref = next(it)
    w_ref = next(it)
    gain_ref = next(it) if norm else None
    if has_alias:
        next(it)
    s_ref = next(it)
    f32_ref = next(it) if want_f32 else None
    bf_ref = next(it) if want_bf16 else None
    o4_ref = next(it) if regroup else None
    o16_ref = next(it) if regroup else None
    wb_ref = next(it)
    stage_ref = next(it)
    sem = next(it)
    z_refs = (next(it), next(it))
    z4_ref = next(it) if regroup else None

    s = pl.program_id(0)
    tm = h_ref.shape[0]
    n_slabs = wb_ref.shape[1] // LANES
    active = s < n_tiles

    def finish(z_ref):
        for c in range(n_slabs):
            cols = slice(c * LANES, (c + 1) * LANES)
            z = z_ref[c]
            if norm:
                z = _head_norm(z, gain_ref[:, cols])
            if want_f32:
                if len(f32_ref.shape) == 3:
                    nh = f32_ref.shape[1]
                    f32_ref.reshape(tm * nh, LANES)[pl.ds(c, tm, stride=nh), :] = z
                else:
                    f32_ref[:, cols] = z
            if want_bf16:
                bf_ref[:, cols] = z.astype(BF16)
            if regroup:
                if norm:
                    z_ref[c] = z
                for r in range(4):
                    z4 = z_ref[c, pl.ds(r, tm // 4, stride=4), :]
                    o4_ref[r, :, cols] = z4.astype(BF16)
                    z4_ref[c, r] = z4
                for r in range(4):
                    for c4 in range(4):
                        o16_ref[r + 4 * c4, :, cols] = (
                            z4_ref[c, r, pl.ds(c4, tm // 16, stride=4), :].astype(BF16))

    def multiply(z_ref):
        wide = 2 * LANES
        for c in range(wb_ref.shape[1] // wide):
            z = jnp.dot(h_ref[...], wb_ref[:, c * wide:(c + 1) * wide], preferred_element_type=F32)
            z_ref[2 * c] = z[:, :LANES]
            z_ref[2 * c + 1] = z[:, LANES:]

    @pl.when(s == 0)
    def _():
        z_refs[1][...] = jnp.zeros(z_refs[1].shape, F32)

    @pl.when(active & (s % ni == 0))
    def _():
        tn = wb_ref.shape[1]
        _take_weight_tile(w_ref, layer, lambda j: col0 + j * tn, s // ni, n_tiles // ni,
                          stage_ref, sem, wb_ref)

    for par in range(2):
        @pl.when(active & (s % 2 == par))
        def _():
            finish(z_refs[1 - par])
            multiply(z_refs[par])

    @pl.when(s == n_tiles)
    def _():
        finish(z_refs[(n_tiles - 1) % 2])

    @pl.when(active & (s % ni == ni - 1))
    def _():
        z = jnp.dot(hs_ref[...], wb_ref[...], preferred_element_type=F32)
        if norm:
            z = _head_norm(z, gain_ref[...])
        s_ref[...] = z


def in_proj_section(h, hs, w_in, layer, col0, gain, *, want_f32, want_bf16, regroup=False,
                    seq=None, stack=None, depth=1):
    m, k = h.shape
    ms = hs.shape[0]
    n_sec = D_ATT
    tn, tm = TN_IN, TM_IN
    ni, nj = m // tm, n_sec // tn
    n_tiles = ni * nj
    assert ni >= 2
    norm = gain is not None
    stacked = want_f32 and stack != "unstacked"
    has_alias = stacked and layer > 0

    def cur(s):
        t = jnp.minimum(s, n_tiles - 1)
        return t // ni, t % ni
    def prev(s):
        t = jnp.maximum(s - 1, 0)
        return t // ni, t % ni

    in_specs = [pl.BlockSpec((tm, k), lambda s: (cur(s)[1], 0)),
                pl.BlockSpec((ms, k), lambda s: (0, 0)),
                pl.BlockSpec(memory_space=pl.ANY)]
    args = [h, hs, w_in]
    if norm:
        in_specs.append(pl.BlockSpec((1, tn), lambda s: (0, prev(s)[0])))
        args.append(gain.reshape(1, n_sec))
    if has_alias:
        in_specs.append(pl.BlockSpec(memory_space=pl.ANY))
        args.append(stack)

    out_shape = [jax.ShapeDtypeStruct((ms, n_sec), F32)]
    out_specs = [pl.BlockSpec((ms, tn), lambda s: (0, cur(s)[0]))]
    if want_f32:
        row0 = layer * ni if stacked else 0
        if stacked:
            out_shape.append(jax.ShapeDtypeStruct((depth * m, n_sec // HEAD_DIM, HEAD_DIM), F32))
            out_specs.append(pl.BlockSpec((tm, tn // HEAD_DIM, HEAD_DIM),
                                          lambda s: (row0 + prev(s)[1], prev(s)[0], 0)))
        else:
            out_shape.append(jax.ShapeDtypeStruct((m, n_sec), F32))
            out_specs.append(pl.BlockSpec((tm, tn), lambda s: (prev(s)[1], prev(s)[0])))
    if want_bf16:
        out_shape.append(jax.ShapeDtypeStruct((m, n_sec), BF16))
        out_specs.append(pl.BlockSpec((tm, tn), lambda s: (prev(s)[1], prev(s)[0])))
    if regroup:
        b = m // seq
        tpb = seq // tm
        grouped = lambda s: (prev(s)[1] // tpb, 0, prev(s)[1] % tpb, prev(s)[0])
        out_shape.append(jax.ShapeDtypeStruct((b, 4, seq // 4, n_sec), BF16))
        out_specs.append(pl.BlockSpec((None, 4, tm // 4, tn), grouped))
        out_shape.append(jax.ShapeDtypeStruct((b, 16, seq // 16, n_sec), BF16))
        out_specs.append(pl.BlockSpec((None, 16, tm // 16, tn), grouped))
    scratch = ([pltpu.VMEM((k, tn), BF16), pltpu.VMEM((k, tn), F32), pltpu.SemaphoreType.DMA]
               + [pltpu.VMEM((tn // LANES, tm, LANES), F32)] * 2)
    if regroup:
        scratch.append(pltpu.VMEM((tn // LANES, 4, tm // 4, LANES), F32))

    kern = functools.partial(_inproj_kernel, norm=norm, want_f32=want_f32, want_bf16=want_bf16,
                             regroup=regroup, has_alias=has_alias, ni=ni, n_tiles=n_tiles,
                             layer=layer, col0=col0)
    return pl.pallas_call(
        kern, out_shape=out_shape, grid=(n_tiles + 1,), in_specs=in_specs, out_specs=out_specs,
        scratch_shapes=scratch,
        input_output_aliases=({len(args) - 1: 1} if has_alias else {}),
        compiler_params=_params(1), name="in_proj",
    )(*args)


def _attn_tiles(qkv, bias, prev):
    scores = [lax.dot_general(q, k, (((1,), (1,)), ((), ())), preferred_element_type=F32)
              for q, k, _ in qkv]
    probs = []
    for hh, s in enumerate(scores):
        s = s * (SCALE * LOG2E) - bias[hh]
        m_cur = jnp.max(s, axis=-1, keepdims=True)
        if prev is None:
            probs.append((jnp.exp2(s - m_cur).astype(BF16), jnp.broadcast_to(m_cur, (QB, LANES)), None))
        else:
            m_prev = prev[hh][1]
            m_new = jnp.maximum(m_prev, m_cur)
            p = jnp.exp2(s - jnp.concatenate([m_new] * (s.shape[1] // LANES), axis=1))
            probs.append((p.astype(BF16), m_new, jnp.exp2(m_prev - m_new)))
    new = []
    for hh, (p, m_new, alpha) in enumerate(probs):
        v = qkv[hh][2]
        v_ones = jnp.concatenate([v, jnp.ones(v.shape, BF16)], axis=1)
        pv = jnp.dot(p, v_ones, preferred_element_type=F32)
        if prev is None:
            new.append((pv[:, :LANES], m_new, pv[:, LANES:]))
        else:
            new.append((alpha * prev[hh][0] + pv[:, :LANES], m_new, alpha * prev[hh][2] + pv[:, LANES:]))
    return new


def _window_dist(d, dil):
    return jnp.where((d >= 0) & (d <= 128), (d * dil).astype(F32), MASKED)


def _attn_kernel(sl_ref, q4, k4, v4, q16, k16, v16, o_ref, acc, m_s, l_s, bias_s, nat_s, *, seq):
    g = pl.program_id(1)
    s4 = seq // 4
    state = (acc, m_s, l_s)

    o_i = lax.broadcasted_iota(jnp.int32, (QB, 2 * QB), 0)
    c_i = lax.broadcasted_iota(jnp.int32, (QB, 2 * QB), 1)
    d1 = 4 * ((o_i & 31) - (c_i & 63)) + ((o_i >> 5) - (c_i >> 6))
    heads = range(HEADS_PER_STEP)
    tables = (_window_dist(d1, 1),
              _window_dist(d1 + 128, 1),
              _window_dist(o_i - c_i, 4),
              _window_dist(o_i - c_i + 128, 4),
              _window_dist(o_i - c_i, 16))
    for hh in heads:
        slope = sl_ref[g * HEADS_PER_STEP + hh] * LOG2E
        for t, dist in enumerate(tables):
            bias_s[hh, t] = slope * dist

    def load_state(rows):
        return [tuple(ref[hh, rows, :] for ref in state) for hh in heads]

    def store_state(rows_list, new):
        for hh in heads:
            for ref, val in zip(state, new[hh]):
                n = val.shape[0] // len(rows_list)
                for idx, rows in enumerate(rows_list):
                    ref[hh, rows, :] = val[idx * n:(idx + 1) * n, :]

    def head_cols(hh):
        return slice(hh * HEAD_DIM, (hh + 1) * HEAD_DIM)

    def bias(table, n_keys=2 * QB):
        return [bias_s[hh, table, :, :n_keys] for hh in heads]

    def body1(qb, carry):
        t0 = pl.multiple_of(qb * 32, 32)
        kt0 = pl.multiple_of(jnp.maximum(qb - 1, 0) * 32, 32)
        def slabs(ref, start, n, hh):
            return jnp.concatenate([ref[r, pl.ds(start, n), head_cols(hh)] for r in range(4)], axis=0)
        qkv = [(slabs(q4, t0, 32, hh), slabs(k4, kt0, 64, hh), slabs(v4, kt0, 64, hh)) for hh in heads]
        new = _attn_tiles(qkv, bias(jnp.minimum(qb, 1)), None)
        store_state([pl.ds(r * s4 + t0, 32) for r in range(4)], new)
        return carry
    lax.fori_loop(0, seq // QB, body1, 0, unroll=4)

    n_qb4 = s4 // QB
    assert n_qb4 == 4
    def body4(idx, carry):
        r = idx >> 2
        qb = idx & 3
        q0 = pl.multiple_of(qb * QB, QB)
        k0 = pl.multiple_of(jnp.maximum(qb - 1, 0) * QB, QB)
        rows = pl.ds(r * s4 + q0, QB)
        qkv = [(q4[r, pl.ds(q0, QB), head_cols(hh)], k4[r, pl.ds(k0, 2 * QB), head_cols(hh)],
                v4[r, pl.ds(k0, 2 * QB), head_cols(hh)]) for hh in heads]
        new = _attn_tiles(qkv, bias(2 + jnp.minimum(qb, 1)), load_state(rows))
        store_state([rows], new)
        return carry
    lax.fori_loop(0, 4 * n_qb4, body4, 0, unroll=4)

    def body16(r16, carry):
        rows = pl.ds((r16 & 3) * s4 + (r16 >> 2), QB, stride=4)
        qkv = [(q16[r16, :, head_cols(hh)], k16[r16, :, head_cols(hh)], v16[r16, :, head_cols(hh)])
               for hh in heads]
        new = _attn_tiles(qkv, bias(4, QB), load_state(rows))
        store_state([rows], new)
        return carry
    lax.fori_loop(0, 16, body16, 0, unroll=4)

    tn = nat_s.shape[1] // 4
    def body_out(c, carry):
        t0 = pl.multiple_of(c * tn, tn)
        for hh in range(HEADS_PER_STEP):
            for r in range(4):
                rows = pl.ds(r * s4 + t0, tn)
                nat_s[hh, pl.ds(r, tn, stride=4), :] = acc[hh, rows, :] / l_s[hh, rows, :]
            o_ref[pl.ds(pl.multiple_of(4 * t0, 4 * tn), 4 * tn), head_cols(hh)] = (
                nat_s[hh].astype(o_ref.dtype))
        return carry
    lax.fori_loop(0, s4 // tn, body_out, 0)


def prompt_attention(slopes, qs, ks, vs, batch, seq):
    assert seq // 16 == QB and seq % (4 * QB) == 0
    tw = HEADS_PER_STEP * HEAD_DIM
    r4 = pl.BlockSpec((None, 4, seq // 4, tw), lambda b, g: (b, 0, 0, g))
    r16 = pl.BlockSpec((None, 16, seq // 16, tw), lambda b, g: (b, 0, 0, g))
    state = pltpu.VMEM((HEADS_PER_STEP, seq, LANES), F32)
    return pl.pallas_call(
        functools.partial(_attn_kernel, seq=seq),
        out_shape=jax.ShapeDtypeStruct((batch, seq, D_ATT), BF16),
        grid=(batch, N_HEADS // HEADS_PER_STEP),
        in_specs=[pl.BlockSpec(memory_space=pltpu.SMEM), r4, r4, r4, r16, r16, r16],
        out_specs=pl.BlockSpec((None, seq, tw), lambda b, g: (b, 0, g)),
        scratch_shapes=[state, state, state,
                        pltpu.VMEM((HEADS_PER_STEP, 5, QB, 2 * QB), F32),
                        pltpu.VMEM((HEADS_PER_STEP, 256, LANES), F32)],
        compiler_params=_params(2), name="prompt_attention",
    )(slopes, qs[0], ks[0], vs[0], qs[1], ks[1], vs[1])


def _sgu_kernel(g_ref, u_ref, w_ref, bt_ref, o_ref, wt_ref):
    @pl.when(pl.program_id(0) == 0)
    def _():
        r_i = lax.broadcasted_iota(jnp.int32, (CHUNK, CHUNK), 0)
        c_i = lax.broadcasted_iota(jnp.int32, (CHUNK, CHUNK), 1)
        for gi in range(N_GROUPS):
            wt_ref[gi] = jnp.where(r_i >= c_i, w_ref[gi], 0.0).astype(BF16)

    n_chunks = g_ref.shape[0] // CHUNK
    for gi in range(N_GROUPS):
        cols = slice(gi * HEAD_DIM, (gi + 1) * HEAD_DIM)
        g_wide = jnp.concatenate([g_ref[c * CHUNK:(c + 1) * CHUNK, cols] for c in range(n_chunks)],
                                 axis=1)
        gate = jnp.dot(wt_ref[gi], g_wide, preferred_element_type=F32)
        for c in range(n_chunks):
            rows = slice(c * CHUNK, (c + 1) * CHUNK)
            gate_c = gate[:, c * HEAD_DIM:(c + 1) * HEAD_DIM] + bt_ref[:, gi:gi + 1]
            o_ref[rows, cols] = (u_ref[rows, cols] * gate_c).astype(BF16)


SGU_CHUNKS_PER_STEP = 4


def prompt_sgu(g, u, w_spatial, b_t, layer):
    m = g.shape[0]
    tm = SGU_CHUNKS_PER_STEP * CHUNK
    row = pl.BlockSpec((tm, D_SGU), lambda c: (c, 0))
    return pl.pallas_call(
        _sgu_kernel,
        out_shape=jax.ShapeDtypeStruct((m, D_SGU), BF16),
        grid=(m // tm,),
        in_specs=[row, row,
                  pl.BlockSpec((None, N_GROUPS, CHUNK, CHUNK), lambda c: (layer, 0, 0, 0)),
                  pl.BlockSpec((None, CHUNK, N_GROUPS), lambda c: (layer, 0, 0))],
        out_specs=row,
        scratch_shapes=[pltpu.VMEM((N_GROUPS, CHUNK, CHUNK), BF16)],
        compiler_params=_params(1), name="prompt_sgu",
    )(g, u, w_spatial, b_t)


def _outproj_kernel(a_ref, ug_ref, as_ref, ugs_ref, w_ref, x_ref, xs_ref, o_ref, os_ref,
                    wb_ref, stage_ref, sem, *, layer):
    @pl.when(_is_first_row_tile())
    def _():
        tn = wb_ref.shape[1]
        _take_weight_tile(w_ref, layer, lambda j: j * tn, pl.program_id(0), pl.num_programs(0),
                          stage_ref, sem, wb_ref)

    def project(a, ug):
        z = jnp.dot(a, wb_ref[:D_ATT, :], preferred_element_type=F32)
        return z + jnp.dot(ug, wb_ref[D_ATT:, :], preferred_element_type=F32)

    o_ref[...] = x_ref[...] + project(a_ref[...], ug_ref[...])

    @pl.when(_is_last_row_tile())
    def _():
        os_ref[...] = xs_ref[...] + project(as_ref[...], ugs_ref[...])


def out_proj(att, ug, att_s, ug_s, w_out, x, xs, layer):
    m, ms = x.shape[0], xs.shape[0]
    tn, tm = TN_IN, TM_IN
    lhs = pl.BlockSpec((tm, D_ATT), lambda j, i: (i, 0))
    lhs_s = pl.BlockSpec((ms, D_ATT), lambda j, i: (0, 0))
    res = pl.BlockSpec((tm, tn), lambda j, i: (i, j))
    res_s = pl.BlockSpec((ms, tn), lambda j, i: (0, j))
    return pl.pallas_call(
        functools.partial(_outproj_kernel, layer=layer),
        out_shape=[jax.ShapeDtypeStruct((m, D_MODEL), F32), jax.ShapeDtypeStruct((ms, D_MODEL), F32)],
        grid=(D_MODEL // tn, m // tm),
        in_specs=[lhs, lhs, lhs_s, lhs_s, pl.BlockSpec(memory_space=pl.ANY), res, res_s],
        out_specs=[res, res_s],
        scratch_shapes=[pltpu.VMEM((D_MODEL, tn), BF16), pltpu.VMEM((D_MODEL, tn), F32),
                        pltpu.SemaphoreType.DMA],
        compiler_params=_params(2), name="out_proj",
    )(att, ug, att_s, ug_s, w_out, x, xs)


def _ffn1_kernel(h_ref, hs_ref, wg_ref, wu_ref, wd_ref, a_ref, as_ref, wdb_ref,
                 wgb_ref, wub_ref, gstage_ref, ustage_ref, sems, *, layer):
    wdb_ref[...] = wd_ref[...].astype(BF16)

    @pl.when(_is_first_row_tile())
    def _():
        tn = wgb_ref.shape[1]
        j, nj = pl.program_id(0), pl.num_programs(0)
        _take_weight_tile(wg_ref, layer, lambda jj: jj * tn, j, nj, gstage_ref, sems.at[0], wgb_ref)
        _take_weight_tile(wu_ref, layer, lambda jj: jj * tn, j, nj, ustage_ref, sems.at[1], wub_ref)

    def gated(h):
        g = jnp.dot(h, wgb_ref[...], preferred_element_type=F32)
        u = jnp.dot(h, wub_ref[...], preferred_element_type=F32)
        return (g / (1.0 + jnp.exp(-g)) * u).astype(BF16)

    a_ref[...] = gated(h_ref[...])

    @pl.when(_is_last_row_tile())
    def _():
        as_ref[...] = gated(hs_ref[...])


def ffn_gate_up(h, hs, w_gate, w_up, w_down, layer):
    m, k = h.shape
    ms = hs.shape[0]
    d_ff = w_gate.shape[2]
    tn, tm = TN_FFN, TM_FFN
    nj, ni = d_ff // tn, m // tm
    rb = d_ff // (nj * ni)
    assert rb * nj * ni == d_ff and rb % 16 == 0
    wspec = pl.BlockSpec(memory_space=pl.ANY)
    return pl.pallas_call(
        functools.partial(_ffn1_kernel, layer=layer),
        out_shape=[jax.ShapeDtypeStruct((m, d_ff), BF16), jax.ShapeDtypeStruct((ms, d_ff), BF16),
                   jax.ShapeDtypeStruct((d_ff, D_MODEL), BF16)],
        grid=(nj, ni),
        in_specs=[pl.BlockSpec((tm, k), lambda j, i: (i, 0)),
                  pl.BlockSpec((ms, k), lambda j, i: (0, 0)), wspec, wspec,
                  pl.BlockSpec((None, rb, D_MODEL), lambda j, i: (layer, j * ni + i, 0))],
        out_specs=[pl.BlockSpec((tm, tn), lambda j, i: (i, j)),
                   pl.BlockSpec((ms, tn), lambda j, i: (0, j)),
                   pl.BlockSpec((rb, D_MODEL), lambda j, i: (j * ni + i, 0))],
        scratch_shapes=[pltpu.VMEM((k, tn), BF16)] * 2 + [pltpu.VMEM((k, tn), F32)] * 2
                       + [pltpu.SemaphoreType.DMA((2,))],
        compiler_params=_params(2), name="ffn_gate_up",
    )(h, hs, w_gate, w_up, w_down)


def _ffn2_kernel(a_ref, as_ref, w_ref, x_ref, xs_ref, o_ref, os_ref):
    o_ref[...] = x_ref[...] + jnp.dot(a_ref[...], w_ref[...], preferred_element_type=F32)

    @pl.when(_is_last_row_tile())
    def _():
        os_ref[...] = xs_ref[...] + jnp.dot(as_ref[...], w_ref[...], preferred_element_type=F32)


def ffn_down(a, a_s, w_down_bf16, x, xs):
    m, k = a.shape
    ms = a_s.shape[0]
    tn, tm = TN_DOWN, TM_DOWN
    res = pl.BlockSpec((tm, tn), lambda j, i: (i, j))
    res_s = pl.BlockSpec((ms, tn), lambda j, i: (0, j))
    return pl.pallas_call(
        _ffn2_kernel,
        out_shape=[jax.ShapeDtypeStruct((m, D_MODEL), F32), jax.ShapeDtypeStruct((ms, D_MODEL), F32)],
        grid=(D_MODEL // tn, m // tm),
        in_specs=[pl.BlockSpec((tm, k), lambda j, i: (i, 0)),
                  pl.BlockSpec((ms, k), lambda j, i: (0, 0)),
                  pl.BlockSpec((k, tn), lambda j, i: (0, j)),
                  res, res_s],
        out_specs=[res, res_s],
        compiler_params=_params(2), name="ffn_down",
    )(a, a_s, w_down_bf16, x, xs)


SQ = 16


def _multiplicity(d):
    nonneg = d >= 0
    n = (nonneg & (d <= 128)).astype(F32)
    n = n + (nonneg & (d <= 512) & ((d & 3) == 0)).astype(F32)
    n = n + (nonneg & (d <= 2048) & ((d & 15) == 0)).astype(F32)
    return n


def _sattn_kernel(sl_ref, q_ref, kn_ref, vn_ref, kh_ref, vh_ref, kc_ref, vc_ref, o_ref,
                  acc, m_s, l_s, xk_ref, xv_ref, *, tc, wb, ts):
    c = pl.program_id(1)
    last = pl.num_programs(1) - 1

    @pl.when(c == 0)
    def _():
        m_s[...] = jnp.full(m_s.shape, -MASKED, F32)
        l_s[...] = jnp.zeros(l_s.shape, F32)
        acc[...] = jnp.zeros(acc.shape, F32)

    def pad_rows(x, n):
        return jnp.concatenate([x, jnp.zeros((n - x.shape[0], x.shape[1]), F32)], axis=0)

    def head_cols(h):
        return slice(h * HEAD_DIM, (h + 1) * HEAD_DIM)

    def query(h):
        return pad_rows(q_ref[:, head_cols(h)], SQ).astype(BF16)

    def update(key, value, d):
        mult = _multiplicity(d)
        dist = jnp.where(mult > 0.0, d.astype(F32), MASKED)
        n_rep = d.shape[1] // LANES
        scores = [lax.dot_general(query(h), key(h), (((1,), (1,)), ((), ())),
                                  preferred_element_type=F32) for h in range(N_HEADS)]
        staged = []
        for h, s in enumerate(scores):
            s = s * SCALE - sl_ref[h] * dist
            m_prev = m_s[h]
            m_new = jnp.maximum(m_prev, jnp.max(s, axis=-1, keepdims=True))
            p = mult * jnp.exp(s - jnp.concatenate([m_new] * n_rep, axis=1))
            staged.append((p, m_new, jnp.exp(m_prev - m_new)))
        for h, (p, m_new, alpha) in enumerate(staged):
            l_s[h] = alpha * l_s[h] + jnp.sum(p, axis=-1, keepdims=True)
            acc[h] = alpha * acc[h] + jnp.dot(p.astype(BF16), value(h), preferred_element_type=F32)
            m_s[h] = m_new

    def gathered(x_ref, n_tokens):
        return lambda h: x_ref[h % 4, pl.ds(h // 4, n_tokens, stride=4), :].astype(BF16)

    @pl.when(c < last)
    def _():
        n_half = tc // 2
        for r in range(4):
            rows = n_half * N_HEADS // 4
            xk_ref[r, :rows] = kh_ref[:, pl.ds(r, 8 * N_HEADS // 4, stride=4), :].reshape(rows, HEAD_DIM)
            xv_ref[r, :rows] = vh_ref[:, pl.ds(r, 8 * N_HEADS // 4, stride=4), :].reshape(rows, HEAD_DIM)
        t_h = lax.broadcasted_iota(jnp.int32, (SQ, n_half), 0)
        c_h = lax.broadcasted_iota(jnp.int32, (SQ, n_half), 1)
        token = c * tc + ((c_h >> 3) << 4) + (c_h & 7)
        update(gathered(xk_ref, n_half), gathered(xv_ref, n_half), (wb + t_h) - token)

    @pl.when(c == last)
    def _():
        t_c = lax.broadcasted_iota(jnp.int32, (SQ, tc), 0)
        c_c = lax.broadcasted_iota(jnp.int32, (SQ, tc), 1)
        d_cache = (wb + t_c) - (c * tc + c_c)
        for r in range(4):
            xk_ref[r] = kc_ref[pl.ds(r, 4 * tc, stride=4), :]
            xv_ref[r] = vc_ref[pl.ds(r, 4 * tc, stride=4), :]
        update(gathered(xk_ref, tc), gathered(xv_ref, tc), d_cache)

        t_n = lax.broadcasted_iota(jnp.int32, (SQ, LANES), 0)
        c_n = lax.broadcasted_iota(jnp.int32, (SQ, LANES), 1)
        d_new = jnp.where(c_n < ts, t_n - c_n, -1)
        update(lambda h: pad_rows(kn_ref[:, head_cols(h)], LANES).astype(BF16),
               lambda h: pad_rows(vn_ref[:, head_cols(h)], LANES).astype(BF16), d_new)
        for h in range(N_HEADS):
            o_ref[:, head_cols(h)] = (acc[h] / l_s[h])[:ts, :]


def sample_attention(slopes, q, k_new, v_new, cache_k, cache_v, layer):
    bs, ts, _ = q.shape
    wb = cache_k.shape[2] // N_HEADS
    tc = 512
    n_chunks = wb // tc
    assert wb == WINDOW_MAX and tc == 512 and ts <= 8 and wb % 16 == 0
    new = pl.BlockSpec((None, ts, D_ATT), lambda b, c: (b, 0, 0))
    group = 16 * N_HEADS
    halves = lambda a: a.reshape(a.shape[0], bs, wb // 16, group, HEAD_DIM)
    half = pl.BlockSpec((None, None, tc // 16, group // 2, HEAD_DIM),
                        lambda b, c: (layer, b, jnp.minimum(c, n_chunks - 2), 0, 0))
    near = pl.BlockSpec((None, None, tc * N_HEADS, HEAD_DIM), lambda b, c: (layer, b, n_chunks - 1, 0))
    return pl.pallas_call(
        functools.partial(_sattn_kernel, tc=tc, wb=wb, ts=ts),
        out_shape=jax.ShapeDtypeStruct((bs, ts, D_ATT), F32),
        grid=(bs, n_chunks),
        in_specs=[pl.BlockSpec(memory_space=pltpu.SMEM), new, new, new, half, half, near, near],
        out_specs=new,
        scratch_shapes=[pltpu.VMEM((N_HEADS, SQ, LANES), F32)] * 3
                       + [pltpu.VMEM((4, 4 * tc, HEAD_DIM), F32)] * 2,
        compiler_params=_params(2), name="sample_attention",
    )(slopes, q, k_new, v_new, halves(cache_k), halves(cache_v), cache_k, cache_v)


def _ssgu_kernel(att_ref, g_ref, u_ref, wt_ref, b_ref, att_o, ug_o, *, ts):
    att_o[...] = att_ref[...].astype(BF16)
    n = att_ref.shape[0]
    r_i = lax.broadcasted_iota(jnp.int32, (n, n), 0)
    c_i = lax.broadcasted_iota(jnp.int32, (n, n), 1)
    keep = ((r_i & -ts) == (c_i & -ts)) & ((c_i & (ts - 1)) <= (r_i & (ts - 1)))
    for gi in range(N_GROUPS):
        cols = slice(gi * HEAD_DIM, (gi + 1) * HEAD_DIM)
        w = jnp.where(keep, wt_ref[gi], 0.0).astype(BF16)
        gate = jnp.dot(w, g_ref[:, cols].astype(BF16), preferred_element_type=F32) + b_ref[:, gi:gi + 1]
        ug_o[:, cols] = (u_ref[:, cols] * gate).astype(BF16)


def sample_sgu(att, g, u, w_tiled, b_rows, ts):
    n = att.shape[0]
    return pl.pallas_call(
        functools.partial(_ssgu_kernel, ts=ts),
        out_shape=[jax.ShapeDtypeStruct((n, D_ATT), BF16), jax.ShapeDtypeStruct((n, D_SGU), BF16)],
        compiler_params=pltpu.CompilerParams(vmem_limit_bytes=VMEM_LIMIT),
        name="sample_sgu",
    )(att, g, u, w_tiled, b_rows)


def kernel(x_prompt, x_sample, cache_k_win, cache_v_win, norm1, w_in, q_gain, k_gain, sgu_gain,
           w_spatial, b_spatial, w_out, norm2, w_gate, w_up, w_down):
    bp, sp, _ = x_prompt.shape
    bs, ts, _ = x_sample.shape
    assert ts & (ts - 1) == 0
    depth = w_in.shape[0]
    wb = cache_k_win.shape[2]
    assert sp == WINDOW_MAX and ts <= 8 and wb == WINDOW_MAX
    mp, ms = bp * sp, bs * ts

    xp = x_prompt.reshape(mp, D_MODEL)
    xs = x_sample.reshape(ms, D_MODEL)
    cache_k = cache_k_win.reshape(depth, bs, wb * N_HEADS, HEAD_DIM)
    cache_v = cache_v_win.reshape(depth, bs, wb * N_HEADS, HEAD_DIM)
    slopes = jnp.exp2(-8.0 * jnp.arange(1, N_HEADS + 1, dtype=F32) / N_HEADS)
    b_t = jnp.swapaxes(b_spatial, 1, 2)

    k_stack = v_stack = None
    ks_rows, vs_rows, gs_rows = [], [], []
    for l in range(depth):
        qg = jnp.tile(q_gain[l], N_HEADS)
        kg = jnp.tile(k_gain[l], N_HEADS)
        sg = sgu_gain[l].reshape(D_SGU)

        hp = rms_rows(xp, norm1[l])
        hs = rms_rows(xs, norm1[l])
        sec = functools.partial(in_proj_section, hp, hs, w_in, l, seq=sp, depth=depth)
        q_s, *q_p = sec(0 * D_ATT, qg, want_f32=False, want_bf16=False, regroup=True)
        k_s, k_stack, *k_p = sec(1 * D_ATT, kg, want_f32=True, want_bf16=False, regroup=True,
                                 stack=k_stack)
        v_s, v_stack, *v_p = sec(2 * D_ATT, None, want_f32=True, want_bf16=False, regroup=True,
                                 stack=v_stack)
        u_s, u_p = sec(3 * D_ATT, None, want_f32=True, want_bf16=False, stack="unstacked")
        g_s, g_p = sec(3 * D_ATT + D_SGU, sg, want_f32=False, want_bf16=True)

        att_p = prompt_attention(slopes, q_p, k_p, v_p, bp, sp).reshape(mp, D_ATT)
        ug_p = prompt_sgu(g_p, u_p, w_spatial, b_t, l)

        r3 = lambda a: a.reshape(bs, ts, D_ATT)
        att_s = sample_attention(slopes, r3(q_s), r3(k_s), r3(v_s), cache_k, cache_v, l)
        w_tiled = jnp.tile(w_spatial[l][:, :ts, :ts], (1, bs, bs))
        b_rows = jnp.tile(b_spatial[l][:, :ts].T, (bs, 1))
        att_sb, ug_s = sample_sgu(att_s.reshape(ms, D_ATT), g_s, u_s, w_tiled, b_rows, ts)

        xp, xs = out_proj(att_p, ug_p, att_sb, ug_s, w_out, xp, xs, l)
        a_p, a_s, w_down_bf16 = ffn_gate_up(rms_rows(xp, norm2[l]), rms_rows(xs, norm2[l]),
                                            w_gate, w_up, w_down, l)
        xp, xs = ffn_down(a_p, a_s, w_down_bf16, xp, xs)

        ks_rows.append(k_s)
        vs_rows.append(v_s)
        gs_rows.append(g_s)

    heads = lambda a, b, t: a.reshape(depth, b, t, N_HEADS, HEAD_DIM)
    return (xp.reshape(bp, sp, D_MODEL), xs.reshape(bs, ts, D_MODEL),
            heads(k_stack, bp, sp), heads(v_stack, bp, sp),
            heads(jnp.stack(ks_rows), bs, ts), heads(jnp.stack(vs_rows), bs, ts),
            heads(jnp.stack(gs_rows), bs, ts))
```

```python
import functools

import jax
import jax.numpy as jnp
from jax import lax
from jax.experimental import pallas as pl
from jax.experimental.pallas import tpu as pltpu

F32 = jnp.float32
BF16 = jnp.bfloat16

D_MODEL = 4096
HEAD_DIM = 128
N_HEADS = 16
D_ATT = N_HEADS * HEAD_DIM
D_SGU = D_MODEL - D_ATT
N_GROUPS = D_SGU // HEAD_DIM
CHUNK = 128
WINDOW_MAX = 2048
RMS_EPS = 1e-6
SCALE = HEAD_DIM ** -0.5
LOG2E = 1.4426950408889634
MASKED = 1e30
LANES = 128
VMEM_LIMIT = 60 * 1024 * 1024

HEADS_PER_STEP = 4
QB = 128
SQ = 16
SGU_CHUNKS_PER_STEP = 4

TM_IN = 512
TN_IN = 1024
TM_FFN = 2048
TN_FFN = 256
TM_DOWN = 512
TN_DOWN = 512


def _params(n_axes):
    return pltpu.CompilerParams(dimension_semantics=("arbitrary",) * n_axes,
                                vmem_limit_bytes=VMEM_LIMIT)


def _cast_weight(w_ref, wb_ref):
    k = w_ref.shape[0]
    ck = 256
    def body(c, carry):
        r = pl.multiple_of(c * ck, ck)
        wb_ref[pl.ds(r, ck), :] = w_ref[pl.ds(r, ck), :].astype(BF16)
        return carry
    lax.fori_loop(0, k // ck, body, 0)


def _weight_tile_copy(w_hbm, layer, col, stage_ref, sem):
    col = pl.multiple_of(col, LANES)
    return pltpu.make_async_copy(w_hbm.at[layer, :, pl.ds(col, stage_ref.shape[1])], stage_ref, sem)


def _take_weight_tile(w_hbm, layer, col_of, j, nj, stage_ref, sem, wb_ref):
    @pl.when(j == 0)
    def _():
        _weight_tile_copy(w_hbm, layer, col_of(0), stage_ref, sem).start()
    _weight_tile_copy(w_hbm, layer, col_of(j), stage_ref, sem).wait()
    _cast_weight(stage_ref, wb_ref)

    @pl.when(j + 1 < nj)
    def _():
        _weight_tile_copy(w_hbm, layer, col_of(j + 1), stage_ref, sem).start()


def _is_first_row_tile():
    return pl.program_id(1) == 0


def _is_last_row_tile():
    return pl.program_id(1) == pl.num_programs(1) - 1


def _rms_rows_kernel(x_ref, g_ref, o_ref):
    x = x_ref[...]
    ms = jnp.mean(x * x, axis=-1, keepdims=True)
    o_ref[...] = (x * lax.rsqrt(ms + RMS_EPS) * g_ref[...]).astype(o_ref.dtype)


def rms_rows(x, gain):
    m, d = x.shape
    tm = min(m, 512)
    return pl.pallas_call(
        _rms_rows_kernel,
        out_shape=jax.ShapeDtypeStruct((m, d), BF16),
        grid=(m // tm,),
        in_specs=[pl.BlockSpec((tm, d), lambda i: (i, 0)),
                  pl.BlockSpec((1, d), lambda i: (0, 0))],
        out_specs=pl.BlockSpec((tm, d), lambda i: (i, 0)),
        compiler_params=_params(1),
        name="rms_rows",
    )(x, gain.reshape(1, d))


def _head_norm(z, gain):
    parts = []
    for hh in range(z.shape[1] // HEAD_DIM):
        cols = slice(hh * HEAD_DIM, (hh + 1) * HEAD_DIM)
        zh = z[:, cols]
        ms = jnp.mean(zh * zh, axis=-1, keepdims=True)
        parts.append(zh * lax.rsqrt(ms + RMS_EPS) * gain[:, cols])
    return jnp.concatenate(parts, axis=1)


def _inproj_kernel(*refs, norm, want_f32, want_bf16, regroup, has_alias, ni, n_tiles, layer, col0):
    it = iter(refs)
    h_ref = next(it)
    hs_ref = next(it)
    w_ref = next(it)
    gain_ref = next(it) if norm else None
    if has_alias:
        next(it)
    s_ref = next(it)
    f32_ref = next(it) if want_f32 else None
    bf_ref = next(it) if want_bf16 else None
    o4_ref = next(it) if regroup else None
    o16_ref = next(it) if regroup else None
    wb_ref = next(it)
    stage_ref = next(it)
    sem = next(it)
    z_refs = (next(it), next(it))
    z4_ref = next(it) if regroup else None

    s = pl.program_id(0)
    tm = h_ref.shape[0]
    n_slabs = wb_ref.shape[1] // LANES
    active = s < n_tiles

    def finish(z_ref):
        for c in range(n_slabs):
            cols = slice(c * LANES, (c + 1) * LANES)
            z = z_ref[c]
            if norm:
                z = _head_norm(z, gain_ref[:, cols])
            if want_f32:
                if len(f32_ref.shape) == 3:
                    nh = f32_ref.shape[1]
                    f32_ref.reshape(tm * nh, LANES)[pl.ds(c, tm, stride=nh), :] = z
                else:
                    f32_ref[:, cols] = z
            if want_bf16:
                bf_ref[:, cols] = z.astype(BF16)
            if regroup:
                if norm:
                    z_ref[c] = z
                for r in range(4):
                    z4 = z_ref[c, pl.ds(r, tm // 4, stride=4), :]
                    o4_ref[r, :, cols] = z4.astype(BF16)
                    z4_ref[c, r] = z4
                for r in range(4):
                    for c4 in range(4):
                        o16_ref[r + 4 * c4, :, cols] = (
                            z4_ref[c, r, pl.ds(c4, tm // 16, stride=4), :].astype(BF16))

    def multiply(z_ref):
        wide = 2 * LANES
        for c in range(wb_ref.shape[1] // wide):
            z = jnp.dot(h_ref[...], wb_ref[:, c * wide:(c + 1) * wide], preferred_element_type=F32)
            z_ref[2 * c] = z[:, :LANES]
            z_ref[2 * c + 1] = z[:, LANES:]

    @pl.when(s == 0)
    def _():
        z_refs[1][...] = jnp.zeros(z_refs[1].shape, F32)

    @pl.when(active & (s % ni == 0))
    def _():
        tn = wb_ref.shape[1]
        _take_weight_tile(w_ref, layer, lambda j: col0 + j * tn, s // ni, n_tiles // ni,
                          stage_ref, sem, wb_ref)

    for par in range(2):
        @pl.when(active & (s % 2 == par))
        def _():
            finish(z_refs[1 - par])
            multiply(z_refs[par])

    @pl.when(s == n_tiles)
    def _():
        finish(z_refs[(n_tiles - 1) % 2])

    @pl.when(active & (s % ni == ni - 1))
    def _():
        z = jnp.dot(hs_ref[...], wb_ref[...], preferred_element_type=F32)
        if norm:
            z = _head_norm(z, gain_ref[...])
        s_ref[...] = z


def in_proj_section(h, hs, w_in, layer, col0, gain, *, want_f32, want_bf16, regroup=False,
                    seq=None, stack=None, depth=1):
    m, k = h.shape
    ms = hs.shape[0]
    n_sec = D_ATT
    tn, tm = TN_IN, TM_IN
    ni, nj = m // tm, n_sec // tn
    n_tiles = ni * nj
    assert ni >= 2
    norm = gain is not None
    stacked = want_f32 and stack != "unstacked"
    has_alias = stacked and layer > 0

    def cur(s):
        t = jnp.minimum(s, n_tiles - 1)
        return t // ni, t % ni
    def prev(s):
        t = jnp.maximum(s - 1, 0)
        return t // ni, t % ni

    in_specs = [pl.BlockSpec((tm, k), lambda s: (cur(s)[1], 0)),
                pl.BlockSpec((ms, k), lambda s: (0, 0)),
                pl.BlockSpec(memory_space=pl.ANY)]
    args = [h, hs, w_in]
    if norm:
        in_specs.append(pl.BlockSpec((1, tn), lambda s: (0, prev(s)[0])))
        args.append(gain.reshape(1, n_sec))
    if has_alias:
        in_specs.append(pl.BlockSpec(memory_space=pl.ANY))
        args.append(stack)

    out_shape = [jax.ShapeDtypeStruct((ms, n_sec), F32)]
    out_specs = [pl.BlockSpec((ms, tn), lambda s: (0, cur(s)[0]))]
    if want_f32:
        row0 = layer * ni if stacked else 0
        if stacked:
            out_shape.append(jax.ShapeDtypeStruct((depth * m, n_sec // HEAD_DIM, HEAD_DIM), F32))
            out_specs.append(pl.BlockSpec((tm, tn // HEAD_DIM, HEAD_DIM),
                                          lambda s: (row0 + prev(s)[1], prev(s)[0], 0)))
        else:
            out_shape.append(jax.ShapeDtypeStruct((m, n_sec), F32))
            out_specs.append(pl.BlockSpec((tm, tn), lambda s: (prev(s)[1], prev(s)[0])))
    if want_bf16:
        out_shape.append(jax.ShapeDtypeStruct((m, n_sec), BF16))
        out_specs.append(pl.BlockSpec((tm, tn), lambda s: (prev(s)[1], prev(s)[0])))
    if regroup:
        b = m // seq
        tpb = seq // tm
        grouped = lambda s: (prev(s)[1] // tpb, 0, prev(s)[1] % tpb, prev(s)[0])
        out_shape.append(jax.ShapeDtypeStruct((b, 4, seq // 4, n_sec), BF16))
        out_specs.append(pl.BlockSpec((None, 4, tm // 4, tn), grouped))
        out_shape.append(jax.ShapeDtypeStruct((b, 16, seq // 16, n_sec), BF16))
        out_specs.append(pl.BlockSpec((None, 16, tm // 16, tn), grouped))
    scratch = ([pltpu.VMEM((k, tn), BF16), pltpu.VMEM((k, tn), F32), pltpu.SemaphoreType.DMA]
               + [pltpu.VMEM((tn // LANES, tm, LANES), F32)] * 2)
    if regroup:
        scratch.append(pltpu.VMEM((tn // LANES, 4, tm // 4, LANES), F32))

    kern = functools.partial(_inproj_kernel, norm=norm, want_f32=want_f32, want_bf16=want_bf16,
                             regroup=regroup, has_alias=has_alias, ni=ni, n_tiles=n_tiles,
                             layer=layer, col0=col0)
    return pl.pallas_call(
        kern, out_shape=out_shape, grid=(n_tiles + 1,), in_specs=in_specs, out_specs=out_specs,
        scratch_shapes=scratch,
        input_output_aliases=({len(args) - 1: 1} if has_alias else {}),
        compiler_params=_params(1), name="in_proj",
    )(*args)


def _attn_tiles(qkv, bias, prev):
    scores = [lax.dot_general(q, k, (((1,), (1,)), ((), ())), preferred_element_type=F32)
              for q, k, _ in qkv]
    probs = []
    for hh, s in enumerate(scores):
        s = s * (SCALE * LOG2E) - bias[hh]
        m_cur = jnp.max(s, axis=-1, keepdims=True)
        if prev is None:
            probs.append((jnp.exp2(s - m_cur).astype(BF16), jnp.broadcast_to(m_cur, (QB, LANES)), None))
        else:
            m_prev = prev[hh][1]
            m_new = jnp.maximum(m_prev, m_cur)
            p = jnp.exp2(s - jnp.concatenate([m_new] * (s.shape[1] // LANES), axis=1))
            probs.append((p.astype(BF16), m_new, jnp.exp2(m_prev - m_new)))
    new = []
    for hh, (p, m_new, alpha) in enumerate(probs):
        v = qkv[hh][2]
        v_ones = jnp.concatenate([v, jnp.ones(v.shape, BF16)], axis=1)
        pv = jnp.dot(p, v_ones, preferred_element_type=F32)
        if prev is None:
            new.append((pv[:, :LANES], m_new, pv[:, LANES:]))
        else:
            new.append((alpha * prev[hh][0] + pv[:, :LANES], m_new, alpha * prev[hh][2] + pv[:, LANES:]))
    return new


def _window_dist(d, dil):
    return jnp.where((d >= 0) & (d <= 128), (d * dil).astype(F32), MASKED)


def _attn_kernel(sl_ref, q4, k4, v4, q16, k16, v16, o_ref, acc, m_s, l_s, bias_s, nat_s, *, seq):
    g = pl.program_id(1)
    s4 = seq // 4
    state = (acc, m_s, l_s)

    o_i = lax.broadcasted_iota(jnp.int32, (QB, 2 * QB), 0)
    c_i = lax.broadcasted_iota(jnp.int32, (QB, 2 * QB), 1)
    d1 = 4 * ((o_i & 31) - (c_i & 63)) + ((o_i >> 5) - (c_i >> 6))
    heads = range(HEADS_PER_STEP)
    tables = (_window_dist(d1, 1),
              _window_dist(d1 + 128, 1),
              _window_dist(o_i - c_i, 4),
              _window_dist(o_i - c_i + 128, 4),
              _window_dist(o_i - c_i, 16))
    for hh in heads:
        slope = sl_ref[g * HEADS_PER_STEP + hh] * LOG2E
        for t, dist in enumerate(tables):
            bias_s[hh, t] = slope * dist

    def load_state(rows):
        return [tuple(ref[hh, rows, :] for ref in state) for hh in heads]

    def store_state(rows_list, new):
        for hh in heads:
            for ref, val in zip(state, new[hh]):
                n = val.shape[0] // len(rows_list)
                for idx, rows in enumerate(rows_list):
                    ref[hh, rows, :] = val[idx * n:(idx + 1) * n, :]

    def head_cols(hh):
        return slice(hh * HEAD_DIM, (hh + 1) * HEAD_DIM)

    def bias(table, n_keys=2 * QB):
        return [bias_s[hh, table, :, :n_keys] for hh in heads]

    def body1(qb, carry):
        t0 = pl.multiple_of(qb * 32, 32)
        kt0 = pl.multiple_of(jnp.maximum(qb - 1, 0) * 32, 32)
        def slabs(ref, start, n, hh):
            return jnp.concatenate([ref[r, pl.ds(start, n), head_cols(hh)] for r in range(4)], axis=0)
        qkv = [(slabs(q4, t0, 32, hh), slabs(k4, kt0, 64, hh), slabs(v4, kt0, 64, hh)) for hh in heads]
        new = _attn_tiles(qkv, bias(jnp.minimum(qb, 1)), None)
        store_state([pl.ds(r * s4 + t0, 32) for r in range(4)], new)
        return carry
    lax.fori_loop(0, seq // QB, body1, 0, unroll=4)

    n_qb4 = s4 // QB
    assert n_qb4 == 4
    def body4(idx, carry):
        r = idx >> 2
        qb = idx & 3
        q0 = pl.multiple_of(qb * QB, QB)
        k0 = pl.multiple_of(jnp.maximum(qb - 1, 0) * QB, QB)
        rows = pl.ds(r * s4 + q0, QB)
        qkv = [(q4[r, pl.ds(q0, QB), head_cols(hh)], k4[r, pl.ds(k0, 2 * QB), head_cols(hh)],
                v4[r, pl.ds(k0, 2 * QB), head_cols(hh)]) for hh in heads]
        new = _attn_tiles(qkv, bias(2 + jnp.minimum(qb, 1)), load_state(rows))
        store_state([rows], new)
        return carry
    lax.fori_loop(0, 4 * n_qb4, body4, 0, unroll=4)

    def body16(r16, carry):
        rows = pl.ds((r16 & 3) * s4 + (r16 >> 2), QB, stride=4)
        qkv = [(q16[r16, :, head_cols(hh)], k16[r16, :, head_cols(hh)], v16[r16, :, head_cols(hh)])
               for hh in heads]
        new = _attn_tiles(qkv, bias(4, QB), load_state(rows))
        store_state([rows], new)
        return carry
    lax.fori_loop(0, 16, body16, 0, unroll=4)

    tn = nat_s.shape[1] // 4
    def body_out(c, carry):
        t0 = pl.multiple_of(c * tn, tn)
        for hh in range(HEADS_PER_STEP):
            for r in range(4):
                rows = pl.ds(r * s4 + t0, tn)
                nat_s[hh, pl.ds(r, tn, stride=4), :] = acc[hh, rows, :] / l_s[hh, rows, :]
            o_ref[pl.ds(pl.multiple_of(4 * t0, 4 * tn), 4 * tn), head_cols(hh)] = (
                nat_s[hh].astype(o_ref.dtype))
        return carry
    lax.fori_loop(0, s4 // tn, body_out, 0)


def prompt_attention(slopes, qs, ks, vs, batch, seq):
    assert seq // 16 == QB and seq % (4 * QB) == 0
    tw = HEADS_PER_STEP * HEAD_DIM
    r4 = pl.BlockSpec((None, 4, seq // 4, tw), lambda b, g: (b, 0, 0, g))
    r16 = pl.BlockSpec((None, 16, seq // 16, tw), lambda b, g: (b, 0, 0, g))
    state = pltpu.VMEM((HEADS_PER_STEP, seq, LANES), F32)
    return pl.pallas_call(
        functools.partial(_attn_kernel, seq=seq),
        out_shape=jax.ShapeDtypeStruct((batch, seq, D_ATT), BF16),
        grid=(batch, N_HEADS // HEADS_PER_STEP),
        in_specs=[pl.BlockSpec(memory_space=pltpu.SMEM), r4, r4, r4, r16, r16, r16],
        out_specs=pl.BlockSpec((None, seq, tw), lambda b, g: (b, 0, g)),
        scratch_shapes=[state, state, state,
                        pltpu.VMEM((HEADS_PER_STEP, 5, QB, 2 * QB), F32),
                        pltpu.VMEM((HEADS_PER_STEP, 256, LANES), F32)],
        compiler_params=_params(2), name="prompt_attention",
    )(slopes, qs[0], ks[0], vs[0], qs[1], ks[1], vs[1])


def _sgu_kernel(g_ref, u_ref, w_ref, bt_ref, o_ref, wt_ref):
    @pl.when(pl.program_id(0) == 0)
    def _():
        r_i = lax.broadcasted_iota(jnp.int32, (CHUNK, CHUNK), 0)
        c_i = lax.broadcasted_iota(jnp.int32, (CHUNK, CHUNK), 1)
        for gi in range(N_GROUPS):
            wt_ref[gi] = jnp.where(r_i >= c_i, w_ref[gi], 0.0).astype(BF16)

    n_chunks = g_ref.shape[0] // CHUNK
    for gi in range(N_GROUPS):
        cols = slice(gi * HEAD_DIM, (gi + 1) * HEAD_DIM)
        g_wide = jnp.concatenate([g_ref[c * CHUNK:(c + 1) * CHUNK, cols] for c in range(n_chunks)],
                                 axis=1)
        gate = jnp.dot(wt_ref[gi], g_wide, preferred_element_type=F32)
        for c in range(n_chunks):
            rows = slice(c * CHUNK, (c + 1) * CHUNK)
            gate_c = gate[:, c * HEAD_DIM:(c + 1) * HEAD_DIM] + bt_ref[:, gi:gi + 1]
            o_ref[rows, cols] = (u_ref[rows, cols] * gate_c).astype(BF16)


def prompt_sgu(g, u, w_spatial, b_t, layer):
    m = g.shape[0]
    tm = SGU_CHUNKS_PER_STEP * CHUNK
    row = pl.BlockSpec((tm, D_SGU), lambda c: (c, 0))
    return pl.pallas_call(
        _sgu_kernel,
        out_shape=jax.ShapeDtypeStruct((m, D_SGU), BF16),
        grid=(m // tm,),
        in_specs=[row, row,
                  pl.BlockSpec((None, N_GROUPS, CHUNK, CHUNK), lambda c: (layer, 0, 0, 0)),
                  pl.BlockSpec((None, CHUNK, N_GROUPS), lambda c: (layer, 0, 0))],
        out_specs=row,
        scratch_shapes=[pltpu.VMEM((N_GROUPS, CHUNK, CHUNK), BF16)],
        compiler_params=_params(1), name="prompt_sgu",
    )(g, u, w_spatial, b_t)


def _outproj_kernel(a_ref, ug_ref, as_ref, ugs_ref, w_ref, x_ref, xs_ref, o_ref, os_ref,
                    wb_ref, stage_ref, sem, *, layer):
    @pl.when(_is_first_row_tile())
    def _():
        tn = wb_ref.shape[1]
        _take_weight_tile(w_ref, layer, lambda j: j * tn, pl.program_id(0), pl.num_programs(0),
                          stage_ref, sem, wb_ref)

    def project(a, ug):
        z = jnp.dot(a, wb_ref[:D_ATT, :], preferred_element_type=F32)
        return z + jnp.dot(ug, wb_ref[D_ATT:, :], preferred_element_type=F32)

    o_ref[...] = x_ref[...] + project(a_ref[...], ug_ref[...])

    @pl.when(_is_last_row_tile())
    def _():
        os_ref[...] = xs_ref[...] + project(as_ref[...], ugs_ref[...])


def out_proj(att, ug, att_s, ug_s, w_out, x, xs, layer):
    m, ms = x.shape[0], xs.shape[0]
    tn, tm = TN_IN, TM_IN
    lhs = pl.BlockSpec((tm, D_ATT), lambda j, i: (i, 0))
    lhs_s = pl.BlockSpec((ms, D_ATT), lambda j, i: (0, 0))
    res = pl.BlockSpec((tm, tn), lambda j, i: (i, j))
    res_s = pl.BlockSpec((ms, tn), lambda j, i: (0, j))
    return pl.pallas_call(
        functools.partial(_outproj_kernel, layer=layer),
        out_shape=[jax.ShapeDtypeStruct((m, D_MODEL), F32), jax.ShapeDtypeStruct((ms, D_MODEL), F32)],
        grid=(D_MODEL // tn, m // tm),
        in_specs=[lhs, lhs, lhs_s, lhs_s, pl.BlockSpec(memory_space=pl.ANY), res, res_s],
        out_specs=[res, res_s],
        scratch_shapes=[pltpu.VMEM((D_MODEL, tn), BF16), pltpu.VMEM((D_MODEL, tn), F32),
                        pltpu.SemaphoreType.DMA],
        compiler_params=_params(2), name="out_proj",
    )(att, ug, att_s, ug_s, w_out, x, xs)


def _ffn1_kernel(h_ref, hs_ref, wg_ref, wu_ref, wd_ref, a_ref, as_ref, wdb_ref,
                 wgb_ref, wub_ref, gstage_ref, ustage_ref, sems, *, layer):
    wdb_ref[...] = wd_ref[...].astype(BF16)

    @pl.when(_is_first_row_tile())
    def _():
        tn = wgb_ref.shape[1]
        j, nj = pl.program_id(0), pl.num_programs(0)
        _take_weight_tile(wg_ref, layer, lambda jj: jj * tn, j, nj, gstage_ref, sems.at[0], wgb_ref)
        _take_weight_tile(wu_ref, layer, lambda jj: jj * tn, j, nj, ustage_ref, sems.at[1], wub_ref)

    def gated(h):
        g = jnp.dot(h, wgb_ref[...], preferred_element_type=F32)
        u = jnp.dot(h, wub_ref[...], preferred_element_type=F32)
        return (g / (1.0 + jnp.exp(-g)) * u).astype(BF16)

    a_ref[...] = gated(h_ref[...])

    @pl.when(_is_last_row_tile())
    def _():
        as_ref[...] = gated(hs_ref[...])


def ffn_gate_up(h, hs, w_gate, w_up, w_down, layer):
    m, k = h.shape
    ms = hs.shape[0]
    d_ff = w_gate.shape[2]
    tn, tm = TN_FFN, TM_FFN
    nj, ni = d_ff // tn, m // tm
    rb = d_ff // (nj * ni)
    assert rb * nj * ni == d_ff and rb % 16 == 0
    wspec = pl.BlockSpec(memory_space=pl.ANY)
    return pl.pallas_call(
        functools.partial(_ffn1_kernel, layer=layer),
        out_shape=[jax.ShapeDtypeStruct((m, d_ff), BF16), jax.ShapeDtypeStruct((ms, d_ff), BF16),
                   jax.ShapeDtypeStruct((d_ff, D_MODEL), BF16)],
        grid=(nj, ni),
        in_specs=[pl.BlockSpec((tm, k), lambda j, i: (i, 0)),
                  pl.BlockSpec((ms, k), lambda j, i: (0, 0)), wspec, wspec,
                  pl.BlockSpec((None, rb, D_MODEL), lambda j, i: (layer, j * ni + i, 0))],
        out_specs=[pl.BlockSpec((tm, tn), lambda j, i: (i, j)),
                   pl.BlockSpec((ms, tn), lambda j, i: (0, j)),
                   pl.BlockSpec((rb, D_MODEL), lambda j, i: (j * ni + i, 0))],
        scratch_shapes=[pltpu.VMEM((k, tn), BF16)] * 2 + [pltpu.VMEM((k, tn), F32)] * 2
                       + [pltpu.SemaphoreType.DMA((2,))],
        compiler_params=_params(2), name="ffn_gate_up",
    )(h, hs, w_gate, w_up, w_down)


def _ffn2_kernel(a_ref, as_ref, w_ref, x_ref, xs_ref, o_ref, os_ref):
    o_ref[...] = x_ref[...] + jnp.dot(a_ref[...], w_ref[...], preferred_element_type=F32)

    @pl.when(_is_last_row_tile())
    def _():
        os_ref[...] = xs_ref[...] + jnp.dot(as_ref[...], w_ref[...], preferred_element_type=F32)


def ffn_down(a, a_s, w_down_bf16, x, xs):
    m, k = a.shape
    ms = a_s.shape[0]
    tn, tm = TN_DOWN, TM_DOWN
    res = pl.BlockSpec((tm, tn), lambda j, i: (i, j))
    res_s = pl.BlockSpec((ms, tn), lambda j, i: (0, j))
    return pl.pallas_call(
        _ffn2_kernel,
        out_shape=[jax.ShapeDtypeStruct((m, D_MODEL), F32), jax.ShapeDtypeStruct((ms, D_MODEL), F32)],
        grid=(D_MODEL // tn, m // tm),
        in_specs=[pl.BlockSpec((tm, k), lambda j, i: (i, 0)),
                  pl.BlockSpec((ms, k), lambda j, i: (0, 0)),
                  pl.BlockSpec((k, tn), lambda j, i: (0, j)),
                  res, res_s],
        out_specs=[res, res_s],
        compiler_params=_params(2), name="ffn_down",
    )(a, a_s, w_down_bf16, x, xs)


def _multiplicity(d):
    nonneg = d >= 0
    n = (nonneg & (d <= 128)).astype(F32)
    n = n + (nonneg & (d <= 512) & ((d & 3) == 0)).astype(F32)
    n = n + (nonneg & (d <= 2048) & ((d & 15) == 0)).astype(F32)
    return n


def _sattn_kernel(sl_ref, q_ref, kn_ref, vn_ref, kh_ref, vh_ref, kc_ref, vc_ref, o_ref,
                  acc, m_s, l_s, xk_ref, xv_ref, *, tc, wb, ts):
    c = pl.program_id(1)
    last = pl.num_programs(1) - 1

    @pl.when(c == 0)
    def _():
        m_s[...] = jnp.full(m_s.shape, -MASKED, F32)
        l_s[...] = jnp.zeros(l_s.shape, F32)
        acc[...] = jnp.zeros(acc.shape, F32)

    def pad_rows(x, n):
        return jnp.concatenate([x, jnp.zeros((n - x.shape[0], x.shape[1]), F32)], axis=0)

    def head_cols(h):
        return slice(h * HEAD_DIM, (h + 1) * HEAD_DIM)

    def query(h):
        return pad_rows(q_ref[:, head_cols(h)], SQ).astype(BF16)

    def update(key, value, d):
        mult = _multiplicity(d)
        dist = jnp.where(mult > 0.0, d.astype(F32), MASKED)
        n_rep = d.shape[1] // LANES
        scores = [lax.dot_general(query(h), key(h), (((1,), (1,)), ((), ())),
                                  preferred_element_type=F32) for h in range(N_HEADS)]
        staged = []
        for h, s in enumerate(scores):
            s = s * SCALE - sl_ref[h] * dist
            m_prev = m_s[h]
            m_new = jnp.maximum(m_prev, jnp.max(s, axis=-1, keepdims=True))
            p = mult * jnp.exp(s - jnp.concatenate([m_new] * n_rep, axis=1))
            staged.append((p, m_new, jnp.exp(m_prev - m_new)))
        for h, (p, m_new, alpha) in enumerate(staged):
            l_s[h] = alpha * l_s[h] + jnp.sum(p, axis=-1, keepdims=True)
            acc[h] = alpha * acc[h] + jnp.dot(p.astype(BF16), value(h), preferred_element_type=F32)
            m_s[h] = m_new

    def gathered(x_ref, n_tokens):
        return lambda h: x_ref[h % 4, pl.ds(h // 4, n_tokens, stride=4), :].astype(BF16)

    @pl.when(c < last)
    def _():
        n_half = tc // 2
        for r in range(4):
            rows = n_half * N_HEADS // 4
            xk_ref[r, :rows] = kh_ref[:, pl.ds(r, 8 * N_HEADS // 4, stride=4), :].reshape(rows, HEAD_DIM)
            xv_ref[r, :rows] = vh_ref[:, pl.ds(r, 8 * N_HEADS // 4, stride=4), :].reshape(rows, HEAD_DIM)
        t_h = lax.broadcasted_iota(jnp.int32, (SQ, n_half), 0)
        c_h = lax.broadcasted_iota(jnp.int32, (SQ, n_half), 1)
        token = c * tc + ((c_h >> 3) << 4) + (c_h & 7)
        update(gathered(xk_ref, n_half), gathered(xv_ref, n_half), (wb + t_h) - token)

    @pl.when(c == last)
    def _():
        t_c = lax.broadcasted_iota(jnp.int32, (SQ, tc), 0)
        c_c = lax.broadcasted_iota(jnp.int32, (SQ, tc), 1)
        d_cache = (wb + t_c) - (c * tc + c_c)
        for r in range(4):
            xk_ref[r] = kc_ref[pl.ds(r, 4 * tc, stride=4), :]
            xv_ref[r] = vc_ref[pl.ds(r, 4 * tc, stride=4), :]
        update(gathered(xk_ref, tc), gathered(xv_ref, tc), d_cache)

        t_n = lax.broadcasted_iota(jnp.int32, (SQ, LANES), 0)
        c_n = lax.broadcasted_iota(jnp.int32, (SQ, LANES), 1)
        d_new = jnp.where(c_n < ts, t_n - c_n, -1)
        update(lambda h: pad_rows(kn_ref[:, head_cols(h)], LANES).astype(BF16),
               lambda h: pad_rows(vn_ref[:, head_cols(h)], LANES).astype(BF16), d_new)
        for h in range(N_HEADS):
            o_ref[:, head_cols(h)] = (acc[h] / l_s[h])[:ts, :]


def sample_attention(slopes, q, k_new, v_new, cache_k, cache_v, layer):
    bs, ts, _ = q.shape
    wb = cache_k.shape[2] // N_HEADS
    tc = 512
    n_chunks = wb // tc
    assert wb == WINDOW_MAX and tc == 512 and ts <= 8 and wb % 16 == 0
    new = pl.BlockSpec((None, ts, D_ATT), lambda b, c: (b, 0, 0))
    group = 16 * N_HEADS
    halves = lambda a: a.reshape(a.shape[0], bs, wb // 16, group, HEAD_DIM)
    half = pl.BlockSpec((None, None, tc // 16, group // 2, HEAD_DIM),
                        lambda b, c: (layer, b, jnp.minimum(c, n_chunks - 2), 0, 0))
    near = pl.BlockSpec((None, None, tc * N_HEADS, HEAD_DIM), lambda b, c: (layer, b, n_chunks - 1, 0))
    return pl.pallas_call(
        functools.partial(_sattn_kernel, tc=tc, wb=wb, ts=ts),
        out_shape=jax.ShapeDtypeStruct((bs, ts, D_ATT), F32),
        grid=(bs, n_chunks),
        in_specs=[pl.BlockSpec(memory_space=pltpu.SMEM), new, new, new, half, half, near, near],
        out_specs=new,
        scratch_shapes=[pltpu.VMEM((N_HEADS, SQ, LANES), F32)] * 3
                       + [pltpu.VMEM((4, 4 * tc, HEAD_DIM), F32)] * 2,
        compiler_params=_params(2), name="sample_attention",
    )(slopes, q, k_new, v_new, halves(cache_k), halves(cache_v), cache_k, cache_v)


def _ssgu_kernel(att_ref, g_ref, u_ref, wt_ref, b_ref, att_o, ug_o, *, ts):
    att_o[...] = att_ref[...].astype(BF16)
    n = att_ref.shape[0]
    r_i = lax.broadcasted_iota(jnp.int32, (n, n), 0)
    c_i = lax.broadcasted_iota(jnp.int32, (n, n), 1)
    keep = ((r_i & -ts) == (c_i & -ts)) & ((c_i & (ts - 1)) <= (r_i & (ts - 1)))
    for gi in range(N_GROUPS):
        cols = slice(gi * HEAD_DIM, (gi + 1) * HEAD_DIM)
        w = jnp.where(keep, wt_ref[gi], 0.0).astype(BF16)
        gate = jnp.dot(w, g_ref[:, cols].astype(BF16), preferred_element_type=F32) + b_ref[:, gi:gi + 1]
        ug_o[:, cols] = (u_ref[:, cols] * gate).astype(BF16)


def sample_sgu(att, g, u, w_tiled, b_rows, ts):
    n = att.shape[0]
    return pl.pallas_call(
        functools.partial(_ssgu_kernel, ts=ts),
        out_shape=[jax.ShapeDtypeStruct((n, D_ATT), BF16), jax.ShapeDtypeStruct((n, D_SGU), BF16)],
        compiler_params=pltpu.CompilerParams(vmem_limit_bytes=VMEM_LIMIT),
        name="sample_sgu",
    )(att, g, u, w_tiled, b_rows)


def kernel(x_prompt, x_sample, cache_k_win, cache_v_win, norm1, w_in, q_gain, k_gain, sgu_gain,
           w_spatial, b_spatial, w_out, norm2, w_gate, w_up, w_down):
    bp, sp, _ = x_prompt.shape
    bs, ts, _ = x_sample.shape
    assert ts & (ts - 1) == 0
    depth = w_in.shape[0]
    wb = cache_k_win.shape[2]
    assert sp == WINDOW_MAX and ts <= 8 and wb == WINDOW_MAX
    mp, ms = bp * sp, bs * ts

    xp = x_prompt.reshape(mp, D_MODEL)
    xs = x_sample.reshape(ms, D_MODEL)
    cache_k = cache_k_win.reshape(depth, bs, wb * N_HEADS, HEAD_DIM)
    cache_v = cache_v_win.reshape(depth, bs, wb * N_HEADS, HEAD_DIM)
    slopes = jnp.exp2(-8.0 * jnp.arange(1, N_HEADS + 1, dtype=F32) / N_HEADS)
    b_t = jnp.swapaxes(b_spatial, 1, 2)

    k_stack = v_stack = None
    ks_rows, vs_rows, gs_rows = [], [], []
    for l in range(depth):
        qg = jnp.tile(q_gain[l], N_HEADS)
        kg = jnp.tile(k_gain[l], N_HEADS)
        sg = sgu_gain[l].reshape(D_SGU)

        hp = rms_rows(xp, norm1[l])
        hs = rms_rows(xs, norm1[l])
        sec = functools.partial(in_proj_section, hp, hs, w_in, l, seq=sp, depth=depth)
        q_s, *q_p = sec(0 * D_ATT, qg, want_f32=False, want_bf16=False, regroup=True)
        k_s, k_stack, *k_p = sec(1 * D_ATT, kg, want_f32=True, want_bf16=False, regroup=True,
                                 stack=k_stack)
        v_s, v_stack, *v_p = sec(2 * D_ATT, None, want_f32=True, want_bf16=False, regroup=True,
                                 stack=v_stack)
        u_s, u_p = sec(3 * D_ATT, None, want_f32=True, want_bf16=False, stack="unstacked")
        g_s, g_p = sec(3 * D_ATT + D_SGU, sg, want_f32=False, want_bf16=True)

        att_p = prompt_attention(slopes, q_p, k_p, v_p, bp, sp).reshape(mp, D_ATT)
        ug_p = prompt_sgu(g_p, u_p, w_spatial, b_t, l)

        r3 = lambda a: a.reshape(bs, ts, D_ATT)
        att_s = sample_attention(slopes, r3(q_s), r3(k_s), r3(v_s), cache_k, cache_v, l)
        w_tiled = jnp.tile(w_spatial[l][:, :ts, :ts], (1, bs, bs))
        b_rows = jnp.tile(b_spatial[l][:, :ts].T, (bs, 1))
        att_sb, ug_s = sample_sgu(att_s.reshape(ms, D_ATT), g_s, u_s, w_tiled, b_rows, ts)

        xp, xs = out_proj(att_p, ug_p, att_sb, ug_s, w_out, xp, xs, l)
        a_p, a_s, w_down_bf16 = ffn_gate_up(rms_rows(xp, norm2[l]), rms_rows(xs, norm2[l]),
                                            w_gate, w_up, w_down, l)
        xp, xs = ffn_down(a_p, a_s, w_down_bf16, xp, xs)

        ks_rows.append(k_s)
        vs_rows.append(v_s)
        gs_rows.append(g_s)

    heads = lambda a, b, t: a.reshape(depth, b, t, N_HEADS, HEAD_DIM)
    return (xp.reshape(bp, sp, D_MODEL), xs.reshape(bs, ts, D_MODEL),
            heads(k_stack, bp, sp), heads(v_stack, bp, sp),
            heads(jnp.stack(ks_rows), bs, ts), heads(jnp.stack(vs_rows), bs, ts),
            heads(jnp.stack(gs_rows), bs, ts))
```

```python
import functools

import jax
import jax.numpy as jnp
from jax import lax
from jax.experimental import pallas as pl
from jax.experimental.pallas import tpu as pltpu

F32 = jnp.float32
BF16 = jnp.bfloat16

D_MODEL = 4096
HEAD_DIM = 128
N_HEADS = 16
D_ATT = N_HEADS * HEAD_DIM
D_SGU = D_MODEL - D_ATT
N_GROUPS = D_SGU // HEAD_DIM
CHUNK = 128
WINDOW_MAX = 2048
RMS_EPS = 1e-6
SCALE = HEAD_DIM ** -0.5
LOG2E = 1.4426950408889634
MASKED = 1e30
LANES = 128
VMEM_LIMIT = 60 * 1024 * 1024

HEADS_PER_STEP = 4
QB = 128
SQ = 16
SGU_CHUNKS_PER_STEP = 4

TM_IN = 512
TN_IN = 1024
TM_FFN = 2048
TN_FFN = 256
TM_DOWN = 512
TN_DOWN = 512


def _params(n_axes):
    return pltpu.CompilerParams(dimension_semantics=("arbitrary",) * n_axes,
                                vmem_limit_bytes=VMEM_LIMIT)


def _cast_weight(w_ref, wb_ref):
    k = w_ref.shape[0]
    ck = 256
    def body(c, carry):
        r = pl.multiple_of(c * ck, ck)
        wb_ref[pl.ds(r, ck), :] = w_ref[pl.ds(r, ck), :].astype(BF16)
        return carry
    lax.fori_loop(0, k // ck, body, 0)


def _weight_tile_copy(w_hbm, layer, col, stage_ref, sem):
    col = pl.multiple_of(col, LANES)
    return pltpu.make_async_copy(w_hbm.at[layer, :, pl.ds(col, stage_ref.shape[1])], stage_ref, sem)


def _take_weight_tile(w_hbm, layer, col_of, j, nj, stage_ref, sem, wb_ref):
    @pl.when(j == 0)
    def _():
        _weight_tile_copy(w_hbm, layer, col_of(0), stage_ref, sem).start()
    _weight_tile_copy(w_hbm, layer, col_of(j), stage_ref, sem).wait()
    _cast_weight(stage_ref, wb_ref)

    @pl.when(j + 1 < nj)
    def _():
        _weight_tile_copy(w_hbm, layer, col_of(j + 1), stage_ref, sem).start()


def _is_first_row_tile():
    return pl.program_id(1) == 0


def _is_last_row_tile():
    return pl.program_id(1) == pl.num_programs(1) - 1


def _rms_rows_kernel(x_ref, g_ref, o_ref):
    x = x_ref[...]
    ms = jnp.mean(x * x, axis=-1, keepdims=True)
    o_ref[...] = (x * lax.rsqrt(ms + RMS_EPS) * g_ref[...]).astype(o_ref.dtype)


def rms_rows(x, gain):
    m, d = x.shape
    tm = min(m, 512)
    return pl.pallas_call(
        _rms_rows_kernel,
        out_shape=jax.ShapeDtypeStruct((m, d), BF16),
        grid=(m // tm,),
        in_specs=[pl.BlockSpec((tm, d), lambda i: (i, 0)),
                  pl.BlockSpec((1, d), lambda i: (0, 0))],
        out_specs=pl.BlockSpec((tm, d), lambda i: (i, 0)),
        compiler_params=_params(1),
        name="rms_rows",
    )(x, gain.reshape(1, d))


def _head_norm(z, gain):
    parts = []
    for hh in range(z.shape[1] // HEAD_DIM):
        cols = slice(hh * HEAD_DIM, (hh + 1) * HEAD_DIM)
        zh = z[:, cols]
        ms = jnp.mean(zh * zh, axis=-1, keepdims=True)
        parts.append(zh * lax.rsqrt(ms + RMS_EPS) * gain[:, cols])
    return jnp.concatenate(parts, axis=1)


def _inproj_kernel(*refs, norm, want_f32, want_bf16, regroup, has_alias, ni, n_tiles, layer, col0):
    it = iter(refs)
    h_ref = next(it)
    hs_ref = next(it)
    w_ref = next(it)
    gain_ref = next(it) if norm else None
    if has_alias:
        next(it)
    s_ref = next(it)
    f32_ref = next(it) if want_f32 else None
    bf_ref = next(it) if want_bf16 else None
    o4_ref = next(it) if regroup else None
    o16_ref = next(it) if regroup else None
    wb_ref = next(it)
    stage_ref = next(it)
    sem = next(it)
    z_refs = (next(it), next(it))
    z4_ref = next(it) if regroup else None

    s = pl.program_id(0)
    tm = h_ref.shape[0]
    n_slabs = wb_ref.shape[1] // LANES
    active = s < n_tiles

    def finish(z_ref):
        for c in range(n_slabs):
            cols = slice(c * LANES, (c + 1) * LANES)
            z = z_ref[c]
            if norm:
                z = _head_norm(z, gain_ref[:, cols])
            if want_f32:
                if len(f32_ref.shape) == 3:
                    nh = f32_ref.shape[1]
                    f32_ref.reshape(tm * nh, LANES)[pl.ds(c, tm, stride=nh), :] = z
                else:
                    f32_ref[:, cols] = z
            if want_bf16:
                bf_ref[:, cols] = z.astype(BF16)
            if regroup:
                if norm:
                    z_ref[c] = z
                for r in range(4):
                    z4 = z_ref[c, pl.ds(r, tm // 4, stride=4), :]
                    o4_ref[r, :, cols] = z4.astype(BF16)
                    z4_ref[c, r] = z4
                for r in range(4):
                    for c4 in range(4):
                        o16_ref[r + 4 * c4, :, cols] = (
                            z4_ref[c, r, pl.ds(c4, tm // 16, stride=4), :].astype(BF16))

    def multiply(z_ref):
        wide = 2 * LANES
        for c in range(wb_ref.shape[1] // wide):
            z = jnp.dot(h_ref[...], wb_ref[:, c * wide:(c + 1) * wide], preferred_element_type=F32)
            z_ref[2 * c] = z[:, :LANES]
            z_ref[2 * c + 1] = z[:, LANES:]

    @pl.when(s == 0)
    def _():
        z_refs[1][...] = jnp.zeros(z_refs[1].shape, F32)

    @pl.when(active & (s % ni == 0))
    def _():
        tn = wb_ref.shape[1]
        _take_weight_tile(w_ref, layer, lambda j: col0 + j * tn, s // ni, n_tiles // ni,
                          stage_ref, sem, wb_ref)

    for par in range(2):
        @pl.when(active & (s % 2 == par))
        def _():
            finish(z_refs[1 - par])
            multiply(z_refs[par])

    @pl.when(s == n_tiles)
    def _():
        finish(z_refs[(n_tiles - 1) % 2])

    @pl.when(active & (s % ni == ni - 1))
    def _():
        z = jnp.dot(hs_ref[...], wb_ref[...], preferred_element_type=F32)
        if norm:
            z = _head_norm(z, gain_ref[...])
        s_ref[...] = z


def in_proj_section(h, hs, w_in, layer, col0, gain, *, want_f32, want_bf16, regroup=False,
                    seq=None, stack=None, depth=1):
    m, k = h.shape
    ms = hs.shape[0]
    n_sec = D_ATT
    tn, tm = TN_IN, TM_IN
    ni, nj = m // tm, n_sec // tn
    n_tiles = ni * nj
    assert ni >= 2
    norm = gain is not None
    stacked = want_f32 and stack != "unstacked"
    has_alias = stacked and layer > 0

    def cur(s):
        t = jnp.minimum(s, n_tiles - 1)
        return t // ni, t % ni
    def prev(s):
        t = jnp.maximum(s - 1, 0)
        return t // ni, t % ni

    in_specs = [pl.BlockSpec((tm, k), lambda s: (cur(s)[1], 0)),
                pl.BlockSpec((ms, k), lambda s: (0, 0)),
                pl.BlockSpec(memory_space=pl.ANY)]
    args = [h, hs, w_in]
    if norm:
        in_specs.append(pl.BlockSpec((1, tn), lambda s: (0, prev(s)[0])))
        args.append(gain.reshape(1, n_sec))
    if has_alias:
        in_specs.append(pl.BlockSpec(memory_space=pl.ANY))
        args.append(stack)

    out_shape = [jax.ShapeDtypeStruct((ms, n_sec), F32)]
    out_specs = [pl.BlockSpec((ms, tn), lambda s: (0, cur(s)[0]))]
    if want_f32:
        row0 = layer * ni if stacked else 0
        if stacked:
            out_shape.append(jax.ShapeDtypeStruct((depth * m, n_sec // HEAD_DIM, HEAD_DIM), F32))
            out_specs.append(pl.BlockSpec((tm, tn // HEAD_DIM, HEAD_DIM),
                                          lambda s: (row0 + prev(s)[1], prev(s)[0], 0)))
        else:
            out_shape.append(jax.ShapeDtypeStruct((m, n_sec), F32))
            out_specs.append(pl.BlockSpec((tm, tn), lambda s: (prev(s)[1], prev(s)[0])))
    if want_bf16:
        out_shape.append(jax.ShapeDtypeStruct((m, n_sec), BF16))
        out_specs.append(pl.BlockSpec((tm, tn), lambda s: (prev(s)[1], prev(s)[0])))
    if regroup:
        b = m // seq
        tpb = seq // tm
        grouped = lambda s: (prev(s)[1] // tpb, 0, prev(s)[1] % tpb, prev(s)[0])
        out_shape.append(jax.ShapeDtypeStruct((b, 4, seq // 4, n_sec), BF16))
        out_specs.append(pl.BlockSpec((None, 4, tm // 4, tn), grouped))
        out_shape.append(jax.ShapeDtypeStruct((b, 16, seq // 16, n_sec), BF16))
        out_specs.append(pl.BlockSpec((None, 16, tm // 16, tn), grouped))
    scratch = ([pltpu.VMEM((k, tn), BF16), pltpu.VMEM((k, tn), F32), pltpu.SemaphoreType.DMA]
               + [pltpu.VMEM((tn // LANES, tm, LANES), F32)] * 2)
    if regroup:
        scratch.append(pltpu.VMEM((tn // LANES, 4, tm // 4, LANES), F32))

    kern = functools.partial(_inproj_kernel, norm=norm, want_f32=want_f32, want_bf16=want_bf16,
                             regroup=regroup, has_alias=has_alias, ni=ni, n_tiles=n_tiles,
                             layer=layer, col0=col0)
    return pl.pallas_call(
        kern, out_shape=out_shape, grid=(n_tiles + 1,), in_specs=in_specs, out_specs=out_specs,
        scratch_shapes=scratch,
        input_output_aliases=({len(args) - 1: 1} if has_alias else {}),
        compiler_params=_params(1), name="in_proj",
    )(*args)


def _attn_tiles(qkv, bias, prev):
    scores = [lax.dot_general(q, k, (((1,), (1,)), ((), ())), preferred_element_type=F32)
              for q, k, _ in qkv]
    probs = []
    for hh, s in enumerate(scores):
        s = s * (SCALE * LOG2E) - bias[hh]
        m_cur = jnp.max(s, axis=-1, keepdims=True)
        if prev is None:
            probs.append((jnp.exp2(s - m_cur).astype(BF16), jnp.broadcast_to(m_cur, (QB, LANES)), None))
        else:
            m_prev = prev[hh][1]
            m_new = jnp.maximum(m_prev, m_cur)
            p = jnp.exp2(s - jnp.concatenate([m_new] * (s.shape[1] // LANES), axis=1))
            probs.append((p.astype(BF16), m_new, jnp.exp2(m_prev - m_new)))
    new = []
    for hh, (p, m_new, alpha) in enumerate(probs):
        v = qkv[hh][2]
        v_ones = jnp.concatenate([v, jnp.ones(v.shape, BF16)], axis=1)
        pv = jnp.dot(p, v_ones, preferred_element_type=F32)
        if prev is None:
            new.append((pv[:, :LANES], m_new, pv[:, LANES:]))
        else:
            new.append((alpha * prev[hh][0] + pv[:, :LANES], m_new, alpha * prev[hh][2] + pv[:, LANES:]))
    return new


def _window_dist(d, dil):
    return jnp.where((d >= 0) & (d <= 128), (d * dil).astype(F32), MASKED)


def _attn_kernel(sl_ref, q4, k4, v4, q16, k16, v16, o_ref, acc, m_s, l_s, bias_s, nat_s, *, seq):
    g = pl.program_id(1)
    s4 = seq // 4
    state = (acc, m_s, l_s)

    o_i = lax.broadcasted_iota(jnp.int32, (QB, 2 * QB), 0)
    c_i = lax.broadcasted_iota(jnp.int32, (QB, 2 * QB), 1)
    d1 = 4 * ((o_i & 31) - (c_i & 63)) + ((o_i >> 5) - (c_i >> 6))
    heads = range(HEADS_PER_STEP)
    tables = (_window_dist(d1, 1),
              _window_dist(d1 + 128, 1),
              _window_dist(o_i - c_i, 4),
              _window_dist(o_i - c_i + 128, 4),
              _window_dist(o_i - c_i, 16))
    for hh in heads:
        slope = sl_ref[g * HEADS_PER_STEP + hh] * LOG2E
        for t, dist in enumerate(tables):
            bias_s[hh, t] = slope * dist

    def load_state(rows):
        return [tuple(ref[hh, rows, :] for ref in state) for hh in heads]

    def store_state(rows_list, new):
        for hh in heads:
            for ref, val in zip(state, new[hh]):
                n = val.shape[0] // len(rows_list)
                for idx, rows in enumerate(rows_list):
                    ref[hh, rows, :] = val[idx * n:(idx + 1) * n, :]

    def head_cols(hh):
        return slice(hh * HEAD_DIM, (hh + 1) * HEAD_DIM)

    def bias(table, n_keys=2 * QB):
        return [bias_s[hh, table, :, :n_keys] for hh in heads]

    def body1(qb, carry):
        t0 = pl.multiple_of(qb * 32, 32)
        kt0 = pl.multiple_of(jnp.maximum(qb - 1, 0) * 32, 32)
        def slabs(ref, start, n, hh):
            return jnp.concatenate([ref[r, pl.ds(start, n), head_cols(hh)] for r in range(4)], axis=0)
        qkv = [(slabs(q4, t0, 32, hh), slabs(k4, kt0, 64, hh), slabs(v4, kt0, 64, hh)) for hh in heads]
        new = _attn_tiles(qkv, bias(jnp.minimum(qb, 1)), None)
        store_state([pl.ds(r * s4 + t0, 32) for r in range(4)], new)
        return carry
    lax.fori_loop(0, seq // QB, body1, 0, unroll=4)

    n_qb4 = s4 // QB
    assert n_qb4 == 4
    def body4(idx, carry):
        r = idx >> 2
        qb = idx & 3
        q0 = pl.multiple_of(qb * QB, QB)
        k0 = pl.multiple_of(jnp.maximum(qb - 1, 0) * QB, QB)
        rows = pl.ds(r * s4 + q0, QB)
        qkv = [(q4[r, pl.ds(q0, QB), head_cols(hh)], k4[r, pl.ds(k0, 2 * QB), head_cols(hh)],
                v4[r, pl.ds(k0, 2 * QB), head_cols(hh)]) for hh in heads]
        new = _attn_tiles(qkv, bias(2 + jnp.minimum(qb, 1)), load_state(rows))
        store_state([rows], new)
        return carry
    lax.fori_loop(0, 4 * n_qb4, body4, 0, unroll=4)

    def body16(r16, carry):
        rows = pl.ds((r16 & 3) * s4 + (r16 >> 2), QB, stride=4)
        qkv = [(q16[r16, :, head_cols(hh)], k16[r16, :, head_cols(hh)], v16[r16, :, head_cols(hh)])
               for hh in heads]
        new = _attn_tiles(qkv, bias(4, QB), load_state(rows))
        store_state([rows], new)
        return carry
    lax.fori_loop(0, 16, body16, 0, unroll=4)

    tn = nat_s.shape[1] // 4
    def body_out(c, carry):
        t0 = pl.multiple_of(c * tn, tn)
        for hh in range(HEADS_PER_STEP):
            for r in range(4):
                rows = pl.ds(r * s4 + t0, tn)
                nat_s[hh, pl.ds(r, tn, stride=4), :] = acc[hh, rows, :] / l_s[hh, rows, :]
            o_ref[pl.ds(pl.multiple_of(4 * t0, 4 * tn), 4 * tn), head_cols(hh)] = (
                nat_s[hh].astype(o_ref.dtype))
        return carry
    lax.fori_loop(0, s4 // tn, body_out, 0)


def prompt_attention(slopes, qs, ks, vs, batch, seq):
    assert seq // 16 == QB and seq % (4 * QB) == 0
    tw = HEADS_PER_STEP * HEAD_DIM
    r4 = pl.BlockSpec((None, 4, seq // 4, tw), lambda b, g: (b, 0, 0, g))
    r16 = pl.BlockSpec((None, 16, seq // 16, tw), lambda b, g: (b, 0, 0, g))
    state = pltpu.VMEM((HEADS_PER_STEP, seq, LANES), F32)
    return pl.pallas_call(
        functools.partial(_attn_kernel, seq=seq),
        out_shape=jax.ShapeDtypeStruct((batch, seq, D_ATT), BF16),
        grid=(batch, N_HEADS // HEADS_PER_STEP),
        in_specs=[pl.BlockSpec(memory_space=pltpu.SMEM), r4, r4, r4, r16, r16, r16],
        out_specs=pl.BlockSpec((None, seq, tw), lambda b, g: (b, 0, g)),
        scratch_shapes=[state, state, state,
                        pltpu.VMEM((HEADS_PER_STEP, 5, QB, 2 * QB), F32),
                        pltpu.VMEM((HEADS_PER_STEP, 256, LANES), F32)],
        compiler_params=_params(2), name="prompt_attention",
    )(slopes, qs[0], ks[0], vs[0], qs[1], ks[1], vs[1])


def _sgu_kernel(g_ref, u_ref, w_ref, bt_ref, o_ref, wt_ref):
    @pl.when(pl.program_id(0) == 0)
    def _():
        r_i = lax.broadcasted_iota(jnp.int32, (CHUNK, CHUNK), 0)
        c_i = lax.broadcasted_iota(jnp.int32, (CHUNK, CHUNK), 1)
        for gi in range(N_GROUPS):
            wt_ref[gi] = jnp.where(r_i >= c_i, w_ref[gi], 0.0).astype(BF16)

    n_chunks = g_ref.shape[0] // CHUNK
    for gi in range(N_GROUPS):
        cols = slice(gi * HEAD_DIM, (gi + 1) * HEAD_DIM)
        g_wide = jnp.concatenate([g_ref[c * CHUNK:(c + 1) * CHUNK, cols] for c in range(n_chunks)],
                                 axis=1)
        gate = jnp.dot(wt_ref[gi], g_wide, preferred_element_type=F32)
        for c in range(n_chunks):
            rows = slice(c * CHUNK, (c + 1) * CHUNK)
            gate_c = gate[:, c * HEAD_DIM:(c + 1) * HEAD_DIM] + bt_ref[:, gi:gi + 1]
            o_ref[rows, cols] = (u_ref[rows, cols] * gate_c).astype(BF16)


def prompt_sgu(g, u, w_spatial, b_t, layer):
    m = g.shape[0]
    tm = SGU_CHUNKS_PER_STEP * CHUNK
    row = pl.BlockSpec((tm, D_SGU), lambda c: (c, 0))
    return pl.pallas_call(
        _sgu_kernel,
        out_shape=jax.ShapeDtypeStruct((m, D_SGU), BF16),
        grid=(m // tm,),
        in_specs=[row, row,
                  pl.BlockSpec((None, N_GROUPS, CHUNK, CHUNK), lambda c: (layer, 0, 0, 0)),
                  pl.BlockSpec((None, CHUNK, N_GROUPS), lambda c: (layer, 0, 0))],
        out_specs=row,
        scratch_shapes=[pltpu.VMEM((N_GROUPS, CHUNK, CHUNK), BF16)],
        compiler_params=_params(1), name="prompt_sgu",
    )(g, u, w_spatial, b_t)


def _outproj_kernel(a_ref, ug_ref, as_ref, ugs_ref, w_ref, x_ref, xs_ref, o_ref, os_ref,
                    wb_ref, stage_ref, sem, *, layer):
    @pl.when(_is_first_row_tile())
    def _():
        tn = wb_ref.shape[1]
        _take_weight_tile(w_ref, layer, lambda j: j * tn, pl.program_id(0), pl.num_programs(0),
                          stage_ref, sem, wb_ref)

    def project(a, ug):
        z = jnp.dot(a, wb_ref[:D_ATT, :], preferred_element_type=F32)
        return z + jnp.dot(ug, wb_ref[D_ATT:, :], preferred_element_type=F32)

    o_ref[...] = x_ref[...] + project(a_ref[...], ug_ref[...])

    @pl.when(_is_last_row_tile())
    def _():
        os_ref[...] = xs_ref[...] + project(as_ref[...], ugs_ref[...])


def out_proj(att, ug, att_s, ug_s, w_out, x, xs, layer):
    m, ms = x.shape[0], xs.shape[0]
    tn, tm = TN_IN, TM_IN
    lhs = pl.BlockSpec((tm, D_ATT), lambda j, i: (i, 0))
    lhs_s = pl.BlockSpec((ms, D_ATT), lambda j, i: (0, 0))
    res = pl.BlockSpec((tm, tn), lambda j, i: (i, j))
    res_s = pl.BlockSpec((ms, tn), lambda j, i: (0, j))
    return pl.pallas_call(
        functools.partial(_outproj_kernel, layer=layer),
        out_shape=[jax.ShapeDtypeStruct((m, D_MODEL), F32), jax.ShapeDtypeStruct((ms, D_MODEL), F32)],
        grid=(D_MODEL // tn, m // tm),
        in_specs=[lhs, lhs, lhs_s, lhs_s, pl.BlockSpec(memory_space=pl.ANY), res, res_s],
        out_specs=[res, res_s],
        scratch_shapes=[pltpu.VMEM((D_MODEL, tn), BF16), pltpu.VMEM((D_MODEL, tn), F32),
                        pltpu.SemaphoreType.DMA],
        compiler_params=_params(2), name="out_proj",
    )(att, ug, att_s, ug_s, w_out, x, xs)


def _ffn1_kernel(h_ref, hs_ref, wg_ref, wu_ref, wd_ref, a_ref, as_ref, wdb_ref,
                 wgb_ref, wub_ref, gstage_ref, ustage_ref, sems, *, layer):
    wdb_ref[...] = wd_ref[...].astype(BF16)

    @pl.when(_is_first_row_tile())
    def _():
        tn = wgb_ref.shape[1]
        j, nj = pl.program_id(0), pl.num_programs(0)
        _take_weight_tile(wg_ref, layer, lambda jj: jj * tn, j, nj, gstage_ref, sems.at[0], wgb_ref)
        _take_weight_tile(wu_ref, layer, lambda jj: jj * tn, j, nj, ustage_ref, sems.at[1], wub_ref)

    def gated(h):
        g = jnp.dot(h, wgb_ref[...], preferred_element_type=F32)
        u = jnp.dot(h, wub_ref[...], preferred_element_type=F32)
        return ((0.5 * g) * (1.0 + jnp.tanh(0.5 * g)) * u).astype(BF16)

    a_ref[...] = gated(h_ref[...])

    @pl.when(_is_last_row_tile())
    def _():
        as_ref[...] = gated(hs_ref[...])


def ffn_gate_up(h, hs, w_gate, w_up, w_down, layer):
    m, k = h.shape
    ms = hs.shape[0]
    d_ff = w_gate.shape[2]
    tn, tm = TN_FFN, TM_FFN
    nj, ni = d_ff // tn, m // tm
    rb = d_ff // (nj * ni)
    assert rb * nj * ni == d_ff and rb % 16 == 0
    wspec = pl.BlockSpec(memory_space=pl.ANY)
    return pl.pallas_call(
        functools.partial(_ffn1_kernel, layer=layer),
        out_shape=[jax.ShapeDtypeStruct((m, d_ff), BF16), jax.ShapeDtypeStruct((ms, d_ff), BF16),
                   jax.ShapeDtypeStruct((d_ff, D_MODEL), BF16)],
        grid=(nj, ni),
        in_specs=[pl.BlockSpec((tm, k), lambda j, i: (i, 0)),
                  pl.BlockSpec((ms, k), lambda j, i: (0, 0)), wspec, wspec,
                  pl.BlockSpec((None, rb, D_MODEL), lambda j, i: (layer, j * ni + i, 0))],
        out_specs=[pl.BlockSpec((tm, tn), lambda j, i: (i, j)),
                   pl.BlockSpec((ms, tn), lambda j, i: (0, j)),
                   pl.BlockSpec((rb, D_MODEL), lambda j, i: (j * ni + i, 0))],
        scratch_shapes=[pltpu.VMEM((k, tn), BF16)] * 2 + [pltpu.VMEM((k, tn), F32)] * 2
                       + [pltpu.SemaphoreType.DMA((2,))],
        compiler_params=_params(2), name="ffn_gate_up",
    )(h, hs, w_gate, w_up, w_down)


def _ffn2_kernel(a_ref, as_ref, w_ref, x_ref, xs_ref, o_ref, os_ref):
    o_ref[...] = x_ref[...] + jnp.dot(a_ref[...], w_ref[...], preferred_element_type=F32)

    @pl.when(_is_last_row_tile())
    def _():
        os_ref[...] = xs_ref[...] + jnp.dot(as_ref[...], w_ref[...], preferred_element_type=F32)


def ffn_down(a, a_s, w_down_bf16, x, xs):
    m, k = a.shape
    ms = a_s.shape[0]
    tn, tm = TN_DOWN, TM_DOWN
    res = pl.BlockSpec((tm, tn), lambda j, i: (i, j))
    res_s = pl.BlockSpec((ms, tn), lambda j, i: (0, j))
    return pl.pallas_call(
        _ffn2_kernel,
        out_shape=[jax.ShapeDtypeStruct((m, D_MODEL), F32), jax.ShapeDtypeStruct((ms, D_MODEL), F32)],
        grid=(D_MODEL // tn, m // tm),
        in_specs=[pl.BlockSpec((tm, k), lambda j, i: (i, 0)),
                  pl.BlockSpec((ms, k), lambda j, i: (0, 0)),
                  pl.BlockSpec((k, tn), lambda j, i: (0, j)),
                  res, res_s],
        out_specs=[res, res_s],
        compiler_params=_params(2), name="ffn_down",
    )(a, a_s, w_down_bf16, x, xs)


def _multiplicity(d):
    nonneg = d >= 0
    n = (nonneg & (d <= 128)).astype(F32)
    n = n + (nonneg & (d <= 512) & ((d & 3) == 0)).astype(F32)
    n = n + (nonneg & (d <= 2048) & ((d & 15) == 0)).astype(F32)
    return n


def _sattn_kernel(sl_ref, q_ref, kn_ref, vn_ref, kh_ref, vh_ref, kc_ref, vc_ref, o_ref,
                  acc, m_s, l_s, xk_ref, xv_ref, *, tc, wb, ts):
    c = pl.program_id(1)
    last = pl.num_programs(1) - 1

    @pl.when(c == 0)
    def _():
        m_s[...] = jnp.full(m_s.shape, -MASKED, F32)
        l_s[...] = jnp.zeros(l_s.shape, F32)
        acc[...] = jnp.zeros(acc.shape, F32)

    def pad_rows(x, n):
        return jnp.concatenate([x, jnp.zeros((n - x.shape[0], x.shape[1]), F32)], axis=0)

    def head_cols(h):
        return slice(h * HEAD_DIM, (h + 1) * HEAD_DIM)

    def query(h):
        return pad_rows(q_ref[:, head_cols(h)], SQ).astype(BF16)

    def update(key, value, d):
        mult = _multiplicity(d)
        dist = jnp.where(mult > 0.0, d.astype(F32), MASKED)
        n_rep = d.shape[1] // LANES
        scores = [lax.dot_general(query(h), key(h), (((1,), (1,)), ((), ())),
                                  preferred_element_type=F32) for h in range(N_HEADS)]
        staged = []
        for h, s in enumerate(scores):
            s = s * SCALE - sl_ref[h] * dist
            m_prev = m_s[h]
            m_new = jnp.maximum(m_prev, jnp.max(s, axis=-1, keepdims=True))
            p = mult * jnp.exp(s - jnp.concatenate([m_new] * n_rep, axis=1))
            staged.append((p, m_new, jnp.exp(m_prev - m_new)))
        for h, (p, m_new, alpha) in enumerate(staged):
            l_s[h] = alpha * l_s[h] + jnp.sum(p, axis=-1, keepdims=True)
            acc[h] = alpha * acc[h] + jnp.dot(p.astype(BF16), value(h), preferred_element_type=F32)
            m_s[h] = m_new

    def gathered(x_ref, n_tokens):
        return lambda h: x_ref[h % 4, pl.ds(h // 4, n_tokens, stride=4), :].astype(BF16)

    @pl.when(c < last)
    def _():
        n_half = tc // 2
        for r in range(4):
            rows = n_half * N_HEADS // 4
            xk_ref[r, :rows] = kh_ref[:, pl.ds(r, 8 * N_HEADS // 4, stride=4), :].reshape(rows, HEAD_DIM)
            xv_ref[r, :rows] = vh_ref[:, pl.ds(r, 8 * N_HEADS // 4, stride=4), :].reshape(rows, HEAD_DIM)
        t_h = lax.broadcasted_iota(jnp.int32, (SQ, n_half), 0)
        c_h = lax.broadcasted_iota(jnp.int32, (SQ, n_half), 1)
        token = c * tc + ((c_h >> 3) << 4) + (c_h & 7)
        update(gathered(xk_ref, n_half), gathered(xv_ref, n_half), (wb + t_h) - token)

    @pl.when(c == last)
    def _():
        t_c = lax.broadcasted_iota(jnp.int32, (SQ, tc), 0)
        c_c = lax.broadcasted_iota(jnp.int32, (SQ, tc), 1)
        d_cache = (wb + t_c) - (c * tc + c_c)
        for r in range(4):
            xk_ref[r] = kc_ref[pl.ds(r, 4 * tc, stride=4), :]
            xv_ref[r] = vc_ref[pl.ds(r, 4 * tc, stride=4), :]
        update(gathered(xk_ref, tc), gathered(xv_ref, tc), d_cache)

        t_n = lax.broadcasted_iota(jnp.int32, (SQ, LANES), 0)
        c_n = lax.broadcasted_iota(jnp.int32, (SQ, LANES), 1)
        d_new = jnp.where(c_n < ts, t_n - c_n, -1)
        update(lambda h: pad_rows(kn_ref[:, head_cols(h)], LANES).astype(BF16),
               lambda h: pad_rows(vn_ref[:, head_cols(h)], LANES).astype(BF16), d_new)
        for h in range(N_HEADS):
            o_ref[:, head_cols(h)] = (acc[h] / l_s[h])[:ts, :]


def sample_attention(slopes, q, k_new, v_new, cache_k, cache_v, layer):
    bs, ts, _ = q.shape
    wb = cache_k.shape[2] // N_HEADS
    tc = 512
    n_chunks = wb // tc
    assert wb == WINDOW_MAX and tc == 512 and ts <= 8 and wb % 16 == 0
    new = pl.BlockSpec((None, ts, D_ATT), lambda b, c: (b, 0, 0))
    group = 16 * N_HEADS
    halves = lambda a: a.reshape(a.shape[0], bs, wb // 16, group, HEAD_DIM)
    half = pl.BlockSpec((None, None, tc // 16, group // 2, HEAD_DIM),
                        lambda b, c: (layer, b, jnp.minimum(c, n_chunks - 2), 0, 0))
    near = pl.BlockSpec((None, None, tc * N_HEADS, HEAD_DIM), lambda b, c: (layer, b, n_chunks - 1, 0))
    return pl.pallas_call(
        functools.partial(_sattn_kernel, tc=tc, wb=wb, ts=ts),
        out_shape=jax.ShapeDtypeStruct((bs, ts, D_ATT), F32),
        grid=(bs, n_chunks),
        in_specs=[pl.BlockSpec(memory_space=pltpu.SMEM), new, new, new, half, half, near, near],
        out_specs=new,
        scratch_shapes=[pltpu.VMEM((N_HEADS, SQ, LANES), F32)] * 3
                       + [pltpu.VMEM((4, 4 * tc, HEAD_DIM), F32)] * 2,
        compiler_params=_params(2), name="sample_attention",
    )(slopes, q, k_new, v_new, halves(cache_k), halves(cache_v), cache_k, cache_v)


def _ssgu_kernel(att_ref, g_ref, u_ref, wt_ref, b_ref, att_o, ug_o, *, ts):
    att_o[...] = att_ref[...].astype(BF16)
    n = att_ref.shape[0]
    r_i = lax.broadcasted_iota(jnp.int32, (n, n), 0)
    c_i = lax.broadcasted_iota(jnp.int32, (n, n), 1)
    keep = ((r_i & -ts) == (c_i & -ts)) & ((c_i & (ts - 1)) <= (r_i & (ts - 1)))
    for gi in range(N_GROUPS):
        cols = slice(gi * HEAD_DIM, (gi + 1) * HEAD_DIM)
        w = jnp.where(keep, wt_ref[gi], 0.0).astype(BF16)
        gate = jnp.dot(w, g_ref[:, cols].astype(BF16), preferred_element_type=F32) + b_ref[:, gi:gi + 1]
        ug_o[:, cols] = (u_ref[:, cols] * gate).astype(BF16)


def sample_sgu(att, g, u, w_tiled, b_rows, ts):
    n = att.shape[0]
    return pl.pallas_call(
        functools.partial(_ssgu_kernel, ts=ts),
        out_shape=[jax.ShapeDtypeStruct((n, D_ATT), BF16), jax.ShapeDtypeStruct((n, D_SGU), BF16)],
        compiler_params=pltpu.CompilerParams(vmem_limit_bytes=VMEM_LIMIT),
        name="sample_sgu",
    )(att, g, u, w_tiled, b_rows)


def kernel(x_prompt, x_sample, cache_k_win, cache_v_win, norm1, w_in, q_gain, k_gain, sgu_gain,
           w_spatial, b_spatial, w_out, norm2, w_gate, w_up, w_down):
    bp, sp, _ = x_prompt.shape
    bs, ts, _ = x_sample.shape
    assert ts & (ts - 1) == 0
    depth = w_in.shape[0]
    wb = cache_k_win.shape[2]
    assert sp == WINDOW_MAX and ts <= 8 and wb == WINDOW_MAX
    mp, ms = bp * sp, bs * ts

    xp = x_prompt.reshape(mp, D_MODEL)
    xs = x_sample.reshape(ms, D_MODEL)
    cache_k = cache_k_win.reshape(depth, bs, wb * N_HEADS, HEAD_DIM)
    cache_v = cache_v_win.reshape(depth, bs, wb * N_HEADS, HEAD_DIM)
    slopes = jnp.exp2(-8.0 * jnp.arange(1, N_HEADS + 1, dtype=F32) / N_HEADS)
    b_t = jnp.swapaxes(b_spatial, 1, 2)

    k_stack = v_stack = None
    ks_rows, vs_rows, gs_rows = [], [], []
    for l in range(depth):
        qg = jnp.tile(q_gain[l], N_HEADS)
        kg = jnp.tile(k_gain[l], N_HEADS)
        sg = sgu_gain[l].reshape(D_SGU)

        hp = rms_rows(xp, norm1[l])
        hs = rms_rows(xs, norm1[l])
        sec = functools.partial(in_proj_section, hp, hs, w_in, l, seq=sp, depth=depth)
        q_s, *q_p = sec(0 * D_ATT, qg, want_f32=False, want_bf16=False, regroup=True)
        k_s, k_stack, *k_p = sec(1 * D_ATT, kg, want_f32=True, want_bf16=False, regroup=True,
                                 stack=k_stack)
        v_s, v_stack, *v_p = sec(2 * D_ATT, None, want_f32=True, want_bf16=False, regroup=True,
                                 stack=v_stack)
        u_s, u_p = sec(3 * D_ATT, None, want_f32=True, want_bf16=False, stack="unstacked")
        g_s, g_p = sec(3 * D_ATT + D_SGU, sg, want_f32=False, want_bf16=True)

        att_p = prompt_attention(slopes, q_p, k_p, v_p, bp, sp).reshape(mp, D_ATT)
        ug_p = prompt_sgu(g_p, u_p, w_spatial, b_t, l)

        r3 = lambda a: a.reshape(bs, ts, D_ATT)
        att_s = sample_attention(slopes, r3(q_s), r3(k_s), r3(v_s), cache_k, cache_v, l)
        w_tiled = jnp.tile(w_spatial[l][:, :ts, :ts], (1, bs, bs))
        b_rows = jnp.tile(b_spatial[l][:, :ts].T, (bs, 1))
        att_sb, ug_s = sample_sgu(att_s.reshape(ms, D_ATT), g_s, u_s, w_tiled, b_rows, ts)

        xp, xs = out_proj(att_p, ug_p, att_sb, ug_s, w_out, xp, xs, l)
        a_p, a_s, w_down_bf16 = ffn_gate_up(rms_rows(xp, norm2[l]), rms_rows(xs, norm2[l]),
                                            w_gate, w_up, w_down, l)
        xp, xs = ffn_down(a_p, a_s, w_down_bf16, xp, xs)

        ks_rows.append(k_s)
        vs_rows.append(v_s)
        gs_rows.append(g_s)

    heads = lambda a, b, t: a.reshape(depth, b, t, N_HEADS, HEAD_DIM)
    return (xp.reshape(bp, sp, D_MODEL), xs.reshape(bs, ts, D_MODEL),
            heads(k_stack, bp, sp), heads(v_stack, bp, sp),
            heads(jnp.stack(ks_rows), bs, ts), heads(jnp.stack(vs_rows), bs, ts),
            heads(jnp.stack(gs_rows), bs, ts))
```
